```python
import math
import jax
import jax.numpy as jnp
from jax import lax
import numpy as np

D_MODEL = 1024
BATCH = 4
SEQ = 8192
DEPTH = 2

CHUNK = 64
Q_BLOCK = 128
NORM_EPS = 1e-6
MAX_STREAM_OFFSET = 4096

GDN_HEADS = 8
GDN_DK = 128
GDN_DV = 128
CONV_WIDTH = 4
GDN_QKV = 2 * GDN_HEADS * GDN_DK + GDN_HEADS * GDN_DV
GDN_Z = GDN_HEADS * GDN_DV

MLA_HEADS = 8
MLA_Q_LORA = 384
MLA_KV_LORA = 256
MLA_NOPE = 128
MLA_ROPE = 64
MLA_V = 128
MLA_QK_DIM = MLA_NOPE + MLA_ROPE
ROPE_BASE = 10000.0

D_FF = 4 * D_MODEL

N_BRANCHES = 2
PROJ_SIZES = (GDN_QKV, GDN_Z, GDN_HEADS, GDN_HEADS, MLA_Q_LORA, MLA_KV_LORA, MLA_ROPE, N_BRANCHES * D_MODEL)
PROJ_WIDTH = GDN_QKV + GDN_Z + 2 * GDN_HEADS + MLA_Q_LORA + MLA_KV_LORA + MLA_ROPE + N_BRANCHES * D_MODEL

kernel_name = "hybrid_gdn_mla_sandwich_trunk"


def _rms_norm(x, w):
    xf = x.astype(jnp.float32)
    y = xf * lax.rsqrt(jnp.mean(xf * xf, axis=-1, keepdims=True) + NORM_EPS)
    return (y * w.astype(jnp.float32)).astype(x.dtype)


def _l2_norm(x):
    return x * lax.rsqrt(jnp.sum(x * x, axis=-1, keepdims=True) + NORM_EPS)


def _split_cols(t, sizes):
    outs, start = [], 0
    for s in sizes:
        outs.append(t[..., start:start + s])
        start += s
    return outs


def _causal_conv(x, w):
    s = x.shape[1]
    xp = jnp.pad(x, ((0, 0), (CONV_WIDTH - 1, 0), (0, 0)))
    out = xp[:, 0:s] * w[0]
    for j in range(1, CONV_WIDTH):
        out = out + xp[:, j:j + s] * w[j]
    return out


def _rope(x, cos, sin):
    half = x.shape[-1] // 2
    x1, x2 = x[..., :half], x[..., half:]
    return jnp.concatenate([x1 * cos - x2 * sin, x2 * cos + x1 * sin], axis=-1)


def _gated_delta_rule(q, k, v, g, beta):
    out_dtype = v.dtype
    q, k, v, g, beta = (t.astype(jnp.float32) for t in (q, k, v, g, beta))
    b, s, h, dk = q.shape
    dv = v.shape[-1]
    n = s // CHUNK
    q = q * dk ** -0.5
    to_chunks = lambda t: t.reshape(b, n, CHUNK, h, t.shape[-1]).transpose(0, 3, 1, 2, 4)
    q, k, v = to_chunks(q), to_chunks(k), to_chunks(v)
    g = g.reshape(b, n, CHUNK, h).transpose(0, 3, 1, 2)
    beta = beta.reshape(b, n, CHUNK, h).transpose(0, 3, 1, 2)
    gc = jnp.cumsum(g, axis=-1)
    idx = jnp.arange(CHUNK)
    incl = idx[:, None] >= idx[None, :]
    strict = idx[:, None] > idx[None, :]
    decay = jnp.exp(jnp.where(incl, gc[..., :, None] - gc[..., None, :], -jnp.inf))
    kb = k * beta[..., None]
    vb = v * beta[..., None]
    kk = jnp.einsum('bhncd,bhnmd->bhncm', kb, k) * decay
    lower = jnp.where(strict, kk, 0.0) + jnp.eye(CHUNK, dtype=kk.dtype)
    rhs = jnp.concatenate([vb, kb * jnp.exp(gc)[..., None]], axis=-1)
    sol = lax.linalg.triangular_solve(lower, rhs, left_side=True, lower=True, unit_diagonal=True)
    u, w = sol[..., :dv], sol[..., dv:]
    qk = jnp.einsum('bhncd,bhnmd->bhncm', q, k) * decay
    q_decay = q * jnp.exp(gc)[..., None]
    k_to_end = k * jnp.exp(gc[..., -1:] - gc)[..., None]
    g_end = jnp.exp(gc[..., -1])
    xs = tuple(jnp.moveaxis(t, 2, 0) for t in (qk, q_decay, k_to_end, u, w, g_end))

    def step(state, inp):
        qk_i, qd_i, kend_i, u_i, w_i, gend_i = inp
        v_new = u_i - jnp.einsum('bhck,bhkv->bhcv', w_i, state)
        o = jnp.einsum('bhck,bhkv->bhcv', qd_i, state) + jnp.einsum('bhcm,bhmv->bhcv', qk_i, v_new)
        state = state * gend_i[..., None, None] + jnp.einsum('bhck,bhcv->bhkv', kend_i, v_new)
        return state, o

    state0 = jnp.zeros((b, h, dk, dv), jnp.float32)
    _, o = lax.scan(step, state0, xs)
    o = o.transpose(1, 0, 3, 2, 4).reshape(b, s, h, dv)
    return o.astype(out_dtype)


def _chunk_causal_attention(q, k, v):
    b, s, h, dq = q.shape
    dv = v.shape[-1]
    nb = s // Q_BLOCK
    scale = dq ** -0.5
    k_chunk = jnp.arange(s) // CHUNK
    q_blocks = q.reshape(b, nb, Q_BLOCK, h, dq).transpose(1, 0, 2, 3, 4)

    def attend(args):
        qb, blk = args
        scores = jnp.einsum('bqhd,bkhd->bhqk', qb, k).astype(jnp.float32) * scale
        q_chunk = (blk * Q_BLOCK + jnp.arange(Q_BLOCK)) // CHUNK
        mask = k_chunk[None, :] <= q_chunk[:, None]
        p = jax.nn.softmax(jnp.where(mask, scores, -jnp.inf), axis=-1)
        return jnp.einsum('bhqk,bkhd->bqhd', p.astype(v.dtype), v)

    out = lax.map(attend, (q_blocks, jnp.arange(nb)))
    return out.transpose(1, 0, 2, 3, 4).reshape(b, s, h * dv)


def setup_inputs(seed: int = 0) -> dict:
    key = jax.random.key(seed)
    ks = jax.random.split(key, 20)

    def nrm(k, shape, fan_in):
        return jax.random.normal(k, shape, jnp.float32) * fan_in ** -0.5

    def gain(k, n):
        return 1.0 + 0.02 * jax.random.normal(k, (DEPTH, n), jnp.float32)

    x = jax.random.normal(ks[0], (BATCH, SEQ, D_MODEL), jnp.float32)
    offsets = jax.random.randint(ks[1], (BATCH, 1), 0, MAX_STREAM_OFFSET, dtype=jnp.int32)
    positions = offsets + jnp.arange(SEQ, dtype=jnp.int32)[None, :]
    w_in = nrm(ks[2], (DEPTH, D_MODEL, PROJ_WIDTH), D_MODEL)
    conv_w = nrm(ks[3], (DEPTH, CONV_WIDTH, GDN_QKV), CONV_WIDTH)
    A_log = jnp.log(jax.random.uniform(ks[4], (DEPTH, GDN_HEADS), jnp.float32, 1.0, 16.0))
    dt = jnp.exp(jax.random.uniform(ks[5], (DEPTH, GDN_HEADS), jnp.float32, math.log(1e-3), math.log(1e-1)))
    dt_bias = dt + jnp.log(-jnp.expm1(-dt))
    gdn_norm_w = gain(ks[6], GDN_DV)
    q_a_norm_w = gain(ks[7], MLA_Q_LORA)
    w_uq = nrm(ks[8], (DEPTH, MLA_Q_LORA, MLA_HEADS * MLA_QK_DIM), MLA_Q_LORA)
    kv_a_norm_w = gain(ks[9], MLA_KV_LORA)
    w_ukv = nrm(ks[10], (DEPTH, MLA_KV_LORA, MLA_HEADS * (MLA_NOPE + MLA_V)), MLA_KV_LORA)
    w_branch_a = nrm(ks[11], (DEPTH, GDN_HEADS * GDN_DV, D_MODEL), GDN_HEADS * GDN_DV)
    w_branch_b = nrm(ks[12], (DEPTH, MLA_HEADS * MLA_V, D_MODEL), MLA_HEADS * MLA_V)
    w_out = nrm(ks[13], (DEPTH, D_MODEL, D_MODEL), D_MODEL)
    pre_mix_norm_w = gain(ks[14], D_MODEL)
    post_mix_norm_w = gain(ks[15], D_MODEL)
    pre_ffn_norm_w = gain(ks[16], D_MODEL)
    post_ffn_norm_w = gain(ks[17], D_MODEL)
    w_ff1 = nrm(ks[18], (DEPTH, D_MODEL, D_FF), D_MODEL)
    w_ff2 = nrm(ks[19], (DEPTH, D_FF, D_MODEL), D_FF)
    return {"x": x, "positions": positions, "w_in": w_in, "conv_w": conv_w, "A_log": A_log,
            "dt_bias": dt_bias, "gdn_norm_w": gdn_norm_w, "q_a_norm_w": q_a_norm_w, "w_uq": w_uq,
            "kv_a_norm_w": kv_a_norm_w, "w_ukv": w_ukv, "w_branch_a": w_branch_a, "w_branch_b": w_branch_b,
            "w_out": w_out, "pre_mix_norm_w": pre_mix_norm_w, "post_mix_norm_w": post_mix_norm_w,
            "pre_ffn_norm_w": pre_ffn_norm_w, "post_ffn_norm_w": post_ffn_norm_w, "w_ff1": w_ff1, "w_ff2": w_ff2}


def reference(x, positions, w_in, conv_w, A_log, dt_bias, gdn_norm_w, q_a_norm_w, w_uq, kv_a_norm_w, w_ukv,
              w_branch_a, w_branch_b, w_out, pre_mix_norm_w, post_mix_norm_w, pre_ffn_norm_w, post_ffn_norm_w,
              w_ff1, w_ff2):
    b, s, _ = x.shape
    inv_freq = ROPE_BASE ** (-jnp.arange(0, MLA_ROPE, 2, dtype=jnp.float32) / MLA_ROPE)
    ang = positions.astype(jnp.float32)[..., None] * inv_freq
    cos, sin = jnp.cos(ang).astype(x.dtype), jnp.sin(ang).astype(x.dtype)

    for l in range(DEPTH):
        u = _rms_norm(x, pre_mix_norm_w[l])
        proj = u @ w_in[l]
        qkv_a, z_a, a_a, b_a, cq, ckv, k_rope, gate_logits = _split_cols(proj, PROJ_SIZES)

        qkv_a = jax.nn.silu(_causal_conv(qkv_a, conv_w[l]))
        q_a, k_a, v_a = _split_cols(qkv_a, (GDN_HEADS * GDN_DK, GDN_HEADS * GDN_DK, GDN_HEADS * GDN_DV))
        q_a = _l2_norm(q_a.reshape(b, s, GDN_HEADS, GDN_DK).astype(jnp.float32))
        k_a = _l2_norm(k_a.reshape(b, s, GDN_HEADS, GDN_DK).astype(jnp.float32))
        v_a = v_a.reshape(b, s, GDN_HEADS, GDN_DV)
        g_a = -jnp.exp(A_log[l].astype(jnp.float32)) * jax.nn.softplus(
            a_a.astype(jnp.float32) + dt_bias[l].astype(jnp.float32))
        beta_a = jax.nn.sigmoid(b_a.astype(jnp.float32))
        o_a = _gated_delta_rule(q_a, k_a, v_a, g_a, beta_a)
        o_a = _rms_norm(o_a, gdn_norm_w[l]) * jax.nn.silu(z_a.reshape(b, s, GDN_HEADS, GDN_DV))
        y_a = o_a.reshape(b, s, GDN_HEADS * GDN_DV) @ w_branch_a[l]

        q_b = (_rms_norm(cq, q_a_norm_w[l]) @ w_uq[l]).reshape(b, s, MLA_HEADS, MLA_QK_DIM)
        q_b = jnp.concatenate([q_b[..., :MLA_NOPE], _rope(q_b[..., MLA_NOPE:], cos[:, :, None, :], sin[:, :, None, :])], axis=-1)
        kv_b = (_rms_norm(ckv, kv_a_norm_w[l]) @ w_ukv[l]).reshape(b, s, MLA_HEADS, MLA_NOPE + MLA_V)
        k_nope, v_b = kv_b[..., :MLA_NOPE], kv_b[..., MLA_NOPE:]
        k_pe = jnp.broadcast_to(_rope(k_rope, cos, sin)[:, :, None, :], (b, s, MLA_HEADS, MLA_ROPE))
        k_b = jnp.concatenate([k_nope, k_pe], axis=-1)
        y_b = _chunk_causal_attention(q_b, k_b, v_b) @ w_branch_b[l]

        gates = jax.nn.sigmoid(gate_logits.reshape(b, s, N_BRANCHES, D_MODEL))
        h = gates[:, :, 0] * y_a + gates[:, :, 1] * y_b
        x = x + _rms_norm(h @ w_out[l], post_mix_norm_w[l])

        u = _rms_norm(x, pre_ffn_norm_w[l])
        f = jnp.square(jax.nn.relu(u @ w_ff1[l])) @ w_ff2[l]
        x = x + _rms_norm(f, post_ffn_norm_w[l])
    return x
```

```python
import functools

import jax
import jax.numpy as jnp
import numpy as np
from jax import lax
from jax.experimental import pallas as pl
from jax.experimental.pallas import tpu as pltpu

F32 = jnp.float32
BF16 = jnp.bfloat16

LANES = 128
CHUNK = 64
NORM_EPS = 1e-6
HEADS = 8
HEAD_DIM = 128
CONV_WIDTH = 4
MLA_Q_LORA = 384
MLA_KV_LORA = 256
MLA_ROPE = 64
MLA_QK_DIM = HEAD_DIM + MLA_ROPE
ROPE_BASE = 10000.0
VMEM_LIMIT = 56 * 1024 * 1024


def _rms(x, w):
    return x * lax.rsqrt(jnp.mean(x * x, axis=-1, keepdims=True) + NORM_EPS) * w


def _silu(x):
    return x * jax.nn.sigmoid(x)


def _dot(a, b):
    return jnp.dot(a, b, preferred_element_type=F32)


def _dot_nt(a, b):
    return lax.dot_general(a, b, (((1,), (1,)), ((), ())), preferred_element_type=F32)


def _dot_tn(a, b):
    return lax.dot_general(a, b, (((0,), (0,)), ((), ())), preferred_element_type=F32)


def _split(a):
    hi = a.astype(BF16)
    lo = (a - hi.astype(F32)).astype(BF16)
    return hi, lo


def _dot3(a, b):
    ah, al = _split(a)
    bh, bl = _split(b)
    return _dot(ah, bh) + (_dot(ah, bl) + _dot(al, bh))


def _resident(shape):
    nd = len(shape)
    return pl.BlockSpec(shape, lambda *_: (0,) * nd, pipeline_mode=pl.Buffered(1))


def _inproj_body(x_ref, pos_ref, nw_ref, wbig_ref, wsm_ref, qan_ref, kvn_ref, wuq_ref, wukv_ref, rope_ref,
                 qkv_ref, z_ref, gate_ref, ab_ref, q_ref, kn_ref, kpe_ref, v_ref):
    d = x_ref.shape[1]
    u = _rms(x_ref[...], nw_ref[...]).astype(BF16)
    qkv_w = qkv_ref.shape[1]
    qkv_ref[...] = _dot(u, wbig_ref[:, :qkv_w])
    z_ref[...] = _dot(u, wbig_ref[:, qkv_w:qkv_w + d])
    gate_ref[...] = _dot(u, wbig_ref[:, qkv_w + d:])
    sm = _dot(u, wsm_ref[...])
    cq = sm[:, :MLA_Q_LORA]
    ckv = sm[:, MLA_Q_LORA:MLA_Q_LORA + MLA_KV_LORA]
    o = MLA_Q_LORA + MLA_KV_LORA
    kr, krs, ab = sm[:, o:o + LANES], sm[:, o + LANES:o + 2 * LANES], sm[:, o + 2 * LANES:]
    ab_ref[...] = ab

    ang = pos_ref[...] * rope_ref[0:1, :]
    cosv = jnp.cos(ang) * rope_ref[1:2, :]
    sinv = jnp.sin(ang) * rope_ref[2:3, :]
    kpe_ref[...] = (kr * cosv + krs * sinv).astype(BF16)

    cqn = _rms(cq, qan_ref[...]).astype(BF16)
    q3 = _dot(cqn, wuq_ref[...])
    hw = HEADS * HEAD_DIM
    scale = MLA_QK_DIM ** -0.5
    for h in range(HEADS):
        lo = h * HEAD_DIM
        qn = q3[:, lo:lo + HEAD_DIM]
        qp = q3[:, hw + lo:hw + lo + HEAD_DIM] * cosv + q3[:, 2 * hw + lo:2 * hw + lo + HEAD_DIM] * sinv
        q_ref[:, 2 * lo:2 * lo + HEAD_DIM] = (qn * scale).astype(BF16)
        q_ref[:, 2 * lo + HEAD_DIM:2 * lo + 2 * HEAD_DIM] = (qp * scale).astype(BF16)

    ckvn = _rms(ckv, kvn_ref[...]).astype(BF16)
    kv = _dot(ckvn, wukv_ref[...])
    kn_ref[...] = kv[:, :hw].astype(BF16)
    v_ref[...] = kv[:, hw:].astype(BF16)


def _inproj(x2, pos2, nw, wbig, wsm, qan, kvn, wuq, wukv, rope_tab, tm):
    t, d = x2.shape
    qkv_w = 3 * HEADS * HEAD_DIM
    hw = HEADS * HEAD_DIM
    row = lambda w: pl.BlockSpec((tm, w), lambda i: (i, 0))
    outs = [(qkv_w, F32), (d, F32), (2 * d, F32), (LANES, F32), (2 * hw, BF16), (hw, BF16), (LANES, BF16), (hw, BF16)]
    return pl.pallas_call(
        _inproj_body,
        grid=(t // tm,),
        in_specs=[row(d), row(1), _resident(nw.shape), _resident(wbig.shape), _resident(wsm.shape),
                  _resident(qan.shape), _resident(kvn.shape), _resident(wuq.shape), _resident(wukv.shape),
                  _resident(rope_tab.shape)],
        out_specs=[row(w) for w, _ in outs],
        out_shape=[jax.ShapeDtypeStruct((t, w), dt) for w, dt in outs],
        compiler_params=pltpu.CompilerParams(dimension_semantics=("arbitrary",), vmem_limit_bytes=VMEM_LIMIT),
        name="inproj",
    )(x2, pos2, nw, wbig, wsm, qan, kvn, wuq, wukv, rope_tab)


def _tri_inverse(low, eye, m16, m32, m64):
    ld = low * m16
    x = eye - ld
    p = _dot3(ld, ld)
    x = x + _dot3(x, p)
    p = _dot3(p, p)
    x = x + _dot3(x, p)
    p = _dot3(p, p)
    x = x + _dot3(x, p)
    x = x - _dot3(_dot3(x, low * m32), x)
    x = x - _dot3(_dot3(x, low * m64), x)
    return x


def _gdn_body(qkv_ref, ab_ref, z_ref, convw_ref, gpar_ref, nw_ref, o_ref, buf_ref, state_ref):
    c = CHUNK
    hw = HEADS * HEAD_DIM
    tail = 8

    @pl.when(pl.program_id(1) == 0)
    def _():
        buf_ref[c:c + tail, :] = jnp.zeros((tail, buf_ref.shape[1]), F32)
        state_ref[...] = jnp.zeros_like(state_ref)

    buf_ref[0:tail, :] = buf_ref[c:c + tail, :]
    buf_ref[tail:tail + c, :] = qkv_ref[...]

    def conv_silu(lo):
        acc = None
        for j in range(CONV_WIDTH):
            r0 = tail - (CONV_WIDTH - 1) + j
            term = buf_ref[r0:r0 + c, lo:lo + HEAD_DIM] * convw_ref[j:j + 1, lo:lo + HEAD_DIM]
            acc = term if acc is None else acc + term
        return _silu(acc)

    ab = ab_ref[...]
    xg = ab + gpar_ref[0:1, :]
    softplus = jnp.maximum(xg, 0.0) + jnp.log(1.0 + jnp.exp(-jnp.abs(xg)))
    g = -jnp.exp(gpar_ref[1:2, :]) * softplus
    beta_all = jax.nn.sigmoid(ab)

    ri = lax.broadcasted_iota(jnp.int32, (c, c), 0)
    ci = lax.broadcasted_iota(jnp.int32, (c, c), 1)
    incl = ri >= ci
    strict = ri > ci
    eye = (ri == ci).astype(F32)
    m16 = ((ri // 16) == (ci // 16)).astype(F32)
    m32 = jnp.logical_and((ri // 32) == (ci // 32), (ri // 16) != (ci // 16)).astype(F32)
    m64 = ((ri // 32) != (ci // 32)).astype(F32)

    gc_col = jnp.dot(incl.astype(F32), g, preferred_element_type=F32, precision=lax.Precision.HIGHEST)
    g_t = jnp.concatenate([g, jnp.zeros_like(g)], axis=0).T
    r2 = lax.broadcasted_iota(jnp.int32, (LANES, LANES), 0)
    c2 = lax.broadcasted_iota(jnp.int32, (LANES, LANES), 1)
    gc_row = jnp.dot(g_t[0:HEADS, :], (r2 <= c2).astype(F32), preferred_element_type=F32,
                     precision=lax.Precision.HIGHEST)

    for h in range(HEADS):
        lo = h * HEAD_DIM
        q = conv_silu(lo)
        k = conv_silu(hw + lo)
        v = conv_silu(2 * hw + lo)
        q = q * lax.rsqrt(jnp.sum(q * q, axis=-1, keepdims=True) + NORM_EPS) * (HEAD_DIM ** -0.5)
        k = k * lax.rsqrt(jnp.sum(k * k, axis=-1, keepdims=True) + NORM_EPS)
        gcol = gc_col[:, h:h + 1]
        grow = gc_row[h:h + 1, 0:c]
        beta = beta_all[:, HEADS + h:HEADS + h + 1]
        decay = jnp.exp(jnp.where(incl, gcol - grow, -jnp.inf))
        egc = jnp.exp(gcol)
        gend = gcol[c - 1:c, :]
        kb = k * beta
        vb = v * beta
        kbb = kb.astype(BF16)
        kbf = k.astype(BF16)
        kk = _dot_nt(kbb, kbf) * decay
        t_inv = _tri_inverse(jnp.where(strict, kk, 0.0), eye, m16, m32, m64)
        rhs = jnp.concatenate([vb, kb * egc], axis=1)
        sol = _dot3(t_inv, rhs)
        u, w = sol[:, :HEAD_DIM], sol[:, HEAD_DIM:]
        qk = (_dot_nt(q.astype(BF16), kbf) * decay).astype(BF16)
        state = state_ref[h]
        sb = state.astype(BF16)
        wq = jnp.concatenate([w, q * egc], axis=0).astype(BF16)
        ws_qs = _dot(wq, sb)
        v_new = u - ws_qs[:c]
        vnb = v_new.astype(BF16)
        o = ws_qs[c:] + _dot(qk, vnb)
        k_end = (k * jnp.exp(gend - gcol)).astype(BF16)
        state_ref[h] = state * jnp.exp(gend) + _dot_tn(k_end, vnb)
        o = _rms(o, nw_ref[...]) * _silu(z_ref[:, lo:lo + HEAD_DIM])
        o_ref[:, lo:lo + HEAD_DIM] = o.astype(o_ref.dtype)


def _gdn(qkv, ab, z, convw, gpar, nw, batch, seq):
    t, qkv_w = qkv.shape
    hw = HEADS * HEAD_DIM
    n = seq // CHUNK
    row = lambda w: pl.BlockSpec((CHUNK, w), lambda b, s: (b * n + s, 0))
    return pl.pallas_call(
        _gdn_body,
        grid=(batch, n),
        in_specs=[row(qkv_w), row(LANES), row(hw), _resident(convw.shape), _resident(gpar.shape),
                  _resident(nw.shape)],
        out_specs=row(hw),
        out_shape=jax.ShapeDtypeStruct((t, hw), BF16),
        scratch_shapes=[pltpu.VMEM((CHUNK + 8, qkv_w), F32), pltpu.VMEM((HEADS, HEAD_DIM, HEAD_DIM), F32)],
        compiler_params=pltpu.CompilerParams(dimension_semantics=("arbitrary", "arbitrary"),
                                             vmem_limit_bytes=VMEM_LIMIT),
        name="gdn",
    )(qkv, ab, z, convw, gpar, nw)


def _attn_body(q_ref, kn_ref, kpe_ref, v_ref, o_ref, *, tq):
    qi = pl.program_id(2)
    q = q_ref[...]

    def tile(ki, carry, masked):
        m, l, acc = carry
        off = pl.multiple_of(ki * tq, tq)
        kcat = jnp.concatenate([kn_ref[pl.ds(off, tq), :], kpe_ref[pl.ds(off, tq), :]], axis=1)
        s = _dot_nt(q, kcat)
        if masked:
            rq = lax.broadcasted_iota(jnp.int32, (tq, tq), 0) // CHUNK
            ck = lax.broadcasted_iota(jnp.int32, (tq, tq), 1) // CHUNK
            s = jnp.where(ck <= rq, s, -jnp.inf)
        m_new = jnp.maximum(m, jnp.max(s, axis=-1, keepdims=True))
        alpha = jnp.exp(m - m_new)
        p = jnp.exp(s - m_new)
        l = alpha * l + jnp.sum(p, axis=-1, keepdims=True)
        acc = alpha * acc + _dot(p.astype(BF16), v_ref[pl.ds(off, tq), :])
        return m_new, l, acc

    init = (jnp.full((tq, 1), -jnp.inf, F32), jnp.zeros((tq, 1), F32), jnp.zeros((tq, HEAD_DIM), F32))
    carry = lax.fori_loop(0, qi, lambda ki, cr: tile(ki, cr, False), init)
    _, l, acc = tile(qi, carry, True)
    o_ref[...] = (acc / l).astype(o_ref.dtype)


def _attn(q, kn, kpe, v, batch, seq, tq):
    hw = HEADS * HEAD_DIM
    q3 = q.reshape(batch, seq, 2 * hw)
    kn3 = kn.reshape(batch, seq, hw)
    kpe3 = kpe.reshape(batch, seq, LANES)
    v3 = v.reshape(batch, seq, hw)
    out = pl.pallas_call(
        functools.partial(_attn_body, tq=tq),
        grid=(batch, HEADS, seq // tq),
        in_specs=[pl.BlockSpec((None, tq, 2 * HEAD_DIM), lambda b, h, i: (b, i, h)),
                  pl.BlockSpec((None, seq, HEAD_DIM), lambda b, h, i: (b, 0, h)),
                  pl.BlockSpec((None, seq, LANES), lambda b, h, i: (b, 0, 0)),
                  pl.BlockSpec((None, seq, HEAD_DIM), lambda b, h, i: (b, 0, h))],
        out_specs=pl.BlockSpec((None, tq, HEAD_DIM), lambda b, h, i: (b, i, h)),
        out_shape=jax.ShapeDtypeStruct((batch, seq, hw), BF16),
        compiler_params=pltpu.CompilerParams(dimension_semantics=("arbitrary", "arbitrary", "arbitrary"),
                                             vmem_limit_bytes=VMEM_LIMIT),
        name="attn",
    )(q3, kn3, kpe3, v3)
    return out.reshape(batch * seq, hw)


def _mix_body(oa_ref, at_ref, g_ref, x_ref, wa_ref, wb_ref, wo_ref, nw_ref, o_ref):
    d = x_ref.shape[1]
    ya = _dot(oa_ref[...], wa_ref[...])
    yb = _dot(at_ref[...], wb_ref[...])
    h = jax.nn.sigmoid(g_ref[:, :d]) * ya + jax.nn.sigmoid(g_ref[:, d:]) * yb
    y = _dot(h.astype(BF16), wo_ref[...])
    o_ref[...] = x_ref[...] + _rms(y, nw_ref[...])


def _mix(oa, at, gate, x2, wa, wb, wo, nw, tm):
    t, d = x2.shape
    row = lambda w: pl.BlockSpec((tm, w), lambda i: (i, 0))
    return pl.pallas_call(
        _mix_body,
        grid=(t // tm,),
        in_specs=[row(oa.shape[1]), row(at.shape[1]), row(2 * d), row(d), _resident(wa.shape),
                  _resident(wb.shape), _resident(wo.shape), _resident(nw.shape)],
        out_specs=row(d),
        out_shape=jax.ShapeDtypeStruct((t, d), F32),
        compiler_params=pltpu.CompilerParams(dimension_semantics=("arbitrary",), vmem_limit_bytes=VMEM_LIMIT),
        name="mix",
    )(oa, at, gate, x2, wa, wb, wo, nw)


def _ffn_body(x_ref, n1_ref, w1_ref, w2_ref, n2_ref, o_ref, *, fc):
    x = x_ref[...]
    u = _rms(x, n1_ref[...]).astype(BF16)
    acc = None
    for c0 in range(0, w1_ref.shape[1], fc):
        hcol = jnp.maximum(_dot(u, w1_ref[:, c0:c0 + fc]), 0.0)
        part = _dot((hcol * hcol).astype(BF16), w2_ref[c0:c0 + fc, :])
        acc = part if acc is None else acc + part
    o_ref[...] = x + _rms(acc, n2_ref[...])


def _ffn(x2, n1, w1, w2, n2, tm):
    t, d = x2.shape
    row = pl.BlockSpec((tm, d), lambda i: (i, 0))
    return pl.pallas_call(
        functools.partial(_ffn_body, fc=min(1024, w1.shape[1])),
        grid=(t // tm,),
        in_specs=[row, _resident(n1.shape), _resident(w1.shape), _resident(w2.shape), _resident(n2.shape)],
        out_specs=row,
        out_shape=jax.ShapeDtypeStruct((t, d), F32),
        compiler_params=pltpu.CompilerParams(dimension_semantics=("arbitrary",), vmem_limit_bytes=VMEM_LIMIT),
        name="ffn",
    )(x2, n1, w1, w2, n2)


def _rope_table():
    inv_freq = ROPE_BASE ** (-jnp.arange(0, MLA_ROPE, 2, dtype=F32) / MLA_ROPE)
    half = MLA_ROPE // 2
    zeros = jnp.zeros((LANES - MLA_ROPE,), F32)
    rows = [jnp.concatenate([inv_freq, inv_freq, zeros]),
            jnp.concatenate([jnp.ones((MLA_ROPE,), F32), zeros]),
            jnp.concatenate([-jnp.ones((half,), F32), jnp.ones((half,), F32), zeros])]
    return jnp.concatenate([jnp.stack(rows), jnp.zeros((5, LANES), F32)], axis=0)


def _pad_cols(w, width):
    return jnp.pad(w, ((0, 0), (0, width - w.shape[1])))


def _layer_weights(w_in, w_uq, w_ukv):
    d = w_in.shape[0]
    hw = HEADS * HEAD_DIM
    o = 3 * hw
    qkv, z = w_in[:, :o], w_in[:, o:o + hw]
    o += hw
    a_b = w_in[:, o:o + 2 * HEADS]
    o += 2 * HEADS
    cq = w_in[:, o:o + MLA_Q_LORA]
    o += MLA_Q_LORA
    ckv = w_in[:, o:o + MLA_KV_LORA]
    o += MLA_KV_LORA
    kr = w_in[:, o:o + MLA_ROPE]
    o += MLA_ROPE
    gates = w_in[:, o:]
    half = MLA_ROPE // 2
    krs = jnp.concatenate([kr[:, half:], kr[:, :half]], axis=1)
    wbig = jnp.concatenate([qkv, z, gates], axis=1).astype(BF16)
    wsm = jnp.concatenate([cq, ckv, _pad_cols(kr, LANES), _pad_cols(krs, LANES), _pad_cols(a_b, LANES)],
                          axis=1).astype(BF16)
    uq = w_uq.reshape(MLA_Q_LORA, HEADS, MLA_QK_DIM)
    nope, pe = uq[:, :, :HEAD_DIM], uq[:, :, HEAD_DIM:]
    pes = jnp.concatenate([pe[:, :, half:], pe[:, :, :half]], axis=2)
    padh = lambda w: jnp.pad(w, ((0, 0), (0, 0), (0, HEAD_DIM - MLA_ROPE))).reshape(MLA_Q_LORA, hw)
    wuq = jnp.concatenate([nope.reshape(MLA_Q_LORA, hw), padh(pe), padh(pes)], axis=1).astype(BF16)
    ukv = w_ukv.reshape(MLA_KV_LORA, HEADS, 2 * HEAD_DIM)
    wukv = jnp.concatenate([ukv[:, :, :HEAD_DIM].reshape(MLA_KV_LORA, hw),
                            ukv[:, :, HEAD_DIM:].reshape(MLA_KV_LORA, hw)], axis=1).astype(BF16)
    return wbig, wsm, wuq, wukv


def kernel(x, positions, w_in, conv_w, A_log, dt_bias, gdn_norm_w, q_a_norm_w, w_uq, kv_a_norm_w, w_ukv, w_branch_a, w_branch_b, w_out, pre_mix_norm_w, post_mix_norm_w, pre_ffn_norm_w, post_ffn_norm_w, w_ff1, w_ff2):
    batch, seq, d = x.shape
    t = batch * seq
    depth = w_in.shape[0]
    tm_proj = min(256, t)
    tm_mlp = min(512, t)
    tq = min(512, seq)
    x2 = x.reshape(t, d)
    pos2 = positions.astype(F32).reshape(t, 1)
    rope_tab = _rope_table()
    row = lambda v: v.reshape(1, -1)
    for l in range(depth):
        wbig, wsm, wuq, wukv = _layer_weights(w_in[l], w_uq[l], w_ukv[l])
        qkv, z, gate, ab, q, kn, kpe, v = _inproj(
            x2, pos2, row(pre_mix_norm_w[l]), wbig, wsm, row(q_a_norm_w[l]), row(kv_a_norm_w[l]), wuq, wukv,
            rope_tab, tm_proj)
        convw = jnp.pad(conv_w[l], ((0, 8 - CONV_WIDTH), (0, 0)))
        gpar = jnp.zeros((8, LANES), F32).at[0, :HEADS].set(dt_bias[l]).at[1, :HEADS].set(A_log[l])
        oa = _gdn(qkv, ab, z, convw, gpar, row(gdn_norm_w[l]), batch, seq)
        at = _attn(q, kn, kpe, v, batch, seq, tq)
        x2 = _mix(oa, at, gate, x2, w_branch_a[l].astype(BF16), w_branch_b[l].astype(BF16),
                  w_out[l].astype(BF16), row(post_mix_norm_w[l]), tm_mlp)
        x2 = _ffn(x2, row(pre_ffn_norm_w[l]), w_ff1[l].astype(BF16), w_ff2[l].astype(BF16),
                  row(post_ffn_norm_w[l]), tm_mlp)
    return x2.reshape(batch, seq, d)
```

```python
import functools

import jax
import jax.numpy as jnp
import numpy as np
from jax import lax
from jax.experimental import pallas as pl
from jax.experimental.pallas import tpu as pltpu

F32 = jnp.float32
BF16 = jnp.bfloat16

LANES = 128
CHUNK = 64
NORM_EPS = 1e-6
HEADS = 8
HEAD_DIM = 128
CONV_WIDTH = 4
MLA_Q_LORA = 384
MLA_KV_LORA = 256
MLA_ROPE = 64
MLA_QK_DIM = HEAD_DIM + MLA_ROPE
ROPE_BASE = 10000.0
VMEM_LIMIT = 56 * 1024 * 1024


def _rms(x, w):
    return x * lax.rsqrt(jnp.mean(x * x, axis=-1, keepdims=True) + NORM_EPS) * w


def _silu(x):
    return x * jax.nn.sigmoid(x)


def _dot(a, b):
    return jnp.dot(a, b, preferred_element_type=F32)


def _dot_nt(a, b):
    return lax.dot_general(a, b, (((1,), (1,)), ((), ())), preferred_element_type=F32)


def _dot_tn(a, b):
    return lax.dot_general(a, b, (((0,), (0,)), ((), ())), preferred_element_type=F32)


def _split(a):
    hi = a.astype(BF16)
    lo = (a - hi.astype(F32)).astype(BF16)
    return hi, lo


def _dot3(a, b):
    ah, al = _split(a)
    bh, bl = _split(b)
    return _dot(ah, bh) + (_dot(ah, bl) + _dot(al, bh))


def _resident(shape):
    nd = len(shape)
    return pl.BlockSpec(shape, lambda *_: (0,) * nd, pipeline_mode=pl.Buffered(1))


def _inproj_body(x_ref, pos_ref, nw_ref, wbig_ref, wsm_ref, qan_ref, kvn_ref, wuq_ref, wukv_ref, rope_ref,
                 qkv_ref, z_ref, gate_ref, ab_ref, q_ref, kn_ref, kpe_ref, v_ref):
    d = x_ref.shape[1]
    u = _rms(x_ref[...], nw_ref[...]).astype(BF16)
    qkv_w = qkv_ref.shape[1]
    qkv_ref[...] = _dot(u, wbig_ref[:, :qkv_w])
    z_ref[...] = _dot(u, wbig_ref[:, qkv_w:qkv_w + d])
    gate_ref[...] = _dot(u, wbig_ref[:, qkv_w + d:])
    sm = _dot(u, wsm_ref[...])
    cq = sm[:, :MLA_Q_LORA]
    ckv = sm[:, MLA_Q_LORA:MLA_Q_LORA + MLA_KV_LORA]
    o = MLA_Q_LORA + MLA_KV_LORA
    kr, krs, ab = sm[:, o:o + LANES], sm[:, o + LANES:o + 2 * LANES], sm[:, o + 2 * LANES:]
    ab_ref[...] = ab

    ang = pos_ref[...] * rope_ref[0:1, :]
    cosv = jnp.cos(ang) * rope_ref[1:2, :]
    sinv = jnp.sin(ang) * rope_ref[2:3, :]
    kpe_ref[...] = (kr * cosv + krs * sinv).astype(BF16)

    cqn = _rms(cq, qan_ref[...]).astype(BF16)
    q3 = _dot(cqn, wuq_ref[...])
    hw = HEADS * HEAD_DIM
    scale = MLA_QK_DIM ** -0.5
    for h in range(HEADS):
        lo = h * HEAD_DIM
        qn = q3[:, lo:lo + HEAD_DIM]
        qp = q3[:, hw + lo:hw + lo + HEAD_DIM] * cosv + q3[:, 2 * hw + lo:2 * hw + lo + HEAD_DIM] * sinv
        q_ref[:, 2 * lo:2 * lo + HEAD_DIM] = (qn * scale).astype(BF16)
        q_ref[:, 2 * lo + HEAD_DIM:2 * lo + 2 * HEAD_DIM] = (qp * scale).astype(BF16)

    ckvn = _rms(ckv, kvn_ref[...]).astype(BF16)
    kv = _dot(ckvn, wukv_ref[...])
    kn_ref[...] = kv[:, :hw].astype(BF16)
    v_ref[...] = kv[:, hw:].astype(BF16)


def _inproj(x2, pos2, nw, wbig, wsm, qan, kvn, wuq, wukv, rope_tab, tm):
    t, d = x2.shape
    qkv_w = 3 * HEADS * HEAD_DIM
    hw = HEADS * HEAD_DIM
    row = lambda w: pl.BlockSpec((tm, w), lambda i: (i, 0))
    outs = [(qkv_w, F32), (d, F32), (2 * d, F32), (LANES, F32), (2 * hw, BF16), (hw, BF16), (LANES, BF16), (hw, BF16)]
    return pl.pallas_call(
        _inproj_body,
        grid=(t // tm,),
        in_specs=[row(d), row(1), _resident(nw.shape), _resident(wbig.shape), _resident(wsm.shape),
                  _resident(qan.shape), _resident(kvn.shape), _resident(wuq.shape), _resident(wukv.shape),
                  _resident(rope_tab.shape)],
        out_specs=[row(w) for w, _ in outs],
        out_shape=[jax.ShapeDtypeStruct((t, w), dt) for w, dt in outs],
        compiler_params=pltpu.CompilerParams(dimension_semantics=("arbitrary",), vmem_limit_bytes=VMEM_LIMIT),
        name="inproj",
    )(x2, pos2, nw, wbig, wsm, qan, kvn, wuq, wukv, rope_tab)


def _gdn_body(qkv_ref, ab_ref, z_ref, convw_ref, gpar_ref, nw_ref, o_ref, buf_ref, state_ref):
    c = CHUNK
    hw = HEADS * HEAD_DIM
    tail = 8

    @pl.when(pl.program_id(1) == 0)
    def _():
        buf_ref[c:c + tail, :] = jnp.zeros((tail, buf_ref.shape[1]), F32)
        state_ref[...] = jnp.zeros_like(state_ref)

    buf_ref[0:tail, :] = buf_ref[c:c + tail, :]
    buf_ref[tail:tail + c, :] = qkv_ref[...]

    def conv_silu(lo):
        acc = None
        for j in range(CONV_WIDTH):
            r0 = tail - (CONV_WIDTH - 1) + j
            term = buf_ref[r0:r0 + c, lo:lo + HEAD_DIM] * convw_ref[j:j + 1, lo:lo + HEAD_DIM]
            acc = term if acc is None else acc + term
        return _silu(acc)

    ab = ab_ref[...]
    xg = ab + gpar_ref[0:1, :]
    softplus = jnp.maximum(xg, 0.0) + jnp.log(1.0 + jnp.exp(-jnp.abs(xg)))
    g = -jnp.exp(gpar_ref[1:2, :]) * softplus
    beta_all = jax.nn.sigmoid(ab)

    ri = lax.broadcasted_iota(jnp.int32, (c, c), 0)
    ci = lax.broadcasted_iota(jnp.int32, (c, c), 1)
    incl = ri >= ci
    strict = ri > ci
    eye = (ri == ci).astype(F32)
    m16 = ((ri // 16) == (ci // 16)).astype(F32)

    gc_col = jnp.dot(incl.astype(F32), g, preferred_element_type=F32, precision=lax.Precision.HIGHEST)
    g_t = jnp.concatenate([g, jnp.zeros_like(g)], axis=0).T
    r2 = lax.broadcasted_iota(jnp.int32, (LANES, LANES), 0)
    c2 = lax.broadcasted_iota(jnp.int32, (LANES, LANES), 1)
    gc_row = jnp.dot(g_t[0:HEADS, :], (r2 <= c2).astype(F32), preferred_element_type=F32,
                     precision=lax.Precision.HIGHEST)

    hs = range(HEADS)
    l2 = lambda t: t * lax.rsqrt(jnp.sum(t * t, axis=-1, keepdims=True) + NORM_EPS)
    q = [l2(conv_silu(h * HEAD_DIM)) * (HEAD_DIM ** -0.5) for h in hs]
    k = [l2(conv_silu(hw + h * HEAD_DIM)) for h in hs]
    v = [conv_silu(2 * hw + h * HEAD_DIM) for h in hs]
    gcol = [gc_col[:, h:h + 1] for h in hs]
    beta = [beta_all[:, HEADS + h:HEADS + h + 1] for h in hs]
    decay = [jnp.exp(jnp.where(incl, gcol[h] - gc_row[h:h + 1, 0:c], -jnp.inf)) for h in hs]
    egc = [jnp.exp(t) for t in gcol]
    gend = [t[c - 1:c, :] for t in gcol]
    kb = [k[h] * beta[h] for h in hs]
    kbf = [t.astype(BF16) for t in k]
    low = [jnp.where(strict, _dot_nt(kb[h].astype(BF16), kbf[h]) * decay[h], 0.0) for h in hs]
    rhs = [jnp.concatenate([v[h] * beta[h], kb[h] * egc[h]], axis=1) for h in hs]
    qk = [(_dot_nt(q[h].astype(BF16), kbf[h]) * decay[h]).astype(BF16) for h in hs]

    ld = [t * m16 for t in low]
    off = [low[h] - ld[h] for h in hs]
    td = [eye - t for t in ld]
    p = [_dot3(t, t) for t in ld]
    td = [td[h] + _dot3(td[h], p[h]) for h in hs]
    p = [_dot3(t, t) for t in p]
    td = [td[h] + _dot3(td[h], p[h]) for h in hs]
    p = [_dot3(t, t) for t in p]
    td = [td[h] + _dot3(td[h], p[h]) for h in hs]
    y = [_dot3(td[h], rhs[h]) for h in hs]
    m = [_dot3(td[h], off[h]) for h in hs]
    mm = [_dot3(t, t) for t in m]
    y = [y[h] - _dot3(m[h], y[h]) for h in hs]
    sol = [y[h] + _dot3(mm[h], y[h]) for h in hs]

    state = [state_ref[h] for h in hs]
    wq = [jnp.concatenate([sol[h][:, HEAD_DIM:], q[h] * egc[h]], axis=0).astype(BF16) for h in hs]
    ws_qs = [_dot(wq[h], state[h].astype(BF16)) for h in hs]
    vnb = [(sol[h][:, :HEAD_DIM] - ws_qs[h][:c]).astype(BF16) for h in hs]
    o = [ws_qs[h][c:] + _dot(qk[h], vnb[h]) for h in hs]
    k_end = [(k[h] * jnp.exp(gend[h] - gcol[h])).astype(BF16) for h in hs]
    for h in hs:
        state_ref[h] = state[h] * jnp.exp(gend[h]) + _dot_tn(k_end[h], vnb[h])
    for h in hs:
        lo = h * HEAD_DIM
        out = _rms(o[h], nw_ref[...]) * _silu(z_ref[:, lo:lo + HEAD_DIM])
        o_ref[:, lo:lo + HEAD_DIM] = out.astype(o_ref.dtype)


def _gdn(qkv, ab, z, convw, gpar, nw, batch, seq):
    t, qkv_w = qkv.shape
    hw = HEADS * HEAD_DIM
    n = seq // CHUNK
    row = lambda w: pl.BlockSpec((CHUNK, w), lambda b, s: (b * n + s, 0))
    return pl.pallas_call(
        _gdn_body,
        grid=(batch, n),
        in_specs=[row(qkv_w), row(LANES), row(hw), _resident(convw.shape), _resident(gpar.shape),
                  _resident(nw.shape)],
        out_specs=row(hw),
        out_shape=jax.ShapeDtypeStruct((t, hw), BF16),
        scratch_shapes=[pltpu.VMEM((CHUNK + 8, qkv_w), F32), pltpu.VMEM((HEADS, HEAD_DIM, HEAD_DIM), F32)],
        compiler_params=pltpu.CompilerParams(dimension_semantics=("arbitrary", "arbitrary"),
                                             vmem_limit_bytes=VMEM_LIMIT),
        name="gdn",
    )(qkv, ab, z, convw, gpar, nw)


def _attn_body(q_ref, kn_ref, kpe_ref, v_ref, o_ref, *, tq):
    qi = pl.program_id(2)
    q = q_ref[...]

    def tile(ki, carry, masked):
        m, l, acc = carry
        off = pl.multiple_of(ki * tq, tq)
        kcat = jnp.concatenate([kn_ref[pl.ds(off, tq), :], kpe_ref[pl.ds(off, tq), :]], axis=1)
        s = _dot_nt(q, kcat)
        if masked:
            rq = lax.broadcasted_iota(jnp.int32, (tq, tq), 0) // CHUNK
            ck = lax.broadcasted_iota(jnp.int32, (tq, tq), 1) // CHUNK
            s = jnp.where(ck <= rq, s, -jnp.inf)
        m_new = jnp.maximum(m, jnp.max(s, axis=-1, keepdims=True))
        alpha = jnp.exp(m - m_new)
        p = jnp.exp(s - m_new)
        l = alpha * l + jnp.sum(p, axis=-1, keepdims=True)
        acc = alpha * acc + _dot(p.astype(BF16), v_ref[pl.ds(off, tq), :])
        return m_new, l, acc

    init = (jnp.full((tq, 1), -jnp.inf, F32), jnp.zeros((tq, 1), F32), jnp.zeros((tq, HEAD_DIM), F32))
    carry = lax.fori_loop(0, qi, lambda ki, cr: tile(ki, cr, False), init)
    _, l, acc = tile(qi, carry, True)
    o_ref[...] = (acc / l).astype(o_ref.dtype)


def _attn(q, kn, kpe, v, batch, seq, tq):
    hw = HEADS * HEAD_DIM
    q3 = q.reshape(batch, seq, 2 * hw)
    kn3 = kn.reshape(batch, seq, hw)
    kpe3 = kpe.reshape(batch, seq, LANES)
    v3 = v.reshape(batch, seq, hw)
    out = pl.pallas_call(
        functools.partial(_attn_body, tq=tq),
        grid=(batch, HEADS, seq // tq),
        in_specs=[pl.BlockSpec((None, tq, 2 * HEAD_DIM), lambda b, h, i: (b, i, h)),
                  pl.BlockSpec((None, seq, HEAD_DIM), lambda b, h, i: (b, 0, h)),
                  pl.BlockSpec((None, seq, LANES), lambda b, h, i: (b, 0, 0)),
                  pl.BlockSpec((None, seq, HEAD_DIM), lambda b, h, i: (b, 0, h))],
        out_specs=pl.BlockSpec((None, tq, HEAD_DIM), lambda b, h, i: (b, i, h)),
        out_shape=jax.ShapeDtypeStruct((batch, seq, hw), BF16),
        compiler_params=pltpu.CompilerParams(dimension_semantics=("arbitrary", "arbitrary", "arbitrary"),
                                             vmem_limit_bytes=VMEM_LIMIT),
        name="attn",
    )(q3, kn3, kpe3, v3)
    return out.reshape(batch * seq, hw)


def _mix_body(oa_ref, at_ref, g_ref, x_ref, wa_ref, wb_ref, wo_ref, nw_ref, o_ref):
    d = x_ref.shape[1]
    ya = _dot(oa_ref[...], wa_ref[...])
    yb = _dot(at_ref[...], wb_ref[...])
    h = jax.nn.sigmoid(g_ref[:, :d]) * ya + jax.nn.sigmoid(g_ref[:, d:]) * yb
    y = _dot(h.astype(BF16), wo_ref[...])
    o_ref[...] = x_ref[...] + _rms(y, nw_ref[...])


def _mix(oa, at, gate, x2, wa, wb, wo, nw, tm):
    t, d = x2.shape
    row = lambda w: pl.BlockSpec((tm, w), lambda i: (i, 0))
    return pl.pallas_call(
        _mix_body,
        grid=(t // tm,),
        in_specs=[row(oa.shape[1]), row(at.shape[1]), row(2 * d), row(d), _resident(wa.shape),
                  _resident(wb.shape), _resident(wo.shape), _resident(nw.shape)],
        out_specs=row(d),
        out_shape=jax.ShapeDtypeStruct((t, d), F32),
        compiler_params=pltpu.CompilerParams(dimension_semantics=("arbitrary",), vmem_limit_bytes=VMEM_LIMIT),
        name="mix",
    )(oa, at, gate, x2, wa, wb, wo, nw)


def _ffn_body(x_ref, n1_ref, w1_ref, w2_ref, n2_ref, o_ref, *, fc):
    x = x_ref[...]
    u = _rms(x, n1_ref[...]).astype(BF16)
    acc = None
    for c0 in range(0, w1_ref.shape[1], fc):
        hcol = jnp.maximum(_dot(u, w1_ref[:, c0:c0 + fc]), 0.0)
        part = _dot((hcol * hcol).astype(BF16), w2_ref[c0:c0 + fc, :])
        acc = part if acc is None else acc + part
    o_ref[...] = x + _rms(acc, n2_ref[...])


def _ffn(x2, n1, w1, w2, n2, tm):
    t, d = x2.shape
    row = pl.BlockSpec((tm, d), lambda i: (i, 0))
    return pl.pallas_call(
        functools.partial(_ffn_body, fc=min(1024, w1.shape[1])),
        grid=(t // tm,),
        in_specs=[row, _resident(n1.shape), _resident(w1.shape), _resident(w2.shape), _resident(n2.shape)],
        out_specs=row,
        out_shape=jax.ShapeDtypeStruct((t, d), F32),
        compiler_params=pltpu.CompilerParams(dimension_semantics=("arbitrary",), vmem_limit_bytes=VMEM_LIMIT),
        name="ffn",
    )(x2, n1, w1, w2, n2)


def _rope_table():
    inv_freq = ROPE_BASE ** (-jnp.arange(0, MLA_ROPE, 2, dtype=F32) / MLA_ROPE)
    half = MLA_ROPE // 2
    zeros = jnp.zeros((LANES - MLA_ROPE,), F32)
    rows = [jnp.concatenate([inv_freq, inv_freq, zeros]),
            jnp.concatenate([jnp.ones((MLA_ROPE,), F32), zeros]),
            jnp.concatenate([-jnp.ones((half,), F32), jnp.ones((half,), F32), zeros])]
    return jnp.concatenate([jnp.stack(rows), jnp.zeros((5, LANES), F32)], axis=0)


def _pad_cols(w, width):
    return jnp.pad(w, ((0, 0), (0, width - w.shape[1])))


def _layer_weights(w_in, w_uq, w_ukv):
    d = w_in.shape[0]
    hw = HEADS * HEAD_DIM
    o = 3 * hw
    qkv, z = w_in[:, :o], w_in[:, o:o + hw]
    o += hw
    a_b = w_in[:, o:o + 2 * HEADS]
    o += 2 * HEADS
    cq = w_in[:, o:o + MLA_Q_LORA]
    o += MLA_Q_LORA
    ckv = w_in[:, o:o + MLA_KV_LORA]
    o += MLA_KV_LORA
    kr = w_in[:, o:o + MLA_ROPE]
    o += MLA_ROPE
    gates = w_in[:, o:]
    half = MLA_ROPE // 2
    krs = jnp.concatenate([kr[:, half:], kr[:, :half]], axis=1)
    wbig = jnp.concatenate([qkv, z, gates], axis=1).astype(BF16)
    wsm = jnp.concatenate([cq, ckv, _pad_cols(kr, LANES), _pad_cols(krs, LANES), _pad_cols(a_b, LANES)],
                          axis=1).astype(BF16)
    uq = w_uq.reshape(MLA_Q_LORA, HEADS, MLA_QK_DIM)
    nope, pe = uq[:, :, :HEAD_DIM], uq[:, :, HEAD_DIM:]
    pes = jnp.concatenate([pe[:, :, half:], pe[:, :, :half]], axis=2)
    padh = lambda w: jnp.pad(w, ((0, 0), (0, 0), (0, HEAD_DIM - MLA_ROPE))).reshape(MLA_Q_LORA, hw)
    wuq = jnp.concatenate([nope.reshape(MLA_Q_LORA, hw), padh(pe), padh(pes)], axis=1).astype(BF16)
    ukv = w_ukv.reshape(MLA_KV_LORA, HEADS, 2 * HEAD_DIM)
    wukv = jnp.concatenate([ukv[:, :, :HEAD_DIM].reshape(MLA_KV_LORA, hw),
                            ukv[:, :, HEAD_DIM:].reshape(MLA_KV_LORA, hw)], axis=1).astype(BF16)
    return wbig, wsm, wuq, wukv


def kernel(x, positions, w_in, conv_w, A_log, dt_bias, gdn_norm_w, q_a_norm_w, w_uq, kv_a_norm_w, w_ukv, w_branch_a, w_branch_b, w_out, pre_mix_norm_w, post_mix_norm_w, pre_ffn_norm_w, post_ffn_norm_w, w_ff1, w_ff2):
    batch, seq, d = x.shape
    t = batch * seq
    depth = w_in.shape[0]
    tm_proj = min(256, t)
    tm_mlp = min(512, t)
    tq = min(512, seq)
    x2 = x.reshape(t, d)
    pos2 = positions.astype(F32).reshape(t, 1)
    rope_tab = _rope_table()
    row = lambda v: v.reshape(1, -1)
    for l in range(depth):
        wbig, wsm, wuq, wukv = _layer_weights(w_in[l], w_uq[l], w_ukv[l])
        qkv, z, gate, ab, q, kn, kpe, v = _inproj(
            x2, pos2, row(pre_mix_norm_w[l]), wbig, wsm, row(q_a_norm_w[l]), row(kv_a_norm_w[l]), wuq, wukv,
            rope_tab, tm_proj)
        convw = jnp.pad(conv_w[l], ((0, 8 - CONV_WIDTH), (0, 0)))
        gpar = jnp.zeros((8, LANES), F32).at[0, :HEADS].set(dt_bias[l]).at[1, :HEADS].set(A_log[l])
        oa = _gdn(qkv, ab, z, convw, gpar, row(gdn_norm_w[l]), batch, seq)
        at = _attn(q, kn, kpe, v, batch, seq, tq)
        x2 = _mix(oa, at, gate, x2, w_branch_a[l].astype(BF16), w_branch_b[l].astype(BF16),
                  w_out[l].astype(BF16), row(post_mix_norm_w[l]), tm_mlp)
        x2 = _ffn(x2, row(pre_ffn_norm_w[l]), w_ff1[l].astype(BF16), w_ff2[l].astype(BF16),
                  row(post_ffn_norm_w[l]), tm_mlp)
    return x2.reshape(batch, seq, d)
```

```python
import functools

import jax
import jax.numpy as jnp
import numpy as np
from jax import lax
from jax.experimental import pallas as pl
from jax.experimental.pallas import tpu as pltpu

F32 = jnp.float32
BF16 = jnp.bfloat16

LANES = 128
CHUNK = 64
NORM_EPS = 1e-6
HEADS = 8
HEAD_DIM = 128
CONV_WIDTH = 4
MLA_Q_LORA = 384
MLA_KV_LORA = 256
MLA_ROPE = 64
MLA_QK_DIM = HEAD_DIM + MLA_ROPE
ROPE_BASE = 10000.0
LOG2_E = 1.4426950408889634
VMEM_LIMIT = 56 * 1024 * 1024


def _rms(x, w):
    return x * lax.rsqrt(jnp.mean(x * x, axis=-1, keepdims=True) + NORM_EPS) * w


def _silu(x):
    return x * jax.nn.sigmoid(x)


def _dot(a, b):
    return jnp.dot(a, b, preferred_element_type=F32)


def _dot_nt(a, b):
    return lax.dot_general(a, b, (((1,), (1,)), ((), ())), preferred_element_type=F32)


def _dot_tn(a, b):
    return lax.dot_general(a, b, (((0,), (0,)), ((), ())), preferred_element_type=F32)


def _dot3(a, b):
    return _dot(a.astype(BF16), b.astype(BF16))


def _resident(shape):
    nd = len(shape)
    return pl.BlockSpec(shape, lambda *_: (0,) * nd, pipeline_mode=pl.Buffered(1))


def _inproj_body(x_ref, pos_ref, nw_ref, wbig_ref, wsm_ref, qan_ref, kvn_ref, wuq_ref, wukv_ref, rope_ref,
                 qkv_ref, z_ref, gate_ref, ab_ref, q_ref, k_ref, v_ref):
    d = x_ref.shape[1]
    u = _rms(x_ref[...], nw_ref[...]).astype(BF16)
    qkv_w = qkv_ref.shape[1]
    qkv_ref[...] = _dot(u, wbig_ref[:, :qkv_w])
    z_ref[...] = _dot(u, wbig_ref[:, qkv_w:qkv_w + d])
    gate_ref[...] = _dot(u, wbig_ref[:, qkv_w + d:])
    sm = _dot(u, wsm_ref[...])
    cq = sm[:, :MLA_Q_LORA]
    ckv = sm[:, MLA_Q_LORA:MLA_Q_LORA + MLA_KV_LORA]
    o = MLA_Q_LORA + MLA_KV_LORA
    kr, krs, ab = sm[:, o:o + LANES], sm[:, o + LANES:o + 2 * LANES], sm[:, o + 2 * LANES:]
    ab_ref[...] = ab

    ang = pos_ref[...] * rope_ref[0:1, :]
    cosv = jnp.cos(ang) * rope_ref[1:2, :]
    sinv = jnp.sin(ang) * rope_ref[2:3, :]
    kpe = (kr * cosv + krs * sinv).astype(BF16)

    cqn = _rms(cq, qan_ref[...]).astype(BF16)
    q3 = _dot(cqn, wuq_ref[...])
    hw = HEADS * HEAD_DIM
    scale = MLA_QK_DIM ** -0.5 * LOG2_E
    for h in range(HEADS):
        lo = h * HEAD_DIM
        qn = q3[:, lo:lo + HEAD_DIM]
        qp = q3[:, hw + lo:hw + lo + HEAD_DIM] * cosv + q3[:, 2 * hw + lo:2 * hw + lo + HEAD_DIM] * sinv
        q_ref[:, 2 * lo:2 * lo + HEAD_DIM] = (qn * scale).astype(BF16)
        q_ref[:, 2 * lo + HEAD_DIM:2 * lo + 2 * HEAD_DIM] = (qp * scale).astype(BF16)

    ckvn = _rms(ckv, kvn_ref[...]).astype(BF16)
    kv = _dot(ckvn, wukv_ref[...])
    ones = jnp.ones((kv.shape[0], HEAD_DIM), BF16)
    for h in range(HEADS):
        lo = h * HEAD_DIM
        k_ref[:, 2 * lo:2 * lo + HEAD_DIM] = kv[:, lo:lo + HEAD_DIM].astype(BF16)
        k_ref[:, 2 * lo + HEAD_DIM:2 * lo + 2 * HEAD_DIM] = kpe
        v_ref[:, 2 * lo:2 * lo + HEAD_DIM] = kv[:, hw + lo:hw + lo + HEAD_DIM].astype(BF16)
        v_ref[:, 2 * lo + HEAD_DIM:2 * lo + 2 * HEAD_DIM] = ones


def _inproj(x2, pos2, nw, wbig, wsm, qan, kvn, wuq, wukv, rope_tab, tm):
    t, d = x2.shape
    qkv_w = 3 * HEADS * HEAD_DIM
    hw = HEADS * HEAD_DIM
    row = lambda w: pl.BlockSpec((tm, w), lambda i: (i, 0))
    outs = [(qkv_w, F32), (d, F32), (2 * d, F32), (LANES, F32), (2 * hw, BF16), (2 * hw, BF16), (2 * hw, BF16)]
    return pl.pallas_call(
        _inproj_body,
        grid=(t // tm,),
        in_specs=[row(d), row(1), _resident(nw.shape), _resident(wbig.shape), _resident(wsm.shape),
                  _resident(qan.shape), _resident(kvn.shape), _resident(wuq.shape), _resident(wukv.shape),
                  _resident(rope_tab.shape)],
        out_specs=[row(w) for w, _ in outs],
        out_shape=[jax.ShapeDtypeStruct((t, w), dt) for w, dt in outs],
        compiler_params=pltpu.CompilerParams(dimension_semantics=("arbitrary",), vmem_limit_bytes=VMEM_LIMIT),
        name="inproj",
    )(x2, pos2, nw, wbig, wsm, qan, kvn, wuq, wukv, rope_tab)


def _gdn_body(qkv_ref, ab_ref, z_ref, convw_ref, gpar_ref, nw_ref, o_ref, buf_ref, state_ref):
    c = CHUNK
    hw = HEADS * HEAD_DIM
    tail = 8

    @pl.when(pl.program_id(1) == 0)
    def _():
        buf_ref[c:c + tail, :] = jnp.zeros((tail, buf_ref.shape[1]), F32)
        state_ref[...] = jnp.zeros_like(state_ref)

    buf_ref[0:tail, :] = buf_ref[c:c + tail, :]
    buf_ref[tail:tail + c, :] = qkv_ref[...]

    def conv_silu(lo):
        acc = None
        for j in range(CONV_WIDTH):
            r0 = tail - (CONV_WIDTH - 1) + j
            term = buf_ref[r0:r0 + c, lo:lo + HEAD_DIM] * convw_ref[j:j + 1, lo:lo + HEAD_DIM]
            acc = term if acc is None else acc + term
        return _silu(acc)

    ab = ab_ref[...]
    xg = ab + gpar_ref[0:1, :]
    softplus = jnp.maximum(xg, 0.0) + jnp.log(1.0 + jnp.exp(-jnp.abs(xg)))
    g = -jnp.exp(gpar_ref[1:2, :]) * softplus
    beta_all = jax.nn.sigmoid(ab)

    ri = lax.broadcasted_iota(jnp.int32, (c, c), 0)
    ci = lax.broadcasted_iota(jnp.int32, (c, c), 1)
    incl = ri >= ci
    strict = ri > ci
    eye = (ri == ci).astype(F32)
    m16 = ((ri // 16) == (ci // 16)).astype(F32)

    gc_col = jnp.dot(incl.astype(F32), g, preferred_element_type=F32, precision=lax.Precision.HIGHEST)
    g_t = jnp.concatenate([g, jnp.zeros_like(g)], axis=0).T
    r2 = lax.broadcasted_iota(jnp.int32, (LANES, LANES), 0)
    c2 = lax.broadcasted_iota(jnp.int32, (LANES, LANES), 1)
    gc_row = jnp.dot(g_t[0:HEADS, :], (r2 <= c2).astype(F32), preferred_element_type=F32,
                     precision=lax.Precision.HIGHEST)

    hs = range(HEADS)
    l2 = lambda t: t * lax.rsqrt(jnp.sum(t * t, axis=-1, keepdims=True) + NORM_EPS)
    q = [l2(conv_silu(h * HEAD_DIM)) * (HEAD_DIM ** -0.5) for h in hs]
    k = [l2(conv_silu(hw + h * HEAD_DIM)) for h in hs]
    v = [conv_silu(2 * hw + h * HEAD_DIM) for h in hs]
    gcol = [gc_col[:, h:h + 1] for h in hs]
    beta = [beta_all[:, HEADS + h:HEADS + h + 1] for h in hs]
    decay = [jnp.exp(jnp.where(incl, gcol[h] - gc_row[h:h + 1, 0:c], -jnp.inf)) for h in hs]
    egc = [jnp.exp(t) for t in gcol]
    gend = [t[c - 1:c, :] for t in gcol]
    kb = [k[h] * beta[h] for h in hs]
    kbf = [t.astype(BF16) for t in k]
    low = [jnp.where(strict, _dot_nt(kb[h].astype(BF16), kbf[h]) * decay[h], 0.0) for h in hs]
    rhs = [jnp.concatenate([v[h] * beta[h], kb[h] * egc[h]], axis=1) for h in hs]
    qk = [(_dot_nt(q[h].astype(BF16), kbf[h]) * decay[h]).astype(BF16) for h in hs]

    ld = [t * m16 for t in low]
    off = [low[h] - ld[h] for h in hs]
    td = [eye - t for t in ld]
    p = [_dot3(t, t) for t in ld]
    td = [td[h] + _dot3(td[h], p[h]) for h in hs]
    p = [_dot3(t, t) for t in p]
    td = [td[h] + _dot3(td[h], p[h]) for h in hs]
    p = [_dot3(t, t) for t in p]
    td = [td[h] + _dot3(td[h], p[h]) for h in hs]
    y = [_dot3(td[h], rhs[h]) for h in hs]
    m = [_dot3(td[h], off[h]) for h in hs]
    mm = [_dot3(t, t) for t in m]
    y = [y[h] - _dot3(m[h], y[h]) for h in hs]
    sol = [y[h] + _dot3(mm[h], y[h]) for h in hs]

    state = [state_ref[h] for h in hs]
    wq = [jnp.concatenate([sol[h][:, HEAD_DIM:], q[h] * egc[h]], axis=0).astype(BF16) for h in hs]
    ws_qs = [_dot(wq[h], state[h].astype(BF16)) for h in hs]
    vnb = [(sol[h][:, :HEAD_DIM] - ws_qs[h][:c]).astype(BF16) for h in hs]
    o = [ws_qs[h][c:] + _dot(qk[h], vnb[h]) for h in hs]
    k_end = [(k[h] * jnp.exp(gend[h] - gcol[h])).astype(BF16) for h in hs]
    for h in hs:
        state_ref[h] = state[h] * jnp.exp(gend[h]) + _dot_tn(k_end[h], vnb[h])
    for h in hs:
        lo = h * HEAD_DIM
        out = _rms(o[h], nw_ref[...]) * _silu(z_ref[:, lo:lo + HEAD_DIM])
        o_ref[:, lo:lo + HEAD_DIM] = out.astype(o_ref.dtype)


def _gdn(qkv, ab, z, convw, gpar, nw, batch, seq):
    t, qkv_w = qkv.shape
    hw = HEADS * HEAD_DIM
    n = seq // CHUNK
    row = lambda w: pl.BlockSpec((CHUNK, w), lambda b, s: (b * n + s, 0))
    return pl.pallas_call(
        _gdn_body,
        grid=(batch, n),
        in_specs=[row(qkv_w), row(LANES), row(hw), _resident(convw.shape), _resident(gpar.shape),
                  _resident(nw.shape)],
        out_specs=row(hw),
        out_shape=jax.ShapeDtypeStruct((t, hw), BF16),
        scratch_shapes=[pltpu.VMEM((CHUNK + 8, qkv_w), F32), pltpu.VMEM((HEADS, HEAD_DIM, HEAD_DIM), F32)],
        compiler_params=pltpu.CompilerParams(dimension_semantics=("arbitrary", "arbitrary"),
                                             vmem_limit_bytes=VMEM_LIMIT),
        name="gdn",
    )(qkv, ab, z, convw, gpar, nw)


def _attn_body(q_ref, k_ref, v_ref, o_ref, *, th):
    qi = pl.program_id(2)
    qa = q_ref[0:th, :]
    qb = q_ref[th:2 * th, :]

    def scores(qx, off):
        return _dot_nt(qx, k_ref[pl.ds(off, th), :])

    def update(st, s, off):
        m, acc = st
        m_new = jnp.maximum(m, jnp.max(s, axis=-1, keepdims=True))
        p = jnp.exp2(s - m_new).astype(BF16)
        acc = jnp.exp2(m - m_new) * acc + _dot(p, v_ref[pl.ds(off, th), :])
        return m_new, acc

    def body(j, carry):
        sta, stb = carry
        o0 = pl.multiple_of(j * (2 * th), 2 * th)
        o1 = o0 + th
        s_a0, s_b0, s_a1, s_b1 = scores(qa, o0), scores(qb, o0), scores(qa, o1), scores(qb, o1)
        sta = update(sta, s_a0, o0)
        stb = update(stb, s_b0, o0)
        sta = update(sta, s_a1, o1)
        stb = update(stb, s_b1, o1)
        return sta, stb

    init = (jnp.full((th, 1), -jnp.inf, F32), jnp.zeros((th, 2 * HEAD_DIM), F32))
    sta, stb = lax.fori_loop(0, qi, body, (init, init))

    d0 = pl.multiple_of(qi * (2 * th), 2 * th)
    d1 = d0 + th
    rq = lax.broadcasted_iota(jnp.int32, (th, th), 0) // CHUNK
    ck = lax.broadcasted_iota(jnp.int32, (th, th), 1) // CHUNK
    visible = ck <= rq
    s_ad, s_b0, s_bd = scores(qa, d0), scores(qb, d0), scores(qb, d1)
    sta = update(sta, jnp.where(visible, s_ad, -jnp.inf), d0)
    stb = update(stb, s_b0, d0)
    stb = update(stb, jnp.where(visible, s_bd, -jnp.inf), d1)
    for (_, acc), r0 in ((sta, 0), (stb, th)):
        o_ref[r0:r0 + th, :] = (acc[:, :HEAD_DIM] / acc[:, HEAD_DIM:]).astype(o_ref.dtype)


def _attn(q, k, v, batch, seq, th):
    hw = HEADS * HEAD_DIM
    tq = 2 * th
    wide = lambda a: a.reshape(batch, seq, 2 * hw)
    head_all = pl.BlockSpec((None, seq, 2 * HEAD_DIM), lambda b, h, i: (b, 0, h))
    out = pl.pallas_call(
        functools.partial(_attn_body, th=th),
        grid=(batch, HEADS, seq // tq),
        in_specs=[pl.BlockSpec((None, tq, 2 * HEAD_DIM), lambda b, h, i: (b, i, h)), head_all, head_all],
        out_specs=pl.BlockSpec((None, tq, HEAD_DIM), lambda b, h, i: (b, i, h)),
        out_shape=jax.ShapeDtypeStruct((batch, seq, hw), BF16),
        compiler_params=pltpu.CompilerParams(dimension_semantics=("arbitrary", "arbitrary", "arbitrary"),
                                             vmem_limit_bytes=VMEM_LIMIT),
        name="attn",
    )(wide(q), wide(k), wide(v))
    return out.reshape(batch * seq, hw)


def _mix_body(oa_ref, at_ref, g_ref, x_ref, wa_ref, wb_ref, wo_ref, nw_ref, o_ref):
    d = x_ref.shape[1]
    ya = _dot(oa_ref[...], wa_ref[...])
    yb = _dot(at_ref[...], wb_ref[...])
    h = jax.nn.sigmoid(g_ref[:, :d]) * ya + jax.nn.sigmoid(g_ref[:, d:]) * yb
    y = _dot(h.astype(BF16), wo_ref[...])
    o_ref[...] = x_ref[...] + _rms(y, nw_ref[...])


def _mix(oa, at, gate, x2, wa, wb, wo, nw, tm):
    t, d = x2.shape
    row = lambda w: pl.BlockSpec((tm, w), lambda i: (i, 0))
    return pl.pallas_call(
        _mix_body,
        grid=(t // tm,),
        in_specs=[row(oa.shape[1]), row(at.shape[1]), row(2 * d), row(d), _resident(wa.shape),
                  _resident(wb.shape), _resident(wo.shape), _resident(nw.shape)],
        out_specs=row(d),
        out_shape=jax.ShapeDtypeStruct((t, d), F32),
        compiler_params=pltpu.CompilerParams(dimension_semantics=("arbitrary",), vmem_limit_bytes=VMEM_LIMIT),
        name="mix",
    )(oa, at, gate, x2, wa, wb, wo, nw)


def _ffn_body(x_ref, n1_ref, w1_ref, w2_ref, n2_ref, o_ref, *, fc):
    x = x_ref[...]
    u = _rms(x, n1_ref[...]).astype(BF16)
    acc = None
    for c0 in range(0, w1_ref.shape[1], fc):
        hcol = jnp.maximum(_dot(u, w1_ref[:, c0:c0 + fc]), 0.0)
        part = _dot((hcol * hcol).astype(BF16), w2_ref[c0:c0 + fc, :])
        acc = part if acc is None else acc + part
    o_ref[...] = x + _rms(acc, n2_ref[...])


def _ffn(x2, n1, w1, w2, n2, tm):
    t, d = x2.shape
    row = pl.BlockSpec((tm, d), lambda i: (i, 0))
    return pl.pallas_call(
        functools.partial(_ffn_body, fc=min(1024, w1.shape[1])),
        grid=(t // tm,),
        in_specs=[row, _resident(n1.shape), _resident(w1.shape), _resident(w2.shape), _resident(n2.shape)],
        out_specs=row,
        out_shape=jax.ShapeDtypeStruct((t, d), F32),
        compiler_params=pltpu.CompilerParams(dimension_semantics=("arbitrary",), vmem_limit_bytes=VMEM_LIMIT),
        name="ffn",
    )(x2, n1, w1, w2, n2)


def _rope_table():
    inv_freq = ROPE_BASE ** (-jnp.arange(0, MLA_ROPE, 2, dtype=F32) / MLA_ROPE)
    half = MLA_ROPE // 2
    zeros = jnp.zeros((LANES - MLA_ROPE,), F32)
    rows = [jnp.concatenate([inv_freq, inv_freq, zeros]),
            jnp.concatenate([jnp.ones((MLA_ROPE,), F32), zeros]),
            jnp.concatenate([-jnp.ones((half,), F32), jnp.ones((half,), F32), zeros])]
    return jnp.concatenate([jnp.stack(rows), jnp.zeros((5, LANES), F32)], axis=0)


def _pad_cols(w, width):
    return jnp.pad(w, ((0, 0), (0, width - w.shape[1])))


def _layer_weights(w_in, w_uq, w_ukv):
    d = w_in.shape[0]
    hw = HEADS * HEAD_DIM
    o = 3 * hw
    qkv, z = w_in[:, :o], w_in[:, o:o + hw]
    o += hw
    a_b = w_in[:, o:o + 2 * HEADS]
    o += 2 * HEADS
    cq = w_in[:, o:o + MLA_Q_LORA]
    o += MLA_Q_LORA
    ckv = w_in[:, o:o + MLA_KV_LORA]
    o += MLA_KV_LORA
    kr = w_in[:, o:o + MLA_ROPE]
    o += MLA_ROPE
    gates = w_in[:, o:]
    half = MLA_ROPE // 2
    krs = jnp.concatenate([kr[:, half:], kr[:, :half]], axis=1)
    wbig = jnp.concatenate([qkv, z, gates], axis=1).astype(BF16)
    wsm = jnp.concatenate([cq, ckv, _pad_cols(kr, LANES), _pad_cols(krs, LANES), _pad_cols(a_b, LANES)],
                          axis=1).astype(BF16)
    uq = w_uq.reshape(MLA_Q_LORA, HEADS, MLA_QK_DIM)
    nope, pe = uq[:, :, :HEAD_DIM], uq[:, :, HEAD_DIM:]
    pes = jnp.concatenate([pe[:, :, half:], pe[:, :, :half]], axis=2)
    padh = lambda w: jnp.pad(w, ((0, 0), (0, 0), (0, HEAD_DIM - MLA_ROPE))).reshape(MLA_Q_LORA, hw)
    wuq = jnp.concatenate([nope.reshape(MLA_Q_LORA, hw), padh(pe), padh(pes)], axis=1).astype(BF16)
    ukv = w_ukv.reshape(MLA_KV_LORA, HEADS, 2 * HEAD_DIM)
    wukv = jnp.concatenate([ukv[:, :, :HEAD_DIM].reshape(MLA_KV_LORA, hw),
                            ukv[:, :, HEAD_DIM:].reshape(MLA_KV_LORA, hw)], axis=1).astype(BF16)
    return wbig, wsm, wuq, wukv


def kernel(x, positions, w_in, conv_w, A_log, dt_bias, gdn_norm_w, q_a_norm_w, w_uq, kv_a_norm_w, w_ukv, w_branch_a, w_branch_b, w_out, pre_mix_norm_w, post_mix_norm_w, pre_ffn_norm_w, post_ffn_norm_w, w_ff1, w_ff2):
    batch, seq, d = x.shape
    t = batch * seq
    depth = w_in.shape[0]
    tm_proj = min(256, t)
    tm_mlp = min(512, t)
    th = min(512, seq // 2)
    x2 = x.reshape(t, d)
    pos2 = positions.astype(F32).reshape(t, 1)
    rope_tab = _rope_table()
    row = lambda v: v.reshape(1, -1)
    for l in range(depth):
        wbig, wsm, wuq, wukv = _layer_weights(w_in[l], w_uq[l], w_ukv[l])
        qkv, z, gate, ab, q, kc, va = _inproj(
            x2, pos2, row(pre_mix_norm_w[l]), wbig, wsm, row(q_a_norm_w[l]), row(kv_a_norm_w[l]), wuq, wukv,
            rope_tab, tm_proj)
        convw = jnp.pad(conv_w[l], ((0, 8 - CONV_WIDTH), (0, 0)))
        gpar = jnp.zeros((8, LANES), F32).at[0, :HEADS].set(dt_bias[l]).at[1, :HEADS].set(A_log[l])
        oa = _gdn(qkv, ab, z, convw, gpar, row(gdn_norm_w[l]), batch, seq)
        at = _attn(q, kc, va, batch, seq, th)
        x2 = _mix(oa, at, gate, x2, w_branch_a[l].astype(BF16), w_branch_b[l].astype(BF16),
                  w_out[l].astype(BF16), row(post_mix_norm_w[l]), tm_mlp)
        x2 = _ffn(x2, row(pre_ffn_norm_w[l]), w_ff1[l].astype(BF16), w_ff2[l].astype(BF16),
                  row(post_ffn_norm_w[l]), tm_mlp)
    return x2.reshape(batch, seq, d)
```

```python
import functools

import jax
import jax.numpy as jnp
from jax import lax
from jax.experimental import pallas as pl
from jax.experimental.pallas import tpu as pltpu

F32 = jnp.float32
BF16 = jnp.bfloat16

LANES = 128
CHUNK = 64
NORM_EPS = 1e-6
HEADS = 8
HEAD_DIM = 128
CONV_WIDTH = 4
CONV_TAIL = 8
MLA_Q_LORA = 384
MLA_KV_LORA = 256
MLA_ROPE = 64
MLA_QK_DIM = HEAD_DIM + MLA_ROPE
ROPE_BASE = 10000.0
LOG2_E = 1.4426950408889634
GDN_CHUNKS_PER_STEP = 4
VMEM_LIMIT = 56 * 1024 * 1024


def _rms(x, w):
    return x * lax.rsqrt(jnp.mean(x * x, axis=-1, keepdims=True) + NORM_EPS) * w


def _silu(x):
    return x * jax.nn.sigmoid(x)


def _dot(a, b):
    return jnp.dot(a, b, preferred_element_type=F32)


def _dot_nt(a, b):
    return lax.dot_general(a, b, (((1,), (1,)), ((), ())), preferred_element_type=F32)


def _dot_tn(a, b):
    return lax.dot_general(a, b, (((0,), (0,)), ((), ())), preferred_element_type=F32)


def _dotb(a, b):
    return _dot(a.astype(BF16), b.astype(BF16))


def _resident(shape):
    nd = len(shape)
    return pl.BlockSpec(shape, lambda *_: (0,) * nd, pipeline_mode=pl.Buffered(1))


def _rope_body(pos_ref, rope_ref, cos_ref, sin_ref):
    ang = pos_ref[...] * rope_ref[0:1, :]
    cos_ref[...] = jnp.cos(ang) * rope_ref[1:2, :]
    sin_ref[...] = jnp.sin(ang) * rope_ref[2:3, :]


def _rope(pos2, rope_tab, tm):
    t = pos2.shape[0]
    out = pl.BlockSpec((tm, LANES), lambda i: (i, 0))
    return pl.pallas_call(
        _rope_body,
        grid=(t // tm,),
        in_specs=[pl.BlockSpec((tm, 1), lambda i: (i, 0)), _resident(rope_tab.shape)],
        out_specs=[out, out],
        out_shape=[jax.ShapeDtypeStruct((t, LANES), F32)] * 2,
        compiler_params=pltpu.CompilerParams(dimension_semantics=("arbitrary",), vmem_limit_bytes=VMEM_LIMIT),
        name="rope",
    )(pos2, rope_tab)


def _inproj_body(x_ref, cos_ref, sin_ref, nw_ref, wbig_ref, wsm_ref, convw_ref, qan_ref, kvn_ref, wuq_ref, wukv_ref,
                 qkv_ref, z_ref, gate_ref, ab_ref, q_ref, k_ref, v_ref, buf_ref, *, tiles_per_seq):
    tm, d = x_ref.shape
    hw = HEADS * HEAD_DIM
    qkv_w = qkv_ref.shape[1]
    u = _rms(x_ref[...], nw_ref[...]).astype(BF16)

    @pl.when(pl.program_id(0) % tiles_per_seq == 0)
    def _():
        buf_ref[tm:tm + CONV_TAIL, :] = jnp.zeros((CONV_TAIL, qkv_w), F32)

    buf_ref[0:CONV_TAIL, :] = buf_ref[tm:tm + CONV_TAIL, :]

    sm = _dot(u, wsm_ref[...])
    cq = sm[:, :MLA_Q_LORA]
    ckv = sm[:, MLA_Q_LORA:MLA_Q_LORA + MLA_KV_LORA]
    o = MLA_Q_LORA + MLA_KV_LORA
    kr, krs, ab = sm[:, o:o + LANES], sm[:, o + LANES:o + 2 * LANES], sm[:, o + 2 * LANES:]
    ab_ref[...] = ab

    cosv = cos_ref[...]
    sinv = sin_ref[...]
    kpe =(kr * cosv + krs * sinv).astype(BF16)

    cqn = _rms(cq, qan_ref[...]).astype(BF16)
    q3 = _dot(cqn, wuq_ref[...])
    scale = MLA_QK_DIM ** -0.5 * LOG2_E
    for h in range(HEADS):
        lo = h * HEAD_DIM
        qn = q3[:, lo:lo + HEAD_DIM]
        qp = q3[:, hw + lo:hw + lo + HEAD_DIM] * cosv + q3[:, 2 * hw + lo:2 * hw + lo + HEAD_DIM] * sinv
        q_ref[:, 2 * lo:2 * lo + HEAD_DIM] = (qn * scale).astype(BF16)
        q_ref[:, 2 * lo + HEAD_DIM:2 * lo + 2 * HEAD_DIM] = (qp * scale).astype(BF16)

    ckvn = _rms(ckv, kvn_ref[...]).astype(BF16)
    kv = _dot(ckvn, wukv_ref[...])
    ones = jnp.ones((tm, HEAD_DIM), BF16)
    for h in range(HEADS):
        lo = h * HEAD_DIM
        k_ref[:, 2 * lo:2 * lo + HEAD_DIM] = kv[:, lo:lo + HEAD_DIM].astype(BF16)
        k_ref[:, 2 * lo + HEAD_DIM:2 * lo + 2 * HEAD_DIM] = kpe
        v_ref[:, 2 * lo:2 * lo + HEAD_DIM] = kv[:, hw + lo:hw + lo + HEAD_DIM].astype(BF16)
        v_ref[:, 2 * lo + HEAD_DIM:2 * lo + 2 * HEAD_DIM] = ones

    sl = 2 * HEAD_DIM
    for s in range(qkv_w // sl):
        buf_ref[CONV_TAIL:CONV_TAIL + tm, s * sl:(s + 1) * sl] = _dot(u, wbig_ref[:, s * sl:(s + 1) * sl])
        zg = _dot(u, wbig_ref[:, qkv_w + s * sl:qkv_w + (s + 1) * sl])
        if s * sl < d:
            z_ref[:, s * sl:(s + 1) * sl] = zg
        else:
            gate_ref[:, s * sl - d:(s + 1) * sl - d] = zg
        for grp in range(2 * s, 2 * s + 2):
            lo = grp * HEAD_DIM
            acc = None
            for j in range(CONV_WIDTH):
                r0 = CONV_TAIL - (CONV_WIDTH - 1) + j
                term = buf_ref[r0:r0 + tm, lo:lo + HEAD_DIM] * convw_ref[j:j + 1, lo:lo + HEAD_DIM]
                acc = term if acc is None else acc + term
            y = _silu(acc)
            if grp < 2 * HEADS:
                y = y * lax.rsqrt(jnp.sum(y * y, axis=-1, keepdims=True) + NORM_EPS)
            if grp < HEADS:
                y = y * (HEAD_DIM ** -0.5)
            qkv_ref[:, lo:lo + HEAD_DIM] = y.astype(BF16)


def _inproj(x2, cosv, sinv, nw, wbig, wsm, convw, qan, kvn, wuq, wukv, tm, seq):
    t, d = x2.shape
    qkv_w = 3 * HEADS * HEAD_DIM
    hw = HEADS * HEAD_DIM
    row = lambda w: pl.BlockSpec((tm, w), lambda i: (i, 0))
    outs = [(qkv_w, BF16), (d, F32), (2 * d, F32), (LANES, F32), (2 * hw, BF16), (2 * hw, BF16), (2 * hw, BF16)]
    return pl.pallas_call(
        functools.partial(_inproj_body, tiles_per_seq=seq // tm),
        grid=(t // tm,),
        in_specs=[row(d), row(LANES), row(LANES), _resident(nw.shape), _resident(wbig.shape), _resident(wsm.shape),
                  _resident(convw.shape), _resident(qan.shape), _resident(kvn.shape), _resident(wuq.shape),
                  _resident(wukv.shape)],
        out_specs=[row(w) for w, _ in outs],
        out_shape=[jax.ShapeDtypeStruct((t, w), dt) for w, dt in outs],
        scratch_shapes=[pltpu.VMEM((tm + CONV_TAIL, qkv_w), F32)],
        compiler_params=pltpu.CompilerParams(dimension_semantics=("arbitrary",), vmem_limit_bytes=VMEM_LIMIT),
        name="inproj",
    )(x2, cosv, sinv, nw, wbig, wsm, convw, qan, kvn, wuq, wukv)


def _gdn_body(qkv_ref, ab_ref, z_ref, gpar_ref, nw_ref, o_ref, state_ref, *, nc):
    c = CHUNK
    ts = nc * c
    hw = HEADS * HEAD_DIM

    @pl.when(pl.program_id(1) == 0)
    def _():
        state_ref[...] = jnp.zeros_like(state_ref)

    ab = ab_ref[...]
    xg = ab + gpar_ref[0:1, :]
    softplus = jnp.maximum(xg, 0.0) + jnp.log(1.0 + jnp.exp(-jnp.abs(xg)))
    g = -jnp.exp(gpar_ref[1:2, :]) * softplus
    beta_all = jax.nn.sigmoid(ab)

    rt = lax.broadcasted_iota(jnp.int32, (ts, ts), 0)
    ct = lax.broadcasted_iota(jnp.int32, (ts, ts), 1)
    same = (rt // c) == (ct // c)
    gc_col = jnp.dot(jnp.logical_and(same, rt >= ct).astype(F32), g, preferred_element_type=F32,
                     precision=lax.Precision.HIGHEST)
    gc_row = jnp.dot(g.T[0:HEADS, :], jnp.logical_and(same, rt <= ct).astype(F32), preferred_element_type=F32,
                     precision=lax.Precision.HIGHEST)

    ri = lax.broadcasted_iota(jnp.int32, (c, c), 0)
    ci = lax.broadcasted_iota(jnp.int32, (c, c), 1)
    incl = ri >= ci
    strict = ri > ci
    eye = (ri == ci).astype(F32)
    m16 = ((ri // 16) == (ci // 16)).astype(F32)

    its = [(ch, h) for ch in range(nc) for h in range(HEADS)]
    n = range(len(its))
    rows = lambda ch: slice(ch * c, (ch + 1) * c)
    q = [qkv_ref[rows(ch), h * HEAD_DIM:(h + 1) * HEAD_DIM] for ch, h in its]
    k = [qkv_ref[rows(ch), hw + h * HEAD_DIM:hw + (h + 1) * HEAD_DIM] for ch, h in its]
    v = [qkv_ref[rows(ch), 2 * hw + h * HEAD_DIM:2 * hw + (h + 1) * HEAD_DIM].astype(F32) for ch, h in its]
    kf = [t.astype(F32) for t in k]
    gcol = [gc_col[rows(ch), h:h + 1] for ch, h in its]
    beta = [beta_all[rows(ch), HEADS + h:HEADS + h + 1] for ch, h in its]
    decay = [jnp.exp(jnp.where(incl, gcol[i] - gc_row[h:h + 1, rows(ch)], -jnp.inf)) for i, (ch, h) in enumerate(its)]
    egc = [jnp.exp(t) for t in gcol]
    gend = [t[c - 1:c, :] for t in gcol]
    kb = [kf[i] * beta[i] for i in n]
    low = [jnp.where(strict, _dot_nt(kb[i].astype(BF16), k[i]) * decay[i], 0.0) for i in n]
    rhs = [jnp.concatenate([v[i] * beta[i], kb[i] * egc[i]], axis=1) for i in n]
    qk = [(_dot_nt(q[i], k[i]) * decay[i]).astype(BF16) for i in n]

    ld = [t * m16 for t in low]
    off = [low[i] - ld[i] for i in n]
    td = [eye - t for t in ld]
    p = [_dotb(t, t) for t in ld]
    td = [td[i] + _dotb(td[i], p[i]) for i in n]
    p = [_dotb(t, t) for t in p]
    td = [td[i] + _dotb(td[i], p[i]) for i in n]
    p = [_dotb(t, t) for t in p]
    td = [td[i] + _dotb(td[i], p[i]) for i in n]
    y = [_dotb(td[i], rhs[i]) for i in n]
    m = [_dotb(td[i], off[i]) for i in n]
    mm = [_dotb(t, t) for t in m]
    y = [y[i] - _dotb(m[i], y[i]) for i in n]
    sol = [y[i] + _dotb(mm[i], y[i]) for i in n]
    wq = [jnp.concatenate([sol[i][:, HEAD_DIM:], q[i].astype(F32) * egc[i]], axis=0).astype(BF16) for i in n]
    k_end = [(kf[i] * jnp.exp(gend[i] - gcol[i])).astype(BF16) for i in n]

    state = [state_ref[h] for h in range(HEADS)]
    for ch in range(nc):
        idx = [ch * HEADS + h for h in range(HEADS)]
        ws_qs = [_dot(wq[i], state[h].astype(BF16)) for h, i in enumerate(idx)]
        vnb = [(sol[i][:, :HEAD_DIM] - ws_qs[h][:c]).astype(BF16) for h, i in enumerate(idx)]
        o = [ws_qs[h][c:] + _dot(qk[i], vnb[h]) for h, i in enumerate(idx)]
        state = [state[h] * jnp.exp(gend[i]) + _dot_tn(k_end[i], vnb[h]) for h, i in enumerate(idx)]
        for h in range(HEADS):
            lo = h * HEAD_DIM
            out = _rms(o[h], nw_ref[...]) * _silu(z_ref[rows(ch), lo:lo + HEAD_DIM])
            o_ref[rows(ch), lo:lo + HEAD_DIM] = out.astype(o_ref.dtype)
    for h in range(HEADS):
        state_ref[h] = state[h]


def _gdn(qkv, ab, z, gpar, nw, batch, seq, nc):
    t, qkv_w = qkv.shape
    hw = HEADS * HEAD_DIM
    ts = nc * CHUNK
    assert ts % LANES == 0 and seq % ts == 0
    n = seq // ts
    row = lambda w: pl.BlockSpec((ts, w), lambda b, s: (b * n + s, 0))
    return pl.pallas_call(
        functools.partial(_gdn_body, nc=nc),
        grid=(batch, n),
        in_specs=[row(qkv_w), row(LANES), row(hw), _resident(gpar.shape), _resident(nw.shape)],
        out_specs=row(hw),
        out_shape=jax.ShapeDtypeStruct((t, hw), BF16),
        scratch_shapes=[pltpu.VMEM((HEADS, HEAD_DIM, HEAD_DIM), F32)],
        compiler_params=pltpu.CompilerParams(dimension_semantics=("arbitrary", "arbitrary"),
                                             vmem_limit_bytes=VMEM_LIMIT),
        name="gdn",
    )(qkv, ab, z, gpar, nw)


def _attn_body(q_ref, k_ref, v_ref, o_ref, *, th):
    qi = pl.program_id(2)
    qa = q_ref[0:th, :]
    qb = q_ref[th:2 * th, :]

    def scores(qx, off):
        return _dot_nt(qx, k_ref[pl.ds(off, th), :])

    def update(st, s, off):
        m, acc = st
        m_new = jnp.maximum(m, jnp.max(s, axis=-1, keepdims=True))
        p = jnp.exp2(s - m_new).astype(BF16)
        acc = jnp.exp2(m - m_new) * acc + _dot(p, v_ref[pl.ds(off, th), :])
        return m_new, acc

    def body(j, carry):
        sta, stb = carry
        o0 = pl.multiple_of(j * (2 * th), 2 * th)
        o1 = o0 + th
        s_a0, s_b0, s_a1, s_b1 = scores(qa, o0), scores(qb, o0), scores(qa, o1), scores(qb, o1)
        sta = update(sta, s_a0, o0)
        stb = update(stb, s_b0, o0)
        sta = update(sta, s_a1, o1)
        stb = update(stb, s_b1, o1)
        return sta, stb

    init = (jnp.full((th, 1), -jnp.inf, F32), jnp.zeros((th, 2 * HEAD_DIM), F32))
    sta, stb = lax.fori_loop(0, qi, body, (init, init))

    d0 = pl.multiple_of(qi * (2 * th), 2 * th)
    d1 = d0 + th
    rq = lax.broadcasted_iota(jnp.int32, (th, th), 0) // CHUNK
    ck = lax.broadcasted_iota(jnp.int32, (th, th), 1) // CHUNK
    visible = ck <= rq
    s_ad, s_b0, s_bd = scores(qa, d0), scores(qb, d0), scores(qb, d1)
    sta = update(sta, jnp.where(visible, s_ad, -jnp.inf), d0)
    stb = update(stb, s_b0, d0)
    stb = update(stb, jnp.where(visible, s_bd, -jnp.inf), d1)
    for (_, acc), r0 in ((sta, 0), (stb, th)):
        o_ref[r0:r0 + th, :] = (acc[:, :HEAD_DIM] / acc[:, HEAD_DIM:]).astype(o_ref.dtype)


def _attn(q, k, v, batch, seq, th):
    hw = HEADS * HEAD_DIM
    tq = 2 * th
    wide = lambda a: a.reshape(batch, seq, 2 * hw)
    head_all = pl.BlockSpec((None, seq, 2 * HEAD_DIM), lambda b, h, i: (b, 0, h))
    out = pl.pallas_call(
        functools.partial(_attn_body, th=th),
        grid=(batch, HEADS, seq // tq),
        in_specs=[pl.BlockSpec((None, tq, 2 * HEAD_DIM), lambda b, h, i: (b, i, h)), head_all, head_all],
        out_specs=pl.BlockSpec((None, tq, HEAD_DIM), lambda b, h, i: (b, i, h)),
        out_shape=jax.ShapeDtypeStruct((batch, seq, hw), BF16),
        compiler_params=pltpu.CompilerParams(dimension_semantics=("arbitrary", "arbitrary", "arbitrary"),
                                             vmem_limit_bytes=VMEM_LIMIT),
        name="attn",
    )(wide(q), wide(k), wide(v))
    return out.reshape(batch * seq, hw)


def _mix_body(oa_ref, at_ref, g_ref, x_ref, wa_ref, wb_ref, wo_ref, nw_ref, o_ref):
    d = x_ref.shape[1]
    ya = _dot(oa_ref[...], wa_ref[...])
    yb = _dot(at_ref[...], wb_ref[...])
    h = jax.nn.sigmoid(g_ref[:, :d]) * ya + jax.nn.sigmoid(g_ref[:, d:]) * yb
    y = _dot(h.astype(BF16), wo_ref[...])
    o_ref[...] = x_ref[...] + _rms(y, nw_ref[...])


def _mix(oa, at, gate, x2, wa, wb, wo, nw, tm):
    t, d = x2.shape
    row = lambda w: pl.BlockSpec((tm, w), lambda i: (i, 0))
    return pl.pallas_call(
        _mix_body,
        grid=(t // tm,),
        in_specs=[row(oa.shape[1]), row(at.shape[1]), row(2 * d), row(d), _resident(wa.shape),
                  _resident(wb.shape), _resident(wo.shape), _resident(nw.shape)],
        out_specs=row(d),
        out_shape=jax.ShapeDtypeStruct((t, d), F32),
        compiler_params=pltpu.CompilerParams(dimension_semantics=("arbitrary",), vmem_limit_bytes=VMEM_LIMIT),
        name="mix",
    )(oa, at, gate, x2, wa, wb, wo, nw)


def _ffn_body(x_ref, n1_ref, w1_ref, w2_ref, n2_ref, o_ref, *, fc):
    x = x_ref[...]
    u = _rms(x, n1_ref[...]).astype(BF16)
    acc = None
    for c0 in range(0, w1_ref.shape[1], fc):
        hcol = jnp.maximum(_dot(u, w1_ref[:, c0:c0 + fc]), 0.0)
        part = _dot((hcol * hcol).astype(BF16), w2_ref[c0:c0 + fc, :])
        acc = part if acc is None else acc + part
    o_ref[...] = x + _rms(acc, n2_ref[...])


def _ffn(x2, n1, w1, w2, n2, tm):
    t, d = x2.shape
    row = pl.BlockSpec((tm, d), lambda i: (i, 0))
    return pl.pallas_call(
        functools.partial(_ffn_body, fc=min(1024, w1.shape[1])),
        grid=(t // tm,),
        in_specs=[row, _resident(n1.shape), _resident(w1.shape), _resident(w2.shape), _resident(n2.shape)],
        out_specs=row,
        out_shape=jax.ShapeDtypeStruct((t, d), F32),
        compiler_params=pltpu.CompilerParams(dimension_semantics=("arbitrary",), vmem_limit_bytes=VMEM_LIMIT),
        name="ffn",
    )(x2, n1, w1, w2, n2)


def _rope_table():
    inv_freq = ROPE_BASE ** (-jnp.arange(0, MLA_ROPE, 2, dtype=F32) / MLA_ROPE)
    half = MLA_ROPE // 2
    zeros = jnp.zeros((LANES - MLA_ROPE,), F32)
    rows = [jnp.concatenate([inv_freq, inv_freq, zeros]),
            jnp.concatenate([jnp.ones((MLA_ROPE,), F32), zeros]),
            jnp.concatenate([-jnp.ones((half,), F32), jnp.ones((half,), F32), zeros])]
    return jnp.concatenate([jnp.stack(rows), jnp.zeros((5, LANES), F32)], axis=0)


def _pad_cols(w, width):
    return jnp.pad(w, ((0, 0), (0, width - w.shape[1])))


def _layer_weights(w_in, w_uq, w_ukv):
    hw = HEADS * HEAD_DIM
    o = 3 * hw
    qkv, z = w_in[:, :o], w_in[:, o:o + hw]
    o += hw
    a_b = w_in[:, o:o + 2 * HEADS]
    o += 2 * HEADS
    cq = w_in[:, o:o + MLA_Q_LORA]
    o += MLA_Q_LORA
    ckv = w_in[:, o:o + MLA_KV_LORA]
    o += MLA_KV_LORA
    kr = w_in[:, o:o + MLA_ROPE]
    o += MLA_ROPE
    gates = w_in[:, o:]
    half = MLA_ROPE // 2
    krs = jnp.concatenate([kr[:, half:], kr[:, :half]], axis=1)
    wbig = jnp.concatenate([qkv, z, gates], axis=1).astype(BF16)
    wsm = jnp.concatenate([cq, ckv, _pad_cols(kr, LANES), _pad_cols(krs, LANES), _pad_cols(a_b, LANES)],
                          axis=1).astype(BF16)
    uq = w_uq.reshape(MLA_Q_LORA, HEADS, MLA_QK_DIM)
    nope, pe = uq[:, :, :HEAD_DIM], uq[:, :, HEAD_DIM:]
    pes = jnp.concatenate([pe[:, :, half:], pe[:, :, :half]], axis=2)
    padh = lambda w: jnp.pad(w, ((0, 0), (0, 0), (0, HEAD_DIM - MLA_ROPE))).reshape(MLA_Q_LORA, hw)
    wuq = jnp.concatenate([nope.reshape(MLA_Q_LORA, hw), padh(pe), padh(pes)], axis=1).astype(BF16)
    ukv = w_ukv.reshape(MLA_KV_LORA, HEADS, 2 * HEAD_DIM)
    wukv = jnp.concatenate([ukv[:, :, :HEAD_DIM].reshape(MLA_KV_LORA, hw),
                            ukv[:, :, HEAD_DIM:].reshape(MLA_KV_LORA, hw)], axis=1).astype(BF16)
    return wbig, wsm, wuq, wukv


def kernel(x, positions, w_in, conv_w, A_log, dt_bias, gdn_norm_w, q_a_norm_w, w_uq, kv_a_norm_w, w_ukv, w_branch_a, w_branch_b, w_out, pre_mix_norm_w, post_mix_norm_w, pre_ffn_norm_w, post_ffn_norm_w, w_ff1, w_ff2):
    batch, seq, d = x.shape
    t = batch * seq
    depth = w_in.shape[0]
    tm_proj = min(256, t)
    tm_mlp = min(512, t)
    th = min(512, seq // 2)
    x2 = x.reshape(t, d)
    pos2 = positions.astype(F32).reshape(t, 1)
    cosv, sinv = _rope(pos2, _rope_table(), tm_mlp)
    row = lambda v: v.reshape(1, -1)
    for l in range(depth):
        wbig, wsm, wuq, wukv = _layer_weights(w_in[l], w_uq[l], w_ukv[l])
        convw = jnp.pad(conv_w[l], ((0, 8 - CONV_WIDTH), (0, 0)))
        qkv, z, gate, ab, q, kc, va = _inproj(
            x2, cosv, sinv, row(pre_mix_norm_w[l]), wbig, wsm, convw, row(q_a_norm_w[l]), row(kv_a_norm_w[l]),
            wuq, wukv, tm_proj, seq)
        gpar = jnp.zeros((8, LANES), F32).at[0, :HEADS].set(dt_bias[l]).at[1, :HEADS].set(A_log[l])
        oa = _gdn(qkv, ab, z, gpar, row(gdn_norm_w[l]), batch, seq, GDN_CHUNKS_PER_STEP)
        at = _attn(q, kc, va, batch, seq, th)
        x2 = _mix(oa, at, gate, x2, w_branch_a[l].astype(BF16), w_branch_b[l].astype(BF16),
                  w_out[l].astype(BF16), row(post_mix_norm_w[l]), tm_mlp)
        x2 = _ffn(x2, row(pre_ffn_norm_w[l]), w_ff1[l].astype(BF16), w_ff2[l].astype(BF16),
                  row(post_ffn_norm_w[l]), tm_mlp)
    return x2.reshape(batch, seq, d)
```

```python
import functools

import jax
import jax.numpy as jnp
from jax import lax
from jax.experimental import pallas as pl
from jax.experimental.pallas import tpu as pltpu

F32 = jnp.float32
BF16 = jnp.bfloat16

LANES = 128
CHUNK = 64
NORM_EPS = 1e-6
HEADS = 8
HEAD_DIM = 128
CONV_WIDTH = 4
CONV_TAIL = 8
MLA_Q_LORA = 384
MLA_KV_LORA = 256
MLA_ROPE = 64
MLA_QK_DIM = HEAD_DIM + MLA_ROPE
ROPE_BASE = 10000.0
LOG2_E = 1.4426950408889634
GDN_CHUNKS_PER_STEP = 4
VMEM_LIMIT = 56 * 1024 * 1024


def _rms(x, w):
    return x * lax.rsqrt(jnp.mean(x * x, axis=-1, keepdims=True) + NORM_EPS) * w


def _silu(x):
    return x * jax.nn.sigmoid(x)


def _dot(a, b):
    return jnp.dot(a, b, preferred_element_type=F32)


def _dot_nt(a, b):
    return lax.dot_general(a, b, (((1,), (1,)), ((), ())), preferred_element_type=F32)


def _dot_tn(a, b):
    return lax.dot_general(a, b, (((0,), (0,)), ((), ())), preferred_element_type=F32)


def _dotb(a, b):
    return _dot(a.astype(BF16), b.astype(BF16))


def _resident(shape):
    nd = len(shape)
    return pl.BlockSpec(shape, lambda *_: (0,) * nd, pipeline_mode=pl.Buffered(1))


def _rope_body(pos_ref, rope_ref, cos_ref, sin_ref):
    ang = pos_ref[...] * rope_ref[0:1, :]
    cos_ref[...] = jnp.cos(ang) * rope_ref[1:2, :]
    sin_ref[...] = jnp.sin(ang) * rope_ref[2:3, :]


def _rope(pos2, rope_tab, tm):
    t = pos2.shape[0]
    out = pl.BlockSpec((tm, LANES), lambda i: (i, 0))
    return pl.pallas_call(
        _rope_body,
        grid=(t // tm,),
        in_specs=[pl.BlockSpec((tm, 1), lambda i: (i, 0)), _resident(rope_tab.shape)],
        out_specs=[out, out],
        out_shape=[jax.ShapeDtypeStruct((t, LANES), F32)] * 2,
        compiler_params=pltpu.CompilerParams(dimension_semantics=("arbitrary",), vmem_limit_bytes=VMEM_LIMIT),
        name="rope",
    )(pos2, rope_tab)


def _inproj_body(x_ref, cos_ref, sin_ref, nw_ref, wbig_ref, wsm_ref, convw_ref, qan_ref, kvn_ref, wuq_ref, wukv_ref,
                 qkv_ref, z_ref, gate_ref, ab_ref, q_ref, k_ref, v_ref, buf_ref, *, tiles_per_seq):
    tm, d = x_ref.shape
    hw = HEADS * HEAD_DIM
    qkv_w = qkv_ref.shape[1]
    u = _rms(x_ref[...], nw_ref[...]).astype(BF16)

    @pl.when(pl.program_id(0) % tiles_per_seq == 0)
    def _():
        buf_ref[tm:tm + CONV_TAIL, :] = jnp.zeros((CONV_TAIL, qkv_w), F32)

    buf_ref[0:CONV_TAIL, :] = buf_ref[tm:tm + CONV_TAIL, :]

    sm = _dot(u, wsm_ref[...])
    cq = sm[:, :MLA_Q_LORA]
    ckv = sm[:, MLA_Q_LORA:MLA_Q_LORA + MLA_KV_LORA]
    o = MLA_Q_LORA + MLA_KV_LORA
    kr, krs, ab = sm[:, o:o + LANES], sm[:, o + LANES:o + 2 * LANES], sm[:, o + 2 * LANES:]
    ab_ref[...] = ab

    cosv = cos_ref[...]
    sinv = sin_ref[...]
    kpe =(kr * cosv + krs * sinv).astype(BF16)

    cqn = _rms(cq, qan_ref[...]).astype(BF16)
    q3 = _dot(cqn, wuq_ref[...])
    scale = MLA_QK_DIM ** -0.5 * LOG2_E
    for h in range(HEADS):
        lo = h * HEAD_DIM
        qn = q3[:, lo:lo + HEAD_DIM]
        qp = q3[:, hw + lo:hw + lo + HEAD_DIM] * cosv + q3[:, 2 * hw + lo:2 * hw + lo + HEAD_DIM] * sinv
        q_ref[:, 2 * lo:2 * lo + HEAD_DIM] = (qn * scale).astype(BF16)
        q_ref[:, 2 * lo + HEAD_DIM:2 * lo + 2 * HEAD_DIM] = (qp * scale).astype(BF16)

    ckvn = _rms(ckv, kvn_ref[...]).astype(BF16)
    kv = _dot(ckvn, wukv_ref[...])
    ones = jnp.ones((tm, HEAD_DIM), BF16)
    for h in range(HEADS):
        lo = h * HEAD_DIM
        k_ref[:, 2 * lo:2 * lo + HEAD_DIM] = kv[:, lo:lo + HEAD_DIM].astype(BF16)
        k_ref[:, 2 * lo + HEAD_DIM:2 * lo + 2 * HEAD_DIM] = kpe
        v_ref[:, 2 * lo:2 * lo + HEAD_DIM] = kv[:, hw + lo:hw + lo + HEAD_DIM].astype(BF16)
        v_ref[:, 2 * lo + HEAD_DIM:2 * lo + 2 * HEAD_DIM] = ones

    sl = 2 * HEAD_DIM
    for s in range(qkv_w // sl):
        buf_ref[CONV_TAIL:CONV_TAIL + tm, s * sl:(s + 1) * sl] = _dot(u, wbig_ref[:, s * sl:(s + 1) * sl])
        zg = _dot(u, wbig_ref[:, qkv_w + s * sl:qkv_w + (s + 1) * sl]).astype(BF16)
        if s * sl < d:
            z_ref[:, s * sl:(s + 1) * sl] = zg
        else:
            gate_ref[:, s * sl - d:(s + 1) * sl - d] = zg
        for grp in range(2 * s, 2 * s + 2):
            lo = grp * HEAD_DIM
            acc = None
            for j in range(CONV_WIDTH):
                r0 = CONV_TAIL - (CONV_WIDTH - 1) + j
                term = buf_ref[r0:r0 + tm, lo:lo + HEAD_DIM] * convw_ref[j:j + 1, lo:lo + HEAD_DIM]
                acc = term if acc is None else acc + term
            y = _silu(acc)
            if grp < 2 * HEADS:
                y = y * lax.rsqrt(jnp.sum(y * y, axis=-1, keepdims=True) + NORM_EPS)
            if grp < HEADS:
                y = y * (HEAD_DIM ** -0.5)
            qkv_ref[:, lo:lo + HEAD_DIM] = y.astype(BF16)


def _inproj(x2, cosv, sinv, nw, wbig, wsm, convw, qan, kvn, wuq, wukv, tm, seq):
    t, d = x2.shape
    qkv_w = 3 * HEADS * HEAD_DIM
    hw = HEADS * HEAD_DIM
    row = lambda w: pl.BlockSpec((tm, w), lambda i: (i, 0))
    outs = [(qkv_w, BF16), (d, BF16), (2 * d, BF16), (LANES, F32), (2 * hw, BF16), (2 * hw, BF16), (2 * hw, BF16)]
    return pl.pallas_call(
        functools.partial(_inproj_body, tiles_per_seq=seq // tm),
        grid=(t // tm,),
        in_specs=[row(d), row(LANES), row(LANES), _resident(nw.shape), _resident(wbig.shape), _resident(wsm.shape),
                  _resident(convw.shape), _resident(qan.shape), _resident(kvn.shape), _resident(wuq.shape),
                  _resident(wukv.shape)],
        out_specs=[row(w) for w, _ in outs],
        out_shape=[jax.ShapeDtypeStruct((t, w), dt) for w, dt in outs],
        scratch_shapes=[pltpu.VMEM((tm + CONV_TAIL, qkv_w), F32)],
        compiler_params=pltpu.CompilerParams(dimension_semantics=("arbitrary",), vmem_limit_bytes=VMEM_LIMIT),
        name="inproj",
    )(x2, cosv, sinv, nw, wbig, wsm, convw, qan, kvn, wuq, wukv)


def _gdn_body(qkv_ref, ab_ref, z_ref, gpar_ref, nw_ref, o_ref, state_ref, *, nc):
    c = CHUNK
    ts = nc * c
    hw = HEADS * HEAD_DIM

    @pl.when(pl.program_id(1) == 0)
    def _():
        state_ref[...] = jnp.zeros_like(state_ref)

    ab = ab_ref[...]
    xg = ab + gpar_ref[0:1, :]
    softplus = jnp.maximum(xg, 0.0) + jnp.log(1.0 + jnp.exp(-jnp.abs(xg)))
    g = -jnp.exp(gpar_ref[1:2, :]) * softplus
    beta_all = jax.nn.sigmoid(ab)

    rt = lax.broadcasted_iota(jnp.int32, (ts, ts), 0)
    ct = lax.broadcasted_iota(jnp.int32, (ts, ts), 1)
    same = (rt // c) == (ct // c)
    gc_col = jnp.dot(jnp.logical_and(same, rt >= ct).astype(F32), g, preferred_element_type=F32,
                     precision=lax.Precision.HIGHEST)
    gc_row = jnp.dot(g.T[0:HEADS, :], jnp.logical_and(same, rt <= ct).astype(F32), preferred_element_type=F32,
                     precision=lax.Precision.HIGHEST)

    ri = lax.broadcasted_iota(jnp.int32, (c, c), 0)
    ci = lax.broadcasted_iota(jnp.int32, (c, c), 1)
    incl = ri >= ci
    strict = ri > ci
    eye = (ri == ci).astype(F32)
    m16 = ((ri // 16) == (ci // 16)).astype(F32)

    its = [(ch, h) for ch in range(nc) for h in range(HEADS)]
    n = range(len(its))
    rows = lambda ch: slice(ch * c, (ch + 1) * c)
    q = [qkv_ref[rows(ch), h * HEAD_DIM:(h + 1) * HEAD_DIM] for ch, h in its]
    k = [qkv_ref[rows(ch), hw + h * HEAD_DIM:hw + (h + 1) * HEAD_DIM] for ch, h in its]
    v = [qkv_ref[rows(ch), 2 * hw + h * HEAD_DIM:2 * hw + (h + 1) * HEAD_DIM].astype(F32) for ch, h in its]
    kf = [t.astype(F32) for t in k]
    gcol = [gc_col[rows(ch), h:h + 1] for ch, h in its]
    beta = [beta_all[rows(ch), HEADS + h:HEADS + h + 1] for ch, h in its]
    decay = [jnp.exp(jnp.where(incl, gcol[i] - gc_row[h:h + 1, rows(ch)], -jnp.inf)) for i, (ch, h) in enumerate(its)]
    egc = [jnp.exp(t) for t in gcol]
    gend = [t[c - 1:c, :] for t in gcol]
    kb = [kf[i] * beta[i] for i in n]
    low = [jnp.where(strict, _dot_nt(kb[i].astype(BF16), k[i]) * decay[i], 0.0) for i in n]
    rhs = [jnp.concatenate([v[i] * beta[i], kb[i] * egc[i]], axis=1) for i in n]
    qk = [(_dot_nt(q[i], k[i]) * decay[i]).astype(BF16) for i in n]

    ld = [t * m16 for t in low]
    off = [low[i] - ld[i] for i in n]
    td = [eye - t for t in ld]
    p = [_dotb(t, t) for t in ld]
    td = [td[i] + _dotb(td[i], p[i]) for i in n]
    p = [_dotb(t, t) for t in p]
    td = [td[i] + _dotb(td[i], p[i]) for i in n]
    p = [_dotb(t, t) for t in p]
    td = [td[i] + _dotb(td[i], p[i]) for i in n]
    y = [_dotb(td[i], rhs[i]) for i in n]
    m = [_dotb(td[i], off[i]) for i in n]
    mm = [_dotb(t, t) for t in m]
    y = [y[i] - _dotb(m[i], y[i]) for i in n]
    sol = [y[i] + _dotb(mm[i], y[i]) for i in n]
    wq = [jnp.concatenate([sol[i][:, HEAD_DIM:], q[i].astype(F32) * egc[i]], axis=0).astype(BF16) for i in n]
    k_end = [(kf[i] * jnp.exp(gend[i] - gcol[i])).astype(BF16) for i in n]

    state = [state_ref[h] for h in range(HEADS)]
    for ch in range(nc):
        idx = [ch * HEADS + h for h in range(HEADS)]
        ws_qs = [_dot(wq[i], state[h].astype(BF16)) for h, i in enumerate(idx)]
        vnb = [(sol[i][:, :HEAD_DIM] - ws_qs[h][:c]).astype(BF16) for h, i in enumerate(idx)]
        o = [ws_qs[h][c:] + _dot(qk[i], vnb[h]) for h, i in enumerate(idx)]
        state = [state[h] * jnp.exp(gend[i]) + _dot_tn(k_end[i], vnb[h]) for h, i in enumerate(idx)]
        for h in range(HEADS):
            lo = h * HEAD_DIM
            out = _rms(o[h], nw_ref[...]) * _silu(z_ref[rows(ch), lo:lo + HEAD_DIM].astype(F32))
            o_ref[rows(ch), lo:lo + HEAD_DIM] = out.astype(o_ref.dtype)
    for h in range(HEADS):
        state_ref[h] = state[h]


def _gdn(qkv, ab, z, gpar, nw, batch, seq, nc):
    t, qkv_w = qkv.shape
    hw = HEADS * HEAD_DIM
    ts = nc * CHUNK
    assert ts % LANES == 0 and seq % ts == 0
    n = seq // ts
    row = lambda w: pl.BlockSpec((ts, w), lambda b, s: (b * n + s, 0))
    return pl.pallas_call(
        functools.partial(_gdn_body, nc=nc),
        grid=(batch, n),
        in_specs=[row(qkv_w), row(LANES), row(hw), _resident(gpar.shape), _resident(nw.shape)],
        out_specs=row(hw),
        out_shape=jax.ShapeDtypeStruct((t, hw), BF16),
        scratch_shapes=[pltpu.VMEM((HEADS, HEAD_DIM, HEAD_DIM), F32)],
        compiler_params=pltpu.CompilerParams(dimension_semantics=("arbitrary", "arbitrary"),
                                             vmem_limit_bytes=VMEM_LIMIT),
        name="gdn",
    )(qkv, ab, z, gpar, nw)


def _attn_body(q_ref, k_ref, v_ref, o_ref, *, th, nq):
    qi = pl.program_id(2)
    qs = [q_ref[c * th:(c + 1) * th, :] for c in range(nq)]

    def scores(qx, off):
        return _dot_nt(k_ref[pl.ds(off, th), :], qx)

    def update(st, s, off):
        m, acc = st
        m_new = jnp.maximum(m, jnp.max(s, axis=0, keepdims=True))
        p = jnp.exp2(s - m_new).astype(BF16)
        acc = jnp.exp2(m - m_new) * acc + _dot_tn(v_ref[pl.ds(off, th), :], p)
        return m_new, acc

    def body(j, sts):
        offs = [pl.multiple_of(j * (2 * th), 2 * th) + t * th for t in range(2)]
        s = [[scores(qx, off) for qx in qs] for off in offs]
        for t, off in enumerate(offs):
            sts = tuple(update(sts[c], s[t][c], off) for c in range(nq))
        return sts

    init = (jnp.full((1, th), -jnp.inf, F32), jnp.zeros((2 * HEAD_DIM, th), F32))
    sts = lax.fori_loop(0, qi * (nq // 2), body, (init,) * nq)

    base = pl.multiple_of(qi * (nq * th), nq * th)
    ck = lax.broadcasted_iota(jnp.int32, (th, th), 0) // CHUNK
    cq = lax.broadcasted_iota(jnp.int32, (th, th), 1) // CHUNK
    visible = ck <= cq
    sts = list(sts)
    for j in range(nq):
        off = base + j * th
        s = [scores(qs[c], off) for c in range(j, nq)]
        s[0] = jnp.where(visible, s[0], -jnp.inf)
        for c in range(j, nq):
            sts[c] = update(sts[c], s[c - j], off)
    for c in range(nq):
        acc = sts[c][1]
        o_ref[c * th:(c + 1) * th, :] = (acc[:HEAD_DIM, :] / acc[HEAD_DIM:HEAD_DIM + 1, :]).T.astype(o_ref.dtype)


def _attn(q, k, v, batch, seq, th, nq):
    hw = HEADS * HEAD_DIM
    tq = nq * th
    assert nq % 2 == 0 and seq % tq == 0
    wide = lambda a: a.reshape(batch, seq, 2 * hw)
    head_all = pl.BlockSpec((None, seq, 2 * HEAD_DIM), lambda b, h, i: (b, 0, h))
    out = pl.pallas_call(
        functools.partial(_attn_body, th=th, nq=nq),
        grid=(batch, HEADS, seq // tq),
        in_specs=[pl.BlockSpec((None, tq, 2 * HEAD_DIM), lambda b, h, i: (b, i, h)), head_all, head_all],
        out_specs=pl.BlockSpec((None, tq, HEAD_DIM), lambda b, h, i: (b, i, h)),
        out_shape=jax.ShapeDtypeStruct((batch, seq, hw), BF16),
        compiler_params=pltpu.CompilerParams(dimension_semantics=("arbitrary", "arbitrary", "arbitrary"),
                                             vmem_limit_bytes=VMEM_LIMIT),
        name="attn",
    )(wide(q), wide(k), wide(v))
    return out.reshape(batch * seq, hw)


def _mix_body(oa_ref, at_ref, g_ref, x_ref, wa_ref, wb_ref, wo_ref, nw_ref, o_ref):
    d = x_ref.shape[1]
    ya = _dot(oa_ref[...], wa_ref[...])
    yb = _dot(at_ref[...], wb_ref[...])
    ga = jax.nn.sigmoid(g_ref[:, :d].astype(F32))
    gb = jax.nn.sigmoid(g_ref[:, d:].astype(F32))
    h = ga * ya + gb * yb
    y = _dot(h.astype(BF16), wo_ref[...])
    o_ref[...] = x_ref[...] + _rms(y, nw_ref[...])


def _mix(oa, at, gate, x2, wa, wb, wo, nw, tm):
    t, d = x2.shape
    row = lambda w: pl.BlockSpec((tm, w), lambda i: (i, 0))
    return pl.pallas_call(
        _mix_body,
        grid=(t // tm,),
        in_specs=[row(oa.shape[1]), row(at.shape[1]), row(2 * d), row(d), _resident(wa.shape),
                  _resident(wb.shape), _resident(wo.shape), _resident(nw.shape)],
        out_specs=row(d),
        out_shape=jax.ShapeDtypeStruct((t, d), F32),
        compiler_params=pltpu.CompilerParams(dimension_semantics=("arbitrary",), vmem_limit_bytes=VMEM_LIMIT),
        name="mix",
    )(oa, at, gate, x2, wa, wb, wo, nw)


def _ffn_body(x_ref, n1_ref, w1_ref, w2_ref, n2_ref, o_ref, *, fc):
    x = x_ref[...]
    u = _rms(x, n1_ref[...]).astype(BF16)
    acc = None
    for c0 in range(0, w1_ref.shape[1], fc):
        hcol = jnp.maximum(_dot(u, w1_ref[:, c0:c0 + fc]), 0.0)
        part = _dot((hcol * hcol).astype(BF16), w2_ref[c0:c0 + fc, :])
        acc = part if acc is None else acc + part
    o_ref[...] = x + _rms(acc, n2_ref[...])


def _ffn(x2, n1, w1, w2, n2, tm):
    t, d = x2.shape
    row = pl.BlockSpec((tm, d), lambda i: (i, 0))
    return pl.pallas_call(
        functools.partial(_ffn_body, fc=min(1024, w1.shape[1])),
        grid=(t // tm,),
        in_specs=[row, _resident(n1.shape), _resident(w1.shape), _resident(w2.shape), _resident(n2.shape)],
        out_specs=row,
        out_shape=jax.ShapeDtypeStruct((t, d), F32),
        compiler_params=pltpu.CompilerParams(dimension_semantics=("arbitrary",), vmem_limit_bytes=VMEM_LIMIT),
        name="ffn",
    )(x2, n1, w1, w2, n2)


def _rope_table():
    inv_freq = ROPE_BASE ** (-jnp.arange(0, MLA_ROPE, 2, dtype=F32) / MLA_ROPE)
    half = MLA_ROPE // 2
    zeros = jnp.zeros((LANES - MLA_ROPE,), F32)
    rows = [jnp.concatenate([inv_freq, inv_freq, zeros]),
            jnp.concatenate([jnp.ones((MLA_ROPE,), F32), zeros]),
            jnp.concatenate([-jnp.ones((half,), F32), jnp.ones((half,), F32), zeros])]
    return jnp.concatenate([jnp.stack(rows), jnp.zeros((5, LANES), F32)], axis=0)


def _pad_cols(w, width):
    return jnp.pad(w, ((0, 0), (0, width - w.shape[1])))


def _layer_weights(w_in, w_uq, w_ukv):
    hw = HEADS * HEAD_DIM
    o = 3 * hw
    qkv, z = w_in[:, :o], w_in[:, o:o + hw]
    o += hw
    a_b = w_in[:, o:o + 2 * HEADS]
    o += 2 * HEADS
    cq = w_in[:, o:o + MLA_Q_LORA]
    o += MLA_Q_LORA
    ckv = w_in[:, o:o + MLA_KV_LORA]
    o += MLA_KV_LORA
    kr = w_in[:, o:o + MLA_ROPE]
    o += MLA_ROPE
    gates = w_in[:, o:]
    half = MLA_ROPE // 2
    krs = jnp.concatenate([kr[:, half:], kr[:, :half]], axis=1)
    wbig = jnp.concatenate([qkv, z, gates], axis=1).astype(BF16)
    wsm = jnp.concatenate([cq, ckv, _pad_cols(kr, LANES), _pad_cols(krs, LANES), _pad_cols(a_b, LANES)],
                          axis=1).astype(BF16)
    uq = w_uq.reshape(MLA_Q_LORA, HEADS, MLA_QK_DIM)
    nope, pe = uq[:, :, :HEAD_DIM], uq[:, :, HEAD_DIM:]
    pes = jnp.concatenate([pe[:, :, half:], pe[:, :, :half]], axis=2)
    padh = lambda w: jnp.pad(w, ((0, 0), (0, 0), (0, HEAD_DIM - MLA_ROPE))).reshape(MLA_Q_LORA, hw)
    wuq = jnp.concatenate([nope.reshape(MLA_Q_LORA, hw), padh(pe), padh(pes)], axis=1).astype(BF16)
    ukv = w_ukv.reshape(MLA_KV_LORA, HEADS, 2 * HEAD_DIM)
    wukv = jnp.concatenate([ukv[:, :, :HEAD_DIM].reshape(MLA_KV_LORA, hw),
                            ukv[:, :, HEAD_DIM:].reshape(MLA_KV_LORA, hw)], axis=1).astype(BF16)
    return wbig, wsm, wuq, wukv


def kernel(x, positions, w_in, conv_w, A_log, dt_bias, gdn_norm_w, q_a_norm_w, w_uq, kv_a_norm_w, w_ukv, w_branch_a, w_branch_b, w_out, pre_mix_norm_w, post_mix_norm_w, pre_ffn_norm_w, post_ffn_norm_w, w_ff1, w_ff2):
    batch, seq, d = x.shape
    t = batch * seq
    depth = w_in.shape[0]
    tm_proj = min(256, t)
    tm_mlp = min(512, t)
    th = min(512, seq // 2)
    nq = 4 if seq % (4 * th) == 0 else 2
    x2 = x.reshape(t, d)
    pos2 = positions.astype(F32).reshape(t, 1)
    cosv, sinv = _rope(pos2, _rope_table(), tm_mlp)
    row = lambda v: v.reshape(1, -1)
    for l in range(depth):
        wbig, wsm, wuq, wukv = _layer_weights(w_in[l], w_uq[l], w_ukv[l])
        convw = jnp.pad(conv_w[l], ((0, 8 - CONV_WIDTH), (0, 0)))
        qkv, z, gate, ab, q, kc, va = _inproj(
            x2, cosv, sinv, row(pre_mix_norm_w[l]), wbig, wsm, convw, row(q_a_norm_w[l]), row(kv_a_norm_w[l]),
            wuq, wukv, tm_proj, seq)
        gpar = jnp.zeros((8, LANES), F32).at[0, :HEADS].set(dt_bias[l]).at[1, :HEADS].set(A_log[l])
        oa = _gdn(qkv, ab, z, gpar, row(gdn_norm_w[l]), batch, seq, GDN_CHUNKS_PER_STEP)
        at = _attn(q, kc, va, batch, seq, th, nq)
        x2 = _mix(oa, at, gate, x2, w_branch_a[l].astype(BF16), w_branch_b[l].astype(BF16),
                  w_out[l].astype(BF16), row(post_mix_norm_w[l]), tm_mlp)
        x2 = _ffn(x2, row(pre_ffn_norm_w[l]), w_ff1[l].astype(BF16), w_ff2[l].astype(BF16),
                  row(post_ffn_norm_w[l]), tm_mlp)
    return x2.reshape(batch, seq, d)
```

```python
import functools

import jax
import jax.numpy as jnp
from jax import lax
from jax.experimental import pallas as pl
from jax.experimental.pallas import tpu as pltpu

F32 = jnp.float32
BF16 = jnp.bfloat16

LANES = 128
CHUNK = 64
NORM_EPS = 1e-6
HEADS = 8
HEAD_DIM = 128
CONV_WIDTH = 4
CONV_TAIL = 8
MLA_Q_LORA = 384
MLA_KV_LORA = 256
MLA_ROPE = 64
MLA_QK_DIM = HEAD_DIM + MLA_ROPE
ROPE_BASE = 10000.0
LOG2_E = 1.4426950408889634
GDN_CHUNKS_PER_STEP = 4
VMEM_LIMIT = 56 * 1024 * 1024


def _rms(x, w):
    return x * lax.rsqrt(jnp.mean(x * x, axis=-1, keepdims=True) + NORM_EPS) * w


def _silu(x):
    return x * jax.nn.sigmoid(x)


def _dot(a, b):
    return jnp.dot(a, b, preferred_element_type=F32)


def _dot_nt(a, b):
    return lax.dot_general(a, b, (((1,), (1,)), ((), ())), preferred_element_type=F32)


def _dot_tn(a, b):
    return lax.dot_general(a, b, (((0,), (0,)), ((), ())), preferred_element_type=F32)


def _dotb(a, b):
    return _dot(a.astype(BF16), b.astype(BF16))


def _resident(shape):
    nd = len(shape)
    return pl.BlockSpec(shape, lambda *_: (0,) * nd, pipeline_mode=pl.Buffered(1))


def _rope_body(pos_ref, rope_ref, cos_ref, sin_ref):
    ang = pos_ref[...] * rope_ref[0:1, :]
    cos_ref[...] = jnp.cos(ang) * rope_ref[1:2, :]
    sin_ref[...] = jnp.sin(ang) * rope_ref[2:3, :]


def _rope(pos2, rope_tab, tm):
    t = pos2.shape[0]
    out = pl.BlockSpec((tm, LANES), lambda i: (i, 0))
    return pl.pallas_call(
        _rope_body,
        grid=(t // tm,),
        in_specs=[pl.BlockSpec((tm, 1), lambda i: (i, 0)), _resident(rope_tab.shape)],
        out_specs=[out, out],
        out_shape=[jax.ShapeDtypeStruct((t, LANES), F32)] * 2,
        compiler_params=pltpu.CompilerParams(dimension_semantics=("arbitrary",), vmem_limit_bytes=VMEM_LIMIT),
        name="rope",
    )(pos2, rope_tab)


def _inproj_body(x_ref, cos_ref, sin_ref, nw_ref, wbig_ref, wsm_ref, convw_ref, qan_ref, kvn_ref, wuq_ref, wukv_ref,
                 qkv_ref, z_ref, gate_ref, ab_ref, q_ref, k_ref, v_ref, buf_ref, *, tiles_per_seq):
    tm, d = x_ref.shape
    hw = HEADS * HEAD_DIM
    qkv_w = qkv_ref.shape[1]
    u = _rms(x_ref[...], nw_ref[...]).astype(BF16)

    @pl.when(pl.program_id(0) % tiles_per_seq == 0)
    def _():
        buf_ref[tm:tm + CONV_TAIL, :] = jnp.zeros((CONV_TAIL, qkv_w), F32)

    buf_ref[0:CONV_TAIL, :] = buf_ref[tm:tm + CONV_TAIL, :]

    sm = _dot(u, wsm_ref[...])
    cq = sm[:, :MLA_Q_LORA]
    ckv = sm[:, MLA_Q_LORA:MLA_Q_LORA + MLA_KV_LORA]
    o = MLA_Q_LORA + MLA_KV_LORA
    kr, krs, ab = sm[:, o:o + LANES], sm[:, o + LANES:o + 2 * LANES], sm[:, o + 2 * LANES:]
    ab_ref[...] = ab

    cosv = cos_ref[...]
    sinv = sin_ref[...]
    kpe =(kr * cosv + krs * sinv).astype(BF16)

    cqn = _rms(cq, qan_ref[...]).astype(BF16)
    q3 = _dot(cqn, wuq_ref[...])
    scale = MLA_QK_DIM ** -0.5 * LOG2_E
    for h in range(HEADS):
        lo = h * HEAD_DIM
        qn = q3[:, lo:lo + HEAD_DIM]
        qp = q3[:, hw + lo:hw + lo + HEAD_DIM] * cosv + q3[:, 2 * hw + lo:2 * hw + lo + HEAD_DIM] * sinv
        q_ref[:, 2 * lo:2 * lo + HEAD_DIM] = (qn * scale).astype(BF16)
        q_ref[:, 2 * lo + HEAD_DIM:2 * lo + 2 * HEAD_DIM] = (qp * scale).astype(BF16)

    ckvn = _rms(ckv, kvn_ref[...]).astype(BF16)
    kv = _dot(ckvn, wukv_ref[...])
    for h in range(HEADS):
        lo = h * HEAD_DIM
        k_ref[:, 2 * lo:2 * lo + HEAD_DIM] = kv[:, lo:lo + HEAD_DIM].astype(BF16)
        k_ref[:, 2 * lo + HEAD_DIM:2 * lo + 2 * HEAD_DIM] = kpe
    v_ref[...] = kv[:, hw:].astype(BF16)

    sl = 2 * HEAD_DIM
    for s in range(qkv_w // sl):
        buf_ref[CONV_TAIL:CONV_TAIL + tm, s * sl:(s + 1) * sl] = _dot(u, wbig_ref[:, s * sl:(s + 1) * sl])
        zg = _dot(u, wbig_ref[:, qkv_w + s * sl:qkv_w + (s + 1) * sl]).astype(BF16)
        if s * sl < d:
            z_ref[:, s * sl:(s + 1) * sl] = zg
        else:
            gate_ref[:, s * sl - d:(s + 1) * sl - d] = zg
        for grp in range(2 * s, 2 * s + 2):
            lo = grp * HEAD_DIM
            acc = None
            for j in range(CONV_WIDTH):
                r0 = CONV_TAIL - (CONV_WIDTH - 1) + j
                term = buf_ref[r0:r0 + tm, lo:lo + HEAD_DIM] * convw_ref[j:j + 1, lo:lo + HEAD_DIM]
                acc = term if acc is None else acc + term
            y = _silu(acc)
            if grp < 2 * HEADS:
                y = y * lax.rsqrt(jnp.sum(y * y, axis=-1, keepdims=True) + NORM_EPS)
            if grp < HEADS:
                y = y * (HEAD_DIM ** -0.5)
            qkv_ref[:, lo:lo + HEAD_DIM] = y.astype(BF16)


def _inproj(x2, cosv, sinv, nw, wbig, wsm, convw, qan, kvn, wuq, wukv, tm, seq):
    t, d = x2.shape
    qkv_w = 3 * HEADS * HEAD_DIM
    hw = HEADS * HEAD_DIM
    row = lambda w: pl.BlockSpec((tm, w), lambda i: (i, 0))
    outs = [(qkv_w, BF16), (d, BF16), (2 * d, BF16), (LANES, F32), (2 * hw, BF16), (2 * hw, BF16), (hw, BF16)]
    return pl.pallas_call(
        functools.partial(_inproj_body, tiles_per_seq=seq // tm),
        grid=(t // tm,),
        in_specs=[row(d), row(LANES), row(LANES), _resident(nw.shape), _resident(wbig.shape), _resident(wsm.shape),
                  _resident(convw.shape), _resident(qan.shape), _resident(kvn.shape), _resident(wuq.shape),
                  _resident(wukv.shape)],
        out_specs=[row(w) for w, _ in outs],
        out_shape=[jax.ShapeDtypeStruct((t, w), dt) for w, dt in outs],
        scratch_shapes=[pltpu.VMEM((tm + CONV_TAIL, qkv_w), F32)],
        compiler_params=pltpu.CompilerParams(dimension_semantics=("arbitrary",), vmem_limit_bytes=VMEM_LIMIT),
        name="inproj",
    )(x2, cosv, sinv, nw, wbig, wsm, convw, qan, kvn, wuq, wukv)


def _gdn_body(qkv_ref, ab_ref, z_ref, gpar_ref, nw_ref, o_ref, state_ref, *, nc):
    c = CHUNK
    ts = nc * c
    hw = HEADS * HEAD_DIM

    @pl.when(pl.program_id(1) == 0)
    def _():
        state_ref[...] = jnp.zeros_like(state_ref)

    ab = ab_ref[...]
    xg = ab + gpar_ref[0:1, :]
    softplus = jnp.maximum(xg, 0.0) + jnp.log(1.0 + jnp.exp(-jnp.abs(xg)))
    g = -jnp.exp(gpar_ref[1:2, :]) * softplus
    beta_all = jax.nn.sigmoid(ab)

    rt = lax.broadcasted_iota(jnp.int32, (ts, ts), 0)
    ct = lax.broadcasted_iota(jnp.int32, (ts, ts), 1)
    same = (rt // c) == (ct // c)
    gc_col = jnp.dot(jnp.logical_and(same, rt >= ct).astype(F32), g, preferred_element_type=F32,
                     precision=lax.Precision.HIGHEST)
    gc_row = jnp.dot(g.T[0:HEADS, :], jnp.logical_and(same, rt <= ct).astype(F32), preferred_element_type=F32,
                     precision=lax.Precision.HIGHEST)

    ri = lax.broadcasted_iota(jnp.int32, (c, c), 0)
    ci = lax.broadcasted_iota(jnp.int32, (c, c), 1)
    incl = ri >= ci
    strict = ri > ci
    eye = (ri == ci).astype(F32)
    m16 = ((ri // 16) == (ci // 16)).astype(F32)

    its = [(ch, h) for ch in range(nc) for h in range(HEADS)]
    n = range(len(its))
    rows = lambda ch: slice(ch * c, (ch + 1) * c)
    q = [qkv_ref[rows(ch), h * HEAD_DIM:(h + 1) * HEAD_DIM] for ch, h in its]
    k = [qkv_ref[rows(ch), hw + h * HEAD_DIM:hw + (h + 1) * HEAD_DIM] for ch, h in its]
    v = [qkv_ref[rows(ch), 2 * hw + h * HEAD_DIM:2 * hw + (h + 1) * HEAD_DIM].astype(F32) for ch, h in its]
    kf = [t.astype(F32) for t in k]
    gcol = [gc_col[rows(ch), h:h + 1] for ch, h in its]
    beta = [beta_all[rows(ch), HEADS + h:HEADS + h + 1] for ch, h in its]
    decay = [jnp.exp(jnp.where(incl, gcol[i] - gc_row[h:h + 1, rows(ch)], -jnp.inf)) for i, (ch, h) in enumerate(its)]
    egc = [jnp.exp(t) for t in gcol]
    gend = [t[c - 1:c, :] for t in gcol]
    kb = [kf[i] * beta[i] for i in n]
    low = [jnp.where(strict, _dot_nt(kb[i].astype(BF16), k[i]) * decay[i], 0.0) for i in n]
    rhs = [jnp.concatenate([v[i] * beta[i], kb[i] * egc[i]], axis=1) for i in n]
    qk = [(_dot_nt(q[i], k[i]) * decay[i]).astype(BF16) for i in n]

    ld = [t * m16 for t in low]
    off = [low[i] - ld[i] for i in n]
    td = [eye - t for t in ld]
    p = [_dotb(t, t) for t in ld]
    td = [td[i] + _dotb(td[i], p[i]) for i in n]
    p = [_dotb(t, t) for t in p]
    td = [td[i] + _dotb(td[i], p[i]) for i in n]
    p = [_dotb(t, t) for t in p]
    td = [td[i] + _dotb(td[i], p[i]) for i in n]
    y = [_dotb(td[i], rhs[i]) for i in n]
    m = [_dotb(td[i], off[i]) for i in n]
    mm = [_dotb(t, t) for t in m]
    y = [y[i] - _dotb(m[i], y[i]) for i in n]
    sol = [y[i] + _dotb(mm[i], y[i]) for i in n]
    wq = [jnp.concatenate([sol[i][:, HEAD_DIM:], q[i].astype(F32) * egc[i]], axis=0).astype(BF16) for i in n]
    k_end = [(kf[i] * jnp.exp(gend[i] - gcol[i])).astype(BF16) for i in n]

    state = [state_ref[h] for h in range(HEADS)]
    for ch in range(nc):
        idx = [ch * HEADS + h for h in range(HEADS)]
        ws_qs = [_dot(wq[i], state[h].astype(BF16)) for h, i in enumerate(idx)]
        vnb = [(sol[i][:, :HEAD_DIM] - ws_qs[h][:c]).astype(BF16) for h, i in enumerate(idx)]
        o = [ws_qs[h][c:] + _dot(qk[i], vnb[h]) for h, i in enumerate(idx)]
        state = [state[h] * jnp.exp(gend[i]) + _dot_tn(k_end[i], vnb[h]) for h, i in enumerate(idx)]
        for h in range(HEADS):
            lo = h * HEAD_DIM
            out = _rms(o[h], nw_ref[...]) * _silu(z_ref[rows(ch), lo:lo + HEAD_DIM].astype(F32))
            o_ref[rows(ch), lo:lo + HEAD_DIM] = out.astype(o_ref.dtype)
    for h in range(HEADS):
        state_ref[h] = state[h]


def _gdn(qkv, ab, z, gpar, nw, batch, seq, nc):
    t, qkv_w = qkv.shape
    hw = HEADS * HEAD_DIM
    ts = nc * CHUNK
    assert ts % LANES == 0 and seq % ts == 0
    n = seq // ts
    row = lambda w: pl.BlockSpec((ts, w), lambda b, s: (b * n + s, 0))
    return pl.pallas_call(
        functools.partial(_gdn_body, nc=nc),
        grid=(batch, n),
        in_specs=[row(qkv_w), row(LANES), row(hw), _resident(gpar.shape), _resident(nw.shape)],
        out_specs=row(hw),
        out_shape=jax.ShapeDtypeStruct((t, hw), BF16),
        scratch_shapes=[pltpu.VMEM((HEADS, HEAD_DIM, HEAD_DIM), F32)],
        compiler_params=pltpu.CompilerParams(dimension_semantics=("arbitrary", "arbitrary"),
                                             vmem_limit_bytes=VMEM_LIMIT),
        name="gdn",
    )(qkv, ab, z, gpar, nw)


def _attn_body(q_ref, k_ref, v_ref, o_ref, *, th, nq):
    qi = pl.program_id(2)
    qs = [q_ref[c * th:(c + 1) * th, :] for c in range(nq)]

    def scores(qx, off):
        return _dot_nt(k_ref[pl.ds(off, th), :], qx)

    def update(st, s, off):
        m, l, acc = st
        m_new = jnp.maximum(m, jnp.max(s, axis=0, keepdims=True))
        alpha = jnp.exp2(m - m_new)
        p = jnp.exp2(s - m_new)
        l = alpha * l + jnp.sum(p, axis=0, keepdims=True)
        acc = alpha * acc + _dot_tn(v_ref[pl.ds(off, th), :], p.astype(BF16))
        return m_new, l, acc

    def body(j, sts):
        offs = [pl.multiple_of(j * (2 * th), 2 * th) + t * th for t in range(2)]
        s = [[scores(qx, off) for qx in qs] for off in offs]
        for t, off in enumerate(offs):
            sts = tuple(update(sts[c], s[t][c], off) for c in range(nq))
        return sts

    init = (jnp.full((1, th), -jnp.inf, F32), jnp.zeros((1, th), F32), jnp.zeros((HEAD_DIM, th), F32))
    sts = lax.fori_loop(0, qi * (nq // 2), body, (init,) * nq)

    base = pl.multiple_of(qi * (nq * th), nq * th)
    ck = lax.broadcasted_iota(jnp.int32, (th, th), 0) // CHUNK
    cq = lax.broadcasted_iota(jnp.int32, (th, th), 1) // CHUNK
    visible = ck <= cq
    sts = list(sts)
    for j in range(nq):
        off = base + j * th
        s = [scores(qs[c], off) for c in range(j, nq)]
        s[0] = jnp.where(visible, s[0], -jnp.inf)
        for c in range(j, nq):
            sts[c] = update(sts[c], s[c - j], off)
    for c in range(nq):
        _, l, acc = sts[c]
        o_ref[c * th:(c + 1) * th, :] = (acc / l).T.astype(o_ref.dtype)


def _attn(q, k, v, batch, seq, th, nq):
    hw = HEADS * HEAD_DIM
    tq = nq * th
    assert nq % 2 == 0 and seq % tq == 0
    wide = lambda a: a.reshape(batch, seq, -1)
    head_all = lambda w: pl.BlockSpec((None, seq, w), lambda b, h, i: (b, 0, h))
    out = pl.pallas_call(
        functools.partial(_attn_body, th=th, nq=nq),
        grid=(batch, HEADS, seq // tq),
        in_specs=[pl.BlockSpec((None, tq, 2 * HEAD_DIM), lambda b, h, i: (b, i, h)), head_all(2 * HEAD_DIM),
                  head_all(HEAD_DIM)],
        out_specs=pl.BlockSpec((None, tq, HEAD_DIM), lambda b, h, i: (b, i, h)),
        out_shape=jax.ShapeDtypeStruct((batch, seq, hw), BF16),
        compiler_params=pltpu.CompilerParams(dimension_semantics=("arbitrary", "arbitrary", "arbitrary"),
                                             vmem_limit_bytes=VMEM_LIMIT),
        name="attn",
    )(wide(q), wide(k), wide(v))
    return out.reshape(batch * seq, hw)


def _mix_body(oa_ref, at_ref, g_ref, x_ref, wa_ref, wb_ref, wo_ref, nw_ref, o_ref):
    d = x_ref.shape[1]
    ya = _dot(oa_ref[...], wa_ref[...])
    yb = _dot(at_ref[...], wb_ref[...])
    ga = jax.nn.sigmoid(g_ref[:, :d].astype(F32))
    gb = jax.nn.sigmoid(g_ref[:, d:].astype(F32))
    h = ga * ya + gb * yb
    y = _dot(h.astype(BF16), wo_ref[...])
    o_ref[...] = x_ref[...] + _rms(y, nw_ref[...])


def _mix(oa, at, gate, x2, wa, wb, wo, nw, tm):
    t, d = x2.shape
    row = lambda w: pl.BlockSpec((tm, w), lambda i: (i, 0))
    return pl.pallas_call(
        _mix_body,
        grid=(t // tm,),
        in_specs=[row(oa.shape[1]), row(at.shape[1]), row(2 * d), row(d), _resident(wa.shape),
                  _resident(wb.shape), _resident(wo.shape), _resident(nw.shape)],
        out_specs=row(d),
        out_shape=jax.ShapeDtypeStruct((t, d), F32),
        compiler_params=pltpu.CompilerParams(dimension_semantics=("arbitrary",), vmem_limit_bytes=VMEM_LIMIT),
        name="mix",
    )(oa, at, gate, x2, wa, wb, wo, nw)


def _ffn_body(x_ref, n1_ref, w1_ref, w2_ref, n2_ref, o_ref, *, fc):
    x = x_ref[...]
    u = _rms(x, n1_ref[...]).astype(BF16)
    acc = None
    for c0 in range(0, w1_ref.shape[1], fc):
        hcol = jnp.maximum(_dot(u, w1_ref[:, c0:c0 + fc]), 0.0)
        part = _dot((hcol * hcol).astype(BF16), w2_ref[c0:c0 + fc, :])
        acc = part if acc is None else acc + part
    o_ref[...] = x + _rms(acc, n2_ref[...])


def _ffn(x2, n1, w1, w2, n2, tm):
    t, d = x2.shape
    row = pl.BlockSpec((tm, d), lambda i: (i, 0))
    return pl.pallas_call(
        functools.partial(_ffn_body, fc=min(1024, w1.shape[1])),
        grid=(t // tm,),
        in_specs=[row, _resident(n1.shape), _resident(w1.shape), _resident(w2.shape), _resident(n2.shape)],
        out_specs=row,
        out_shape=jax.ShapeDtypeStruct((t, d), F32),
        compiler_params=pltpu.CompilerParams(dimension_semantics=("arbitrary",), vmem_limit_bytes=VMEM_LIMIT),
        name="ffn",
    )(x2, n1, w1, w2, n2)


def _rope_table():
    inv_freq = ROPE_BASE ** (-jnp.arange(0, MLA_ROPE, 2, dtype=F32) / MLA_ROPE)
    half = MLA_ROPE // 2
    zeros = jnp.zeros((LANES - MLA_ROPE,), F32)
    rows = [jnp.concatenate([inv_freq, inv_freq, zeros]),
            jnp.concatenate([jnp.ones((MLA_ROPE,), F32), zeros]),
            jnp.concatenate([-jnp.ones((half,), F32), jnp.ones((half,), F32), zeros])]
    return jnp.concatenate([jnp.stack(rows), jnp.zeros((5, LANES), F32)], axis=0)


def _pad_cols(w, width):
    return jnp.pad(w, ((0, 0), (0, width - w.shape[1])))


def _layer_weights(w_in, w_uq, w_ukv):
    hw = HEADS * HEAD_DIM
    o = 3 * hw
    qkv, z = w_in[:, :o], w_in[:, o:o + hw]
    o += hw
    a_b = w_in[:, o:o + 2 * HEADS]
    o += 2 * HEADS
    cq = w_in[:, o:o + MLA_Q_LORA]
    o += MLA_Q_LORA
    ckv = w_in[:, o:o + MLA_KV_LORA]
    o += MLA_KV_LORA
    kr = w_in[:, o:o + MLA_ROPE]
    o += MLA_ROPE
    gates = w_in[:, o:]
    half = MLA_ROPE // 2
    krs = jnp.concatenate([kr[:, half:], kr[:, :half]], axis=1)
    wbig = jnp.concatenate([qkv, z, gates], axis=1).astype(BF16)
    wsm = jnp.concatenate([cq, ckv, _pad_cols(kr, LANES), _pad_cols(krs, LANES), _pad_cols(a_b, LANES)],
                          axis=1).astype(BF16)
    uq = w_uq.reshape(MLA_Q_LORA, HEADS, MLA_QK_DIM)
    nope, pe = uq[:, :, :HEAD_DIM], uq[:, :, HEAD_DIM:]
    pes = jnp.concatenate([pe[:, :, half:], pe[:, :, :half]], axis=2)
    padh = lambda w: jnp.pad(w, ((0, 0), (0, 0), (0, HEAD_DIM - MLA_ROPE))).reshape(MLA_Q_LORA, hw)
    wuq = jnp.concatenate([nope.reshape(MLA_Q_LORA, hw), padh(pe), padh(pes)], axis=1).astype(BF16)
    ukv = w_ukv.reshape(MLA_KV_LORA, HEADS, 2 * HEAD_DIM)
    wukv = jnp.concatenate([ukv[:, :, :HEAD_DIM].reshape(MLA_KV_LORA, hw),
                            ukv[:, :, HEAD_DIM:].reshape(MLA_KV_LORA, hw)], axis=1).astype(BF16)
    return wbig, wsm, wuq, wukv


def kernel(x, positions, w_in, conv_w, A_log, dt_bias, gdn_norm_w, q_a_norm_w, w_uq, kv_a_norm_w, w_ukv, w_branch_a, w_branch_b, w_out, pre_mix_norm_w, post_mix_norm_w, pre_ffn_norm_w, post_ffn_norm_w, w_ff1, w_ff2):
    batch, seq, d = x.shape
    t = batch * seq
    depth = w_in.shape[0]
    tm_proj = min(256, t)
    tm_mlp = min(512, t)
    th = min(512, seq // 2)
    nq = 4 if seq % (4 * th) == 0 else 2
    x2 = x.reshape(t, d)
    pos2 = positions.astype(F32).reshape(t, 1)
    cosv, sinv = _rope(pos2, _rope_table(), tm_mlp)
    row = lambda v: v.reshape(1, -1)
    for l in range(depth):
        wbig, wsm, wuq, wukv = _layer_weights(w_in[l], w_uq[l], w_ukv[l])
        convw = jnp.pad(conv_w[l], ((0, 8 - CONV_WIDTH), (0, 0)))
        qkv, z, gate, ab, q, kc, va = _inproj(
            x2, cosv, sinv, row(pre_mix_norm_w[l]), wbig, wsm, convw, row(q_a_norm_w[l]), row(kv_a_norm_w[l]),
            wuq, wukv, tm_proj, seq)
        gpar = jnp.zeros((8, LANES), F32).at[0, :HEADS].set(dt_bias[l]).at[1, :HEADS].set(A_log[l])
        oa = _gdn(qkv, ab, z, gpar, row(gdn_norm_w[l]), batch, seq, GDN_CHUNKS_PER_STEP)
        at = _attn(q, kc, va, batch, seq, th, nq)
        x2 = _mix(oa, at, gate, x2, w_branch_a[l].astype(BF16), w_branch_b[l].astype(BF16),
                  w_out[l].astype(BF16), row(post_mix_norm_w[l]), tm_mlp)
        x2 = _ffn(x2, row(pre_ffn_norm_w[l]), w_ff1[l].astype(BF16), w_ff2[l].astype(BF16),
                  row(post_ffn_norm_w[l]), tm_mlp)
    return x2.reshape(batch, seq, d)
```

```python
import functools

import jax
import jax.numpy as jnp
from jax import lax
from jax.experimental import pallas as pl
from jax.experimental.pallas import tpu as pltpu

F32 = jnp.float32
BF16 = jnp.bfloat16

LANES = 128
CHUNK = 64
NORM_EPS = 1e-6
HEADS = 8
HEAD_DIM = 128
CONV_WIDTH = 4
CONV_TAIL = 8
MLA_Q_LORA = 384
MLA_KV_LORA = 256
MLA_ROPE = 64
MLA_QK_DIM = HEAD_DIM + MLA_ROPE
ROPE_BASE = 10000.0
LOG2_E = 1.4426950408889634
GDN_CHUNKS_PER_STEP = 4
VMEM_LIMIT = 56 * 1024 * 1024


def _rms(x, w):
    return x * lax.rsqrt(jnp.mean(x * x, axis=-1, keepdims=True) + NORM_EPS) * w


def _silu(x):
    return x * jax.nn.sigmoid(x)


def _dot(a, b):
    return jnp.dot(a, b, preferred_element_type=F32)


def _dot_nt(a, b):
    return lax.dot_general(a, b, (((1,), (1,)), ((), ())), preferred_element_type=F32)


def _dot_tn(a, b):
    return lax.dot_general(a, b, (((0,), (0,)), ((), ())), preferred_element_type=F32)


def _dotb(a, b):
    return _dot(a.astype(BF16), b.astype(BF16))


def _resident(shape):
    nd = len(shape)
    return pl.BlockSpec(shape, lambda *_: (0,) * nd, pipeline_mode=pl.Buffered(1))


def _rope_body(pos_ref, rope_ref, cos_ref, sin_ref):
    ang = pos_ref[...] * rope_ref[0:1, :]
    cos_ref[...] = jnp.cos(ang) * rope_ref[1:2, :]
    sin_ref[...] = jnp.sin(ang) * rope_ref[2:3, :]


def _rope(pos2, rope_tab, tm):
    t = pos2.shape[0]
    out = pl.BlockSpec((tm, LANES), lambda i: (i, 0))
    return pl.pallas_call(
        _rope_body,
        grid=(t // tm,),
        in_specs=[pl.BlockSpec((tm, 1), lambda i: (i, 0)), _resident(rope_tab.shape)],
        out_specs=[out, out],
        out_shape=[jax.ShapeDtypeStruct((t, LANES), F32)] * 2,
        compiler_params=pltpu.CompilerParams(dimension_semantics=("arbitrary",), vmem_limit_bytes=VMEM_LIMIT),
        name="rope",
    )(pos2, rope_tab)


def _inproj_body(x_ref, cos_ref, sin_ref, nw_ref, wbig_ref, wsm_ref, convw_ref, qan_ref, kvn_ref, wuq_ref, wukv_ref,
                 qkv_ref, z_ref, gate_ref, ab_ref, q_ref, k_ref, v_ref, buf_ref, *, tiles_per_seq):
    tm, d = x_ref.shape
    hw = HEADS * HEAD_DIM
    qkv_w = qkv_ref.shape[1]
    u = _rms(x_ref[...], nw_ref[...]).astype(BF16)

    @pl.when(pl.program_id(0) % tiles_per_seq == 0)
    def _():
        buf_ref[tm:tm + CONV_TAIL, :] = jnp.zeros((CONV_TAIL, qkv_w), F32)

    buf_ref[0:CONV_TAIL, :] = buf_ref[tm:tm + CONV_TAIL, :]

    sm = _dot(u, wsm_ref[...])
    cq = sm[:, :MLA_Q_LORA]
    ckv = sm[:, MLA_Q_LORA:MLA_Q_LORA + MLA_KV_LORA]
    o = MLA_Q_LORA + MLA_KV_LORA
    kr, krs, ab = sm[:, o:o + LANES], sm[:, o + LANES:o + 2 * LANES], sm[:, o + 2 * LANES:]
    ab_ref[...] = ab

    cosv = cos_ref[...]
    sinv = sin_ref[...]
    kpe =(kr * cosv + krs * sinv).astype(BF16)

    cqn = _rms(cq, qan_ref[...]).astype(BF16)
    q3 = _dot(cqn, wuq_ref[...])
    scale = MLA_QK_DIM ** -0.5 * LOG2_E
    for h in range(HEADS):
        lo = h * HEAD_DIM
        qn = q3[:, lo:lo + HEAD_DIM]
        qp = q3[:, hw + lo:hw + lo + HEAD_DIM] * cosv + q3[:, 2 * hw + lo:2 * hw + lo + HEAD_DIM] * sinv
        q_ref[:, 2 * lo:2 * lo + HEAD_DIM] = (qn * scale).astype(BF16)
        q_ref[:, 2 * lo + HEAD_DIM:2 * lo + 2 * HEAD_DIM] = (qp * scale).astype(BF16)

    ckvn = _rms(ckv, kvn_ref[...]).astype(BF16)
    kv = _dot(ckvn, wukv_ref[...])
    for h in range(HEADS):
        lo = h * HEAD_DIM
        k_ref[:, 2 * lo:2 * lo + HEAD_DIM] = kv[:, lo:lo + HEAD_DIM].astype(BF16)
        k_ref[:, 2 * lo + HEAD_DIM:2 * lo + 2 * HEAD_DIM] = kpe
    v_ref[...] = kv[:, hw:].astype(BF16)

    sl = 2 * HEAD_DIM
    for s in range(qkv_w // sl):
        buf_ref[CONV_TAIL:CONV_TAIL + tm, s * sl:(s + 1) * sl] = _dot(u, wbig_ref[:, s * sl:(s + 1) * sl])
        zg = _dot(u, wbig_ref[:, qkv_w + s * sl:qkv_w + (s + 1) * sl]).astype(BF16)
        if s * sl < d:
            z_ref[:, s * sl:(s + 1) * sl] = zg
        else:
            gate_ref[:, s * sl - d:(s + 1) * sl - d] = zg
        for grp in range(2 * s, 2 * s + 2):
            lo = grp * HEAD_DIM
            acc = None
            for j in range(CONV_WIDTH):
                r0 = CONV_TAIL - (CONV_WIDTH - 1) + j
                term = buf_ref[r0:r0 + tm, lo:lo + HEAD_DIM] * convw_ref[j:j + 1, lo:lo + HEAD_DIM]
                acc = term if acc is None else acc + term
            y = _silu(acc)
            if grp < 2 * HEADS:
                y = y * lax.rsqrt(jnp.sum(y * y, axis=-1, keepdims=True) + NORM_EPS)
            if grp < HEADS:
                y = y * (HEAD_DIM ** -0.5)
            qkv_ref[:, lo:lo + HEAD_DIM] = y.astype(BF16)


def _inproj(x2, cosv, sinv, nw, wbig, wsm, convw, qan, kvn, wuq, wukv, tm, seq):
    t, d = x2.shape
    qkv_w = 3 * HEADS * HEAD_DIM
    hw = HEADS * HEAD_DIM
    row = lambda w: pl.BlockSpec((tm, w), lambda i: (i, 0))
    outs = [(qkv_w, BF16), (d, BF16), (2 * d, BF16), (LANES, F32), (2 * hw, BF16), (2 * hw, BF16), (hw, BF16)]
    return pl.pallas_call(
        functools.partial(_inproj_body, tiles_per_seq=seq // tm),
        grid=(t // tm,),
        in_specs=[row(d), row(LANES), row(LANES), _resident(nw.shape), _resident(wbig.shape), _resident(wsm.shape),
                  _resident(convw.shape), _resident(qan.shape), _resident(kvn.shape), _resident(wuq.shape),
                  _resident(wukv.shape)],
        out_specs=[row(w) for w, _ in outs],
        out_shape=[jax.ShapeDtypeStruct((t, w), dt) for w, dt in outs],
        scratch_shapes=[pltpu.VMEM((tm + CONV_TAIL, qkv_w), F32)],
        compiler_params=pltpu.CompilerParams(dimension_semantics=("arbitrary",), vmem_limit_bytes=VMEM_LIMIT),
        name="inproj",
    )(x2, cosv, sinv, nw, wbig, wsm, convw, qan, kvn, wuq, wukv)


def _gdn_body(qkv_ref, ab_ref, z_ref, gpar_ref, nw_ref, o_ref, state_ref, *, nc):
    c = CHUNK
    ts = nc * c
    hw = HEADS * HEAD_DIM

    @pl.when(pl.program_id(1) == 0)
    def _():
        state_ref[...] = jnp.zeros_like(state_ref)

    ab = ab_ref[...]
    xg = ab + gpar_ref[0:1, :]
    softplus = jnp.maximum(xg, 0.0) + jnp.log(1.0 + jnp.exp(-jnp.abs(xg)))
    g = -jnp.exp(gpar_ref[1:2, :]) * softplus
    beta_all = jax.nn.sigmoid(ab)

    rt = lax.broadcasted_iota(jnp.int32, (ts, ts), 0)
    ct = lax.broadcasted_iota(jnp.int32, (ts, ts), 1)
    same = (rt // c) == (ct // c)
    tri_t = jnp.logical_and(same, rt <= ct).astype(BF16)
    g_t = g.T
    g_hi = g_t.astype(BF16)
    rest = g_t - g_hi.astype(F32)
    g_mid = rest.astype(BF16)
    g_lo = (rest - g_mid.astype(F32)).astype(BF16)
    gc_t = _dot(g_hi, tri_t) + _dot(g_mid, tri_t) + _dot(g_lo, tri_t)
    gc_row = gc_t[0:HEADS, :]
    gc_col = gc_t.T

    ri = lax.broadcasted_iota(jnp.int32, (c, 2 * c), 0)
    ci = lax.broadcasted_iota(jnp.int32, (c, 2 * c), 1) % c
    left = lax.broadcasted_iota(jnp.int32, (1, 2 * c), 1) < c
    incl = ri >= ci
    strict = ri > ci
    eye = (ri == ci).astype(F32)
    m16 = ((ri // 16) == (ci // 16)).astype(F32)
    zero_n = jnp.zeros((c, c), BF16)
    zero_w = jnp.zeros((c, HEAD_DIM), BF16)

    def bd_narrow(x):
        xb = x.astype(BF16)
        return jnp.concatenate([jnp.concatenate([xb[:, :c], zero_n], axis=1),
                                jnp.concatenate([zero_n, xb[:, c:]], axis=1)], axis=0)

    def bd_wide(x):
        xb = x.astype(BF16)
        return jnp.concatenate([jnp.concatenate([xb[:, :HEAD_DIM], zero_w], axis=1),
                                jnp.concatenate([zero_w, xb[:, HEAD_DIM:]], axis=1)], axis=0)

    def pmul(x, y):
        return _dot(x.astype(BF16), bd_narrow(y))

    def wide(col_a, col_b):
        return jnp.concatenate([jnp.broadcast_to(col_a, (c, HEAD_DIM)), jnp.broadcast_to(col_b, (c, HEAD_DIM))],
                               axis=1)

    its = [(ch, pr) for ch in range(nc) for pr in range(HEADS // 2)]
    n = range(len(its))
    rows = lambda ch: slice(ch * c, (ch + 1) * c)
    pcols = lambda base, pr: slice(base + 2 * pr * HEAD_DIM, base + (2 * pr + 2) * HEAD_DIM)
    q = [qkv_ref[rows(ch), pcols(0, pr)] for ch, pr in its]
    k = [qkv_ref[rows(ch), pcols(hw, pr)] for ch, pr in its]
    v = [qkv_ref[rows(ch), pcols(2 * hw, pr)].astype(F32) for ch, pr in its]
    kf = [t.astype(F32) for t in k]
    gca = [gc_col[rows(ch), 2 * pr:2 * pr + 1] for ch, pr in its]
    gcb = [gc_col[rows(ch), 2 * pr + 1:2 * pr + 2] for ch, pr in its]
    gcol_n = [jnp.where(left, gca[i], gcb[i]) for i in n]
    grow_n = [jnp.concatenate([gc_row[2 * pr:2 * pr + 1, rows(ch)], gc_row[2 * pr + 1:2 * pr + 2, rows(ch)]],
                              axis=1) for ch, pr in its]
    decay = [jnp.exp(jnp.where(incl, gcol_n[i] - grow_n[i], -jnp.inf)) for i in n]
    beta_w = [wide(beta_all[rows(ch), HEADS + 2 * pr:HEADS + 2 * pr + 1],
                   beta_all[rows(ch), HEADS + 2 * pr + 1:HEADS + 2 * pr + 2]) for ch, pr in its]
    egc_w = [jnp.exp(wide(gca[i], gcb[i])) for i in n]
    genda = [t[c - 1:c, :] for t in gca]
    gendb = [t[c - 1:c, :] for t in gcb]
    kend_w = [jnp.exp(wide(genda[i] - gca[i], gendb[i] - gcb[i])) for i in n]
    kb = [kf[i] * beta_w[i] for i in n]
    kblk = [bd_wide(t) for t in k]
    low = [jnp.where(strict, _dot_nt(kb[i].astype(BF16), kblk[i]) * decay[i], 0.0) for i in n]
    qk = [(_dot_nt(q[i], kblk[i]) * decay[i]).astype(BF16) for i in n]

    ld = [t * m16 for t in low]
    off = [low[i] - ld[i] for i in n]
    td = [eye - t for t in ld]
    p = [pmul(t, t) for t in ld]
    td = [td[i] + pmul(td[i], p[i]) for i in n]
    p = [pmul(t, t) for t in p]
    td = [td[i] + pmul(td[i], p[i]) for i in n]
    p = [pmul(t, t) for t in p]
    td = [td[i] + pmul(td[i], p[i]) for i in n]
    m = [pmul(td[i], off[i]) for i in n]
    mm = [pmul(t, t) for t in m]
    t1 = [td[i] - pmul(m[i], td[i]) for i in n]
    tinv = [(t1[i] + pmul(mm[i], t1[i])).astype(BF16) for i in n]
    u = [_dot(tinv[i], bd_wide(v[i] * beta_w[i])) for i in n]
    w = [_dot(tinv[i], bd_wide(kb[i] * egc_w[i])) for i in n]
    wq = [jnp.concatenate([w[i], q[i].astype(F32) * egc_w[i]], axis=0).astype(BF16) for i in n]
    k_end = [(kf[i] * kend_w[i]).astype(BF16) for i in n]

    state = [state_ref[h] for h in range(HEADS)]
    zero_s = jnp.zeros((HEAD_DIM, HEAD_DIM), BF16)
    prs = range(HEADS // 2)
    for ch in range(nc):
        idx = [ch * (HEADS // 2) + pr for pr in prs]
        sblk = [jnp.concatenate([jnp.concatenate([state[2 * pr].astype(BF16), zero_s], axis=1),
                                 jnp.concatenate([zero_s, state[2 * pr + 1].astype(BF16)], axis=1)], axis=0)
                for pr in prs]
        ws_qs = [_dot(wq[i], sblk[pr]) for pr, i in enumerate(idx)]
        vnb = [(u[i] - ws_qs[pr][:c]).astype(BF16) for pr, i in enumerate(idx)]
        o = [ws_qs[pr][c:] + _dot(qk[i], bd_wide(vnb[pr])) for pr, i in enumerate(idx)]
        upd = [_dot_tn(k_end[i], vnb[pr]) for pr, i in enumerate(idx)]
        for pr, i in enumerate(idx):
            state[2 * pr] = state[2 * pr] * jnp.exp(genda[i]) + upd[pr][:HEAD_DIM, :HEAD_DIM]
            state[2 * pr + 1] = state[2 * pr + 1] * jnp.exp(gendb[i]) + upd[pr][HEAD_DIM:, HEAD_DIM:]
        for h in range(HEADS):
            lo = h * HEAD_DIM
            out = _rms(o[h // 2][:, (h % 2) * HEAD_DIM:(h % 2 + 1) * HEAD_DIM], nw_ref[...])
            out = out * _silu(z_ref[rows(ch), lo:lo + HEAD_DIM].astype(F32))
            o_ref[rows(ch), lo:lo + HEAD_DIM] = out.astype(o_ref.dtype)
    for h in range(HEADS):
        state_ref[h] = state[h]


def _gdn(qkv, ab, z, gpar, nw, batch, seq, nc):
    t, qkv_w = qkv.shape
    hw = HEADS * HEAD_DIM
    ts = nc * CHUNK
    assert ts % LANES == 0 and seq % ts == 0
    n = seq // ts
    row = lambda w: pl.BlockSpec((ts, w), lambda b, s: (b * n + s, 0))
    return pl.pallas_call(
        functools.partial(_gdn_body, nc=nc),
        grid=(batch, n),
        in_specs=[row(qkv_w), row(LANES), row(hw), _resident(gpar.shape), _resident(nw.shape)],
        out_specs=row(hw),
        out_shape=jax.ShapeDtypeStruct((t, hw), BF16),
        scratch_shapes=[pltpu.VMEM((HEADS, HEAD_DIM, HEAD_DIM), F32)],
        compiler_params=pltpu.CompilerParams(dimension_semantics=("arbitrary", "arbitrary"),
                                             vmem_limit_bytes=VMEM_LIMIT),
        name="gdn",
    )(qkv, ab, z, gpar, nw)


def _attn_body(q_ref, k_ref, v_ref, o_ref, *, th, nq):
    qi = pl.program_id(2)
    qs = [q_ref[c * th:(c + 1) * th, :] for c in range(nq)]

    def scores(qx, off):
        return _dot_nt(k_ref[pl.ds(off, th), :], qx)

    def update(st, s, off):
        m, l, acc = st
        m_new = jnp.maximum(m, jnp.max(s, axis=0, keepdims=True))
        alpha = jnp.exp2(m - m_new)
        p = jnp.exp2(s - m_new)
        l = alpha * l + jnp.sum(p, axis=0, keepdims=True)
        acc = alpha * acc + _dot_tn(v_ref[pl.ds(off, th), :], p.astype(BF16))
        return m_new, l, acc

    def body(j, sts):
        offs = [pl.multiple_of(j * (2 * th), 2 * th) + t * th for t in range(2)]
        s = [[scores(qx, off) for qx in qs] for off in offs]
        for t, off in enumerate(offs):
            sts = tuple(update(sts[c], s[t][c], off) for c in range(nq))
        return sts

    init = (jnp.full((1, th), -jnp.inf, F32), jnp.zeros((1, th), F32), jnp.zeros((HEAD_DIM, th), F32))
    sts = lax.fori_loop(0, qi * (nq // 2), body, (init,) * nq)

    base = pl.multiple_of(qi * (nq * th), nq * th)
    ck = lax.broadcasted_iota(jnp.int32, (th, th), 0) // CHUNK
    cq = lax.broadcasted_iota(jnp.int32, (th, th), 1) // CHUNK
    visible = ck <= cq
    sts = list(sts)
    for j in range(nq):
        off = base + j * th
        s = [scores(qs[c], off) for c in range(j, nq)]
        s[0] = jnp.where(visible, s[0], -jnp.inf)
        for c in range(j, nq):
            sts[c] = update(sts[c], s[c - j], off)
    for c in range(nq):
        _, l, acc = sts[c]
        o_ref[c * th:(c + 1) * th, :] = (acc / l).T.astype(o_ref.dtype)


def _attn(q, k, v, batch, seq, th, nq):
    hw = HEADS * HEAD_DIM
    tq = nq * th
    assert nq % 2 == 0 and seq % tq == 0
    wide = lambda a: a.reshape(batch, seq, -1)
    head_all = lambda w: pl.BlockSpec((None, seq, w), lambda b, h, i: (b, 0, h))
    out = pl.pallas_call(
        functools.partial(_attn_body, th=th, nq=nq),
        grid=(batch, HEADS, seq // tq),
        in_specs=[pl.BlockSpec((None, tq, 2 * HEAD_DIM), lambda b, h, i: (b, i, h)), head_all(2 * HEAD_DIM),
                  head_all(HEAD_DIM)],
        out_specs=pl.BlockSpec((None, tq, HEAD_DIM), lambda b, h, i: (b, i, h)),
        out_shape=jax.ShapeDtypeStruct((batch, seq, hw), BF16),
        compiler_params=pltpu.CompilerParams(dimension_semantics=("arbitrary", "arbitrary", "arbitrary"),
                                             vmem_limit_bytes=VMEM_LIMIT),
        name="attn",
    )(wide(q), wide(k), wide(v))
    return out.reshape(batch * seq, hw)


def _mix_body(oa_ref, at_ref, g_ref, x_ref, wa_ref, wb_ref, wo_ref, nw_ref, o_ref):
    d = x_ref.shape[1]
    ya = _dot(oa_ref[...], wa_ref[...])
    yb = _dot(at_ref[...], wb_ref[...])
    ga = jax.nn.sigmoid(g_ref[:, :d].astype(F32))
    gb = jax.nn.sigmoid(g_ref[:, d:].astype(F32))
    h = ga * ya + gb * yb
    y = _dot(h.astype(BF16), wo_ref[...])
    o_ref[...] = x_ref[...] + _rms(y, nw_ref[...])


def _mix(oa, at, gate, x2, wa, wb, wo, nw, tm):
    t, d = x2.shape
    row = lambda w: pl.BlockSpec((tm, w), lambda i: (i, 0))
    return pl.pallas_call(
        _mix_body,
        grid=(t // tm,),
        in_specs=[row(oa.shape[1]), row(at.shape[1]), row(2 * d), row(d), _resident(wa.shape),
                  _resident(wb.shape), _resident(wo.shape), _resident(nw.shape)],
        out_specs=row(d),
        out_shape=jax.ShapeDtypeStruct((t, d), F32),
        compiler_params=pltpu.CompilerParams(dimension_semantics=("arbitrary",), vmem_limit_bytes=VMEM_LIMIT),
        name="mix",
    )(oa, at, gate, x2, wa, wb, wo, nw)


def _ffn_body(x_ref, n1_ref, w1_ref, w2_ref, n2_ref, o_ref, *, fc):
    x = x_ref[...]
    u = _rms(x, n1_ref[...]).astype(BF16)
    acc = None
    for c0 in range(0, w1_ref.shape[1], fc):
        hcol = jnp.maximum(_dot(u, w1_ref[:, c0:c0 + fc]), 0.0)
        part = _dot((hcol * hcol).astype(BF16), w2_ref[c0:c0 + fc, :])
        acc = part if acc is None else acc + part
    o_ref[...] = x + _rms(acc, n2_ref[...])


def _ffn(x2, n1, w1, w2, n2, tm):
    t, d = x2.shape
    row = pl.BlockSpec((tm, d), lambda i: (i, 0))
    return pl.pallas_call(
        functools.partial(_ffn_body, fc=min(1024, w1.shape[1])),
        grid=(t // tm,),
        in_specs=[row, _resident(n1.shape), _resident(w1.shape), _resident(w2.shape), _resident(n2.shape)],
        out_specs=row,
        out_shape=jax.ShapeDtypeStruct((t, d), F32),
        compiler_params=pltpu.CompilerParams(dimension_semantics=("arbitrary",), vmem_limit_bytes=VMEM_LIMIT),
        name="ffn",
    )(x2, n1, w1, w2, n2)


def _rope_table():
    inv_freq = ROPE_BASE ** (-jnp.arange(0, MLA_ROPE, 2, dtype=F32) / MLA_ROPE)
    half = MLA_ROPE // 2
    zeros = jnp.zeros((LANES - MLA_ROPE,), F32)
    rows = [jnp.concatenate([inv_freq, inv_freq, zeros]),
            jnp.concatenate([jnp.ones((MLA_ROPE,), F32), zeros]),
            jnp.concatenate([-jnp.ones((half,), F32), jnp.ones((half,), F32), zeros])]
    return jnp.concatenate([jnp.stack(rows), jnp.zeros((5, LANES), F32)], axis=0)


def _pad_cols(w, width):
    return jnp.pad(w, ((0, 0), (0, width - w.shape[1])))


def _layer_weights(w_in, w_uq, w_ukv):
    hw = HEADS * HEAD_DIM
    o = 3 * hw
    qkv, z = w_in[:, :o], w_in[:, o:o + hw]
    o += hw
    a_b = w_in[:, o:o + 2 * HEADS]
    o += 2 * HEADS
    cq = w_in[:, o:o + MLA_Q_LORA]
    o += MLA_Q_LORA
    ckv = w_in[:, o:o + MLA_KV_LORA]
    o += MLA_KV_LORA
    kr = w_in[:, o:o + MLA_ROPE]
    o += MLA_ROPE
    gates = w_in[:, o:]
    half = MLA_ROPE // 2
    krs = jnp.concatenate([kr[:, half:], kr[:, :half]], axis=1)
    wbig = jnp.concatenate([qkv, z, gates], axis=1).astype(BF16)
    wsm = jnp.concatenate([cq, ckv, _pad_cols(kr, LANES), _pad_cols(krs, LANES), _pad_cols(a_b, LANES)],
                          axis=1).astype(BF16)
    uq = w_uq.reshape(MLA_Q_LORA, HEADS, MLA_QK_DIM)
    nope, pe = uq[:, :, :HEAD_DIM], uq[:, :, HEAD_DIM:]
    pes = jnp.concatenate([pe[:, :, half:], pe[:, :, :half]], axis=2)
    padh = lambda w: jnp.pad(w, ((0, 0), (0, 0), (0, HEAD_DIM - MLA_ROPE))).reshape(MLA_Q_LORA, hw)
    wuq = jnp.concatenate([nope.reshape(MLA_Q_LORA, hw), padh(pe), padh(pes)], axis=1).astype(BF16)
    ukv = w_ukv.reshape(MLA_KV_LORA, HEADS, 2 * HEAD_DIM)
    wukv = jnp.concatenate([ukv[:, :, :HEAD_DIM].reshape(MLA_KV_LORA, hw),
                            ukv[:, :, HEAD_DIM:].reshape(MLA_KV_LORA, hw)], axis=1).astype(BF16)
    return wbig, wsm, wuq, wukv


def kernel(x, positions, w_in, conv_w, A_log, dt_bias, gdn_norm_w, q_a_norm_w, w_uq, kv_a_norm_w, w_ukv, w_branch_a, w_branch_b, w_out, pre_mix_norm_w, post_mix_norm_w, pre_ffn_norm_w, post_ffn_norm_w, w_ff1, w_ff2):
    batch, seq, d = x.shape
    t = batch * seq
    depth = w_in.shape[0]
    tm_proj = min(256, t)
    tm_mlp = min(512, t)
    th = min(512, seq // 2)
    nq = 4 if seq % (4 * th) == 0 else 2
    x2 = x.reshape(t, d)
    pos2 = positions.astype(F32).reshape(t, 1)
    cosv, sinv = _rope(pos2, _rope_table(), tm_mlp)
    row = lambda v: v.reshape(1, -1)
    for l in range(depth):
        wbig, wsm, wuq, wukv = _layer_weights(w_in[l], w_uq[l], w_ukv[l])
        convw = jnp.pad(conv_w[l], ((0, 8 - CONV_WIDTH), (0, 0)))
        qkv, z, gate, ab, q, kc, va = _inproj(
            x2, cosv, sinv, row(pre_mix_norm_w[l]), wbig, wsm, convw, row(q_a_norm_w[l]), row(kv_a_norm_w[l]),
            wuq, wukv, tm_proj, seq)
        gpar = jnp.zeros((8, LANES), F32).at[0, :HEADS].set(dt_bias[l]).at[1, :HEADS].set(A_log[l])
        oa = _gdn(qkv, ab, z, gpar, row(gdn_norm_w[l]), batch, seq, GDN_CHUNKS_PER_STEP)
        at = _attn(q, kc, va, batch, seq, th, nq)
        x2 = _mix(oa, at, gate, x2, w_branch_a[l].astype(BF16), w_branch_b[l].astype(BF16),
                  w_out[l].astype(BF16), row(post_mix_norm_w[l]), tm_mlp)
        x2 = _ffn(x2, row(pre_ffn_norm_w[l]), w_ff1[l].astype(BF16), w_ff2[l].astype(BF16),
                  row(post_ffn_norm_w[l]), tm_mlp)
    return x2.reshape(batch, seq, d)
```

```python
import functools

import jax
import jax.numpy as jnp
from jax import lax
from jax.experimental import pallas as pl
from jax.experimental.pallas import tpu as pltpu

F32 = jnp.float32
BF16 = jnp.bfloat16

LANES = 128
SUBLANES = 8
CHUNK = 64
NORM_EPS = 1e-6
HEADS = 8
HEAD_DIM = 128
CONV_WIDTH = 4
CONV_TAIL = 8
MLA_Q_LORA = 384
MLA_KV_LORA = 256
MLA_ROPE = 64
MLA_QK_DIM = HEAD_DIM + MLA_ROPE
ROPE_BASE = 10000.0
LOG2_E = 1.4426950408889634
GDN_CHUNKS_PER_STEP = 4
VMEM_LIMIT = 56 * 1024 * 1024


def _rms(x, w):
    return x * lax.rsqrt(jnp.mean(x * x, axis=-1, keepdims=True) + NORM_EPS) * w


def _sigmoid(x):
    return 0.5 * jnp.tanh(0.5 * x) + 0.5


def _silu(x):
    h = 0.5 * x
    return h * jnp.tanh(h) + h


def _dot(a, b):
    return jnp.dot(a, b, preferred_element_type=F32)


def _dot_nt(a, b):
    return lax.dot_general(a, b, (((1,), (1,)), ((), ())), preferred_element_type=F32)


def _dot_tn(a, b):
    return lax.dot_general(a, b, (((0,), (0,)), ((), ())), preferred_element_type=F32)


def _dotb(a, b):
    return _dot(a.astype(BF16), b.astype(BF16))


def _resident(shape):
    nd = len(shape)
    return pl.BlockSpec(shape, lambda *_: (0,) * nd, pipeline_mode=pl.Buffered(1))


def _rope_body(pos_ref, rope_ref, cos_ref, sin_ref):
    ang = pos_ref[...] * rope_ref[0:1, :]
    cos_ref[...] = jnp.cos(ang) * rope_ref[1:2, :]
    sin_ref[...] = jnp.sin(ang) * rope_ref[2:3, :]


def _rope(pos2, rope_tab, tm):
    t = pos2.shape[0]
    out = pl.BlockSpec((tm, LANES), lambda i: (i, 0))
    return pl.pallas_call(
        _rope_body,
        grid=(t // tm,),
        in_specs=[pl.BlockSpec((tm, 1), lambda i: (i, 0)), _resident(rope_tab.shape)],
        out_specs=[out, out],
        out_shape=[jax.ShapeDtypeStruct((t, LANES), F32)] * 2,
        compiler_params=pltpu.CompilerParams(dimension_semantics=("arbitrary",), vmem_limit_bytes=VMEM_LIMIT),
        name="rope",
    )(pos2, rope_tab)


def _inproj_body(x_ref, cos_ref, sin_ref, nw_ref, wbig_ref, wsm_ref, convw_ref, qan_ref, kvn_ref, wuq_ref, wukv_ref,
                 qkv_ref, z_ref, gate_ref, ab_ref, q_ref, k_ref, vt_ref, buf_ref, *, tiles_per_seq):
    tm, d = x_ref.shape
    hw = HEADS * HEAD_DIM
    qkv_w = qkv_ref.shape[1]
    u = _rms(x_ref[...], nw_ref[...]).astype(BF16)

    @pl.when(pl.program_id(0) % tiles_per_seq == 0)
    def _():
        buf_ref[tm:tm + CONV_TAIL, :] = jnp.zeros((CONV_TAIL, qkv_w), F32)

    buf_ref[0:CONV_TAIL, :] = buf_ref[tm:tm + CONV_TAIL, :]

    sm = _dot(u, wsm_ref[...])
    cq = sm[:, :MLA_Q_LORA]
    ckv = sm[:, MLA_Q_LORA:MLA_Q_LORA + MLA_KV_LORA]
    o = MLA_Q_LORA + MLA_KV_LORA
    kr, krs, ab = sm[:, o:o + LANES], sm[:, o + LANES:o + 2 * LANES], sm[:, o + 2 * LANES:]
    ab_ref[...] = ab

    cosv = cos_ref[...]
    sinv = sin_ref[...]
    kpe = kr * cosv + krs * sinv

    cqn = _rms(cq, qan_ref[...]).astype(BF16)
    q3 = _dot(cqn, wuq_ref[...])
    scale = MLA_QK_DIM ** -0.5 * LOG2_E
    for h in range(HEADS):
        lo = h * HEAD_DIM
        qn = q3[:, lo:lo + HEAD_DIM]
        qp = q3[:, hw + lo:hw + lo + HEAD_DIM] * cosv + q3[:, 2 * hw + lo:2 * hw + lo + HEAD_DIM] * sinv
        q_ref[:, 2 * lo:2 * lo + HEAD_DIM] = (qn * scale).astype(BF16)
        q_ref[:, 2 * lo + HEAD_DIM:2 * lo + 2 * HEAD_DIM] = (qp * scale).astype(BF16)

    ckvn = _rms(ckv, kvn_ref[...]).astype(BF16)
    kv = _dot(ckvn, wukv_ref[...])
    for h in range(HEADS):
        lo = h * HEAD_DIM
        k_ref[:, 2 * lo:2 * lo + HEAD_DIM] = kv[:, lo:lo + HEAD_DIM].astype(BF16)
        k_ref[:, 2 * lo + HEAD_DIM:2 * lo + 2 * HEAD_DIM] = kpe.astype(BF16)
    vt_ref[0] = kv[:, hw:].T.astype(BF16)

    sl = 2 * HEAD_DIM
    for s in range(qkv_w // sl):
        buf_ref[CONV_TAIL:CONV_TAIL + tm, s * sl:(s + 1) * sl] = _dot(u, wbig_ref[:, s * sl:(s + 1) * sl])
        zg = _dot(u, wbig_ref[:, qkv_w + s * sl:qkv_w + (s + 1) * sl]).astype(BF16)
        if s * sl < d:
            z_ref[:, s * sl:(s + 1) * sl] = zg
        else:
            gate_ref[:, s * sl - d:(s + 1) * sl - d] = zg
        for grp in range(2 * s, 2 * s + 2):
            lo = grp * HEAD_DIM
            acc = None
            for j in range(CONV_WIDTH):
                r0 = CONV_TAIL - (CONV_WIDTH - 1) + j
                term = buf_ref[r0:r0 + tm, lo:lo + HEAD_DIM] * convw_ref[j:j + 1, lo:lo + HEAD_DIM]
                acc = term if acc is None else acc + term
            y = _silu(acc)
            if grp < 2 * HEADS:
                y = y * lax.rsqrt(jnp.sum(y * y, axis=-1, keepdims=True) + NORM_EPS)
            if grp < HEADS:
                y = y * (HEAD_DIM ** -0.5)
            qkv_ref[:, lo:lo + HEAD_DIM] = y.astype(BF16)


def _inproj(x2, cosv, sinv, nw, wbig, wsm, convw, qan, kvn, wuq, wukv, tm, seq):
    t, d = x2.shape
    qkv_w = 3 * HEADS * HEAD_DIM
    hw = HEADS * HEAD_DIM
    row = lambda w: pl.BlockSpec((tm, w), lambda i: (i, 0))
    outs = [(qkv_w, BF16), (d, BF16), (2 * d, BF16), (LANES, F32), (2 * hw, BF16), (2 * hw, BF16)]
    return pl.pallas_call(
        functools.partial(_inproj_body, tiles_per_seq=seq // tm),
        grid=(t // tm,),
        in_specs=[row(d), row(LANES), row(LANES), _resident(nw.shape), _resident(wbig.shape), _resident(wsm.shape),
                  _resident(convw.shape), _resident(qan.shape), _resident(kvn.shape), _resident(wuq.shape),
                  _resident(wukv.shape)],
        out_specs=[row(w) for w, _ in outs] + [pl.BlockSpec((1, hw, tm), lambda i: (i, 0, 0))],
        out_shape=[jax.ShapeDtypeStruct((t, w), dt) for w, dt in outs]
        + [jax.ShapeDtypeStruct((t // tm, hw, tm), BF16)],
        scratch_shapes=[pltpu.VMEM((tm + CONV_TAIL, qkv_w), F32)],
        compiler_params=pltpu.CompilerParams(dimension_semantics=("arbitrary",), vmem_limit_bytes=VMEM_LIMIT),
        name="inproj",
    )(x2, cosv, sinv, nw, wbig, wsm, convw, qan, kvn, wuq, wukv)


def _gdn_body(qkv_ref, ab_ref, z_ref, gpar_ref, nw_ref, o_ref, state_ref, *, nc):
    c = CHUNK
    ts = nc * c
    hw = HEADS * HEAD_DIM

    @pl.when(pl.program_id(1) == 0)
    def _():
        state_ref[...] = jnp.zeros_like(state_ref)

    ab = ab_ref[...]
    xg = ab + gpar_ref[0:1, :]
    softplus = jnp.maximum(xg, 0.0) + jnp.log(1.0 + jnp.exp(-jnp.abs(xg)))
    g = -jnp.exp(gpar_ref[1:2, :]) * softplus
    beta_all = _sigmoid(ab)

    rt = lax.broadcasted_iota(jnp.int32, (ts, ts), 0)
    ct = lax.broadcasted_iota(jnp.int32, (ts, ts), 1)
    same = (rt // c) == (ct // c)
    tri_t = jnp.logical_and(same, rt <= ct).astype(BF16)
    g_t = g.T
    g_hi = g_t.astype(BF16)
    rest = g_t - g_hi.astype(F32)
    g_mid = rest.astype(BF16)
    g_lo = (rest - g_mid.astype(F32)).astype(BF16)
    gc_t = _dot(g_hi, tri_t) + _dot(g_mid, tri_t) + _dot(g_lo, tri_t)
    gc_row = gc_t[0:HEADS, :]
    gc_col = gc_t.T

    ri = lax.broadcasted_iota(jnp.int32, (c, 2 * c), 0)
    ci = lax.broadcasted_iota(jnp.int32, (c, 2 * c), 1) % c
    left = lax.broadcasted_iota(jnp.int32, (1, 2 * c), 1) < c
    incl = ri >= ci
    strict = ri > ci
    eye = (ri == ci).astype(F32)
    m16 = ((ri // 16) == (ci // 16)).astype(F32)
    zero_n = jnp.zeros((c, c), BF16)
    zero_w = jnp.zeros((c, HEAD_DIM), BF16)

    def bd_narrow(x):
        xb = x.astype(BF16)
        return jnp.concatenate([jnp.concatenate([xb[:, :c], zero_n], axis=1),
                                jnp.concatenate([zero_n, xb[:, c:]], axis=1)], axis=0)

    def bd_wide(x):
        xb = x.astype(BF16)
        return jnp.concatenate([jnp.concatenate([xb[:, :HEAD_DIM], zero_w], axis=1),
                                jnp.concatenate([zero_w, xb[:, HEAD_DIM:]], axis=1)], axis=0)

    def pmul(x, y):
        return _dot(x.astype(BF16), bd_narrow(y))

    def wide(col_a, col_b):
        return jnp.concatenate([jnp.broadcast_to(col_a, (c, HEAD_DIM)), jnp.broadcast_to(col_b, (c, HEAD_DIM))],
                               axis=1)

    its = [(ch, pr) for ch in range(nc) for pr in range(HEADS // 2)]
    n = range(len(its))
    rows = lambda ch: slice(ch * c, (ch + 1) * c)
    pcols = lambda base, pr: slice(base + 2 * pr * HEAD_DIM, base + (2 * pr + 2) * HEAD_DIM)
    q = [qkv_ref[rows(ch), pcols(0, pr)] for ch, pr in its]
    k = [qkv_ref[rows(ch), pcols(hw, pr)] for ch, pr in its]
    v = [qkv_ref[rows(ch), pcols(2 * hw, pr)].astype(F32) for ch, pr in its]
    kf = [t.astype(F32) for t in k]
    gca = [gc_col[rows(ch), 2 * pr:2 * pr + 1] for ch, pr in its]
    gcb = [gc_col[rows(ch), 2 * pr + 1:2 * pr + 2] for ch, pr in its]
    gcol_n = [jnp.where(left, gca[i], gcb[i]) for i in n]
    grow_n = [jnp.concatenate([gc_row[2 * pr:2 * pr + 1, rows(ch)], gc_row[2 * pr + 1:2 * pr + 2, rows(ch)]],
                              axis=1) for ch, pr in its]
    decay = [jnp.exp(jnp.where(incl, gcol_n[i] - grow_n[i], -jnp.inf)) for i in n]
    beta_w = [wide(beta_all[rows(ch), HEADS + 2 * pr:HEADS + 2 * pr + 1],
                   beta_all[rows(ch), HEADS + 2 * pr + 1:HEADS + 2 * pr + 2]) for ch, pr in its]
    egc_w = [jnp.exp(wide(gca[i], gcb[i])) for i in n]
    genda = [t[c - 1:c, :] for t in gca]
    gendb = [t[c - 1:c, :] for t in gcb]
    kend_w = [jnp.exp(wide(genda[i] - gca[i], gendb[i] - gcb[i])) for i in n]
    kb = [kf[i] * beta_w[i] for i in n]
    kblk = [bd_wide(t) for t in k]
    low = [jnp.where(strict, _dot_nt(kb[i].astype(BF16), kblk[i]) * decay[i], 0.0) for i in n]
    qk = [(_dot_nt(q[i], kblk[i]) * decay[i]).astype(BF16) for i in n]

    ld = [t * m16 for t in low]
    off = [low[i] - ld[i] for i in n]
    td = [eye - t for t in ld]
    p = [pmul(t, t) for t in ld]
    td = [td[i] + pmul(td[i], p[i]) for i in n]
    p = [pmul(t, t) for t in p]
    td = [td[i] + pmul(td[i], p[i]) for i in n]
    p = [pmul(t, t) for t in p]
    td = [td[i] + pmul(td[i], p[i]) for i in n]
    m = [pmul(td[i], off[i]) for i in n]
    mm = [pmul(t, t) for t in m]
    t1 = [td[i] - pmul(m[i], td[i]) for i in n]
    tinv = [(t1[i] + pmul(mm[i], t1[i])).astype(BF16) for i in n]
    u = [_dot(tinv[i], bd_wide(v[i] * beta_w[i])) for i in n]
    w = [_dot(tinv[i], bd_wide(kb[i] * egc_w[i])) for i in n]
    wq = [jnp.concatenate([w[i], q[i].astype(F32) * egc_w[i]], axis=0).astype(BF16) for i in n]
    k_end = [(kf[i] * kend_w[i]).astype(BF16) for i in n]

    state = [state_ref[h] for h in range(HEADS)]
    zero_s = jnp.zeros((HEAD_DIM, HEAD_DIM), BF16)
    prs = range(HEADS // 2)
    for ch in range(nc):
        idx = [ch * (HEADS // 2) + pr for pr in prs]
        sblk = [jnp.concatenate([jnp.concatenate([state[2 * pr].astype(BF16), zero_s], axis=1),
                                 jnp.concatenate([zero_s, state[2 * pr + 1].astype(BF16)], axis=1)], axis=0)
                for pr in prs]
        ws_qs = [_dot(wq[i], sblk[pr]) for pr, i in enumerate(idx)]
        vnb = [(u[i] - ws_qs[pr][:c]).astype(BF16) for pr, i in enumerate(idx)]
        o = [ws_qs[pr][c:] + _dot(qk[i], bd_wide(vnb[pr])) for pr, i in enumerate(idx)]
        upd = [_dot_tn(k_end[i], vnb[pr]) for pr, i in enumerate(idx)]
        for pr, i in enumerate(idx):
            state[2 * pr] = state[2 * pr] * jnp.exp(genda[i]) + upd[pr][:HEAD_DIM, :HEAD_DIM]
            state[2 * pr + 1] = state[2 * pr + 1] * jnp.exp(gendb[i]) + upd[pr][HEAD_DIM:, HEAD_DIM:]
        for h in range(HEADS):
            lo = h * HEAD_DIM
            out = _rms(o[h // 2][:, (h % 2) * HEAD_DIM:(h % 2 + 1) * HEAD_DIM], nw_ref[...])
            out = out * _silu(z_ref[rows(ch), lo:lo + HEAD_DIM].astype(F32))
            o_ref[rows(ch), lo:lo + HEAD_DIM] = out.astype(o_ref.dtype)
    for h in range(HEADS):
        state_ref[h] = state[h]


def _gdn(qkv, ab, z, gpar, nw, batch, seq, nc):
    t, qkv_w = qkv.shape
    hw = HEADS * HEAD_DIM
    ts = nc * CHUNK
    assert ts % LANES == 0 and seq % ts == 0
    n = seq // ts
    row = lambda w: pl.BlockSpec((ts, w), lambda b, s: (b * n + s, 0))
    return pl.pallas_call(
        functools.partial(_gdn_body, nc=nc),
        grid=(batch, n),
        in_specs=[row(qkv_w), row(LANES), row(hw), _resident(gpar.shape), _resident(nw.shape)],
        out_specs=row(hw),
        out_shape=jax.ShapeDtypeStruct((t, hw), BF16),
        scratch_shapes=[pltpu.VMEM((HEADS, HEAD_DIM, HEAD_DIM), F32)],
        compiler_params=pltpu.CompilerParams(dimension_semantics=("arbitrary", "arbitrary"),
                                             vmem_limit_bytes=VMEM_LIMIT),
        name="gdn",
    )(qkv, ab, z, gpar, nw)


def _attn_body(q_ref, k_ref, vt_ref, o_ref, *, th, nq):
    qi = pl.program_id(2)
    tv = vt_ref.shape[2]
    ones = jnp.ones((2 * SUBLANES, th), BF16)
    qs = [q_ref[c * th:(c + 1) * th, :] for c in range(nq)]

    def scores(qx, off):
        return _dot_nt(k_ref[pl.ds(off, th), :], qx)

    def update(st, s, off):
        m, l, acc = st
        m_new = jnp.maximum(m, jnp.max(s, axis=0, keepdims=True))
        alpha = jnp.exp2(m - m_new)
        p = jnp.exp2(s - m_new).astype(BF16)
        vt = jnp.concatenate([vt_ref[off // tv + kb] for kb in range(th // tv)], axis=1)
        pv = _dot(jnp.concatenate([vt, ones], axis=0), p)
        l = alpha * l + pv[HEAD_DIM:HEAD_DIM + 1, :]
        acc = alpha * acc + pv[:HEAD_DIM, :]
        return m_new, l, acc

    def body(j, sts):
        offs = [pl.multiple_of(j * (2 * th), 2 * th) + t * th for t in range(2)]
        s = [[scores(qx, off) for qx in qs] for off in offs]
        for t, off in enumerate(offs):
            sts = tuple(update(sts[c], s[t][c], off) for c in range(nq))
        return sts

    init = (jnp.full((1, th), -jnp.inf, F32), jnp.zeros((1, th), F32), jnp.zeros((HEAD_DIM, th), F32))
    sts = lax.fori_loop(0, qi * (nq // 2), body, (init,) * nq)

    base = pl.multiple_of(qi * (nq * th), nq * th)
    ck = lax.broadcasted_iota(jnp.int32, (th, th), 0) // CHUNK
    cq = lax.broadcasted_iota(jnp.int32, (th, th), 1) // CHUNK
    visible = ck <= cq
    sts = list(sts)
    for j in range(nq):
        off = base + j * th
        s = [scores(qs[c], off) for c in range(j, nq)]
        s[0] = jnp.where(visible, s[0], -jnp.inf)
        for c in range(j, nq):
            sts[c] = update(sts[c], s[c - j], off)
    for c in range(nq):
        _, l, acc = sts[c]
        o_ref[c * th:(c + 1) * th, :] = (acc / l).T.astype(o_ref.dtype)


def _attn(q, k, vt, batch, seq, th, nq):
    hw = HEADS * HEAD_DIM
    tq = nq * th
    assert nq % 2 == 0 and seq % tq == 0
    wide = lambda a: a.reshape(batch, seq, -1)
    tv = vt.shape[2]
    vt4 = vt.reshape(batch, seq // tv, hw, tv)
    out = pl.pallas_call(
        functools.partial(_attn_body, th=th, nq=nq),
        grid=(batch, HEADS, seq // tq),
        in_specs=[pl.BlockSpec((None, tq, 2 * HEAD_DIM), lambda b, h, i: (b, i, h)),
                  pl.BlockSpec((None, seq, 2 * HEAD_DIM), lambda b, h, i: (b, 0, h)),
                  pl.BlockSpec((None, seq // tv, HEAD_DIM, tv), lambda b, h, i: (b, 0, h, 0))],
        out_specs=pl.BlockSpec((None, tq, HEAD_DIM), lambda b, h, i: (b, i, h)),
        out_shape=jax.ShapeDtypeStruct((batch, seq, hw), BF16),
        compiler_params=pltpu.CompilerParams(dimension_semantics=("arbitrary", "arbitrary", "arbitrary"),
                                             vmem_limit_bytes=VMEM_LIMIT),
        name="attn",
    )(wide(q), wide(k), vt4)
    return out.reshape(batch * seq, hw)


def _mix_body(oa_ref, at_ref, g_ref, x_ref, wa_ref, wb_ref, wo_ref, nw_ref, o_ref):
    d = x_ref.shape[1]
    ya = _dot(oa_ref[...], wa_ref[...])
    yb = _dot(at_ref[...], wb_ref[...])
    ga = _sigmoid(g_ref[:, :d].astype(F32))
    gb = _sigmoid(g_ref[:, d:].astype(F32))
    h = ga * ya + gb * yb
    y = _dot(h.astype(BF16), wo_ref[...])
    o_ref[...] = x_ref[...] + _rms(y, nw_ref[...])


def _mix(oa, at, gate, x2, wa, wb, wo, nw, tm):
    t, d = x2.shape
    row = lambda w: pl.BlockSpec((tm, w), lambda i: (i, 0))
    return pl.pallas_call(
        _mix_body,
        grid=(t // tm,),
        in_specs=[row(oa.shape[1]), row(at.shape[1]), row(2 * d), row(d), _resident(wa.shape),
                  _resident(wb.shape), _resident(wo.shape), _resident(nw.shape)],
        out_specs=row(d),
        out_shape=jax.ShapeDtypeStruct((t, d), F32),
        compiler_params=pltpu.CompilerParams(dimension_semantics=("arbitrary",), vmem_limit_bytes=VMEM_LIMIT),
        name="mix",
    )(oa, at, gate, x2, wa, wb, wo, nw)


def _ffn_body(x_ref, n1_ref, w1_ref, w2_ref, n2_ref, o_ref, *, fc):
    x = x_ref[...]
    u = _rms(x, n1_ref[...]).astype(BF16)
    acc = None
    for c0 in range(0, w1_ref.shape[1], fc):
        hcol = jnp.maximum(_dot(u, w1_ref[:, c0:c0 + fc]), 0.0)
        part = _dot((hcol * hcol).astype(BF16), w2_ref[c0:c0 + fc, :])
        acc = part if acc is None else acc + part
    o_ref[...] = x + _rms(acc, n2_ref[...])


def _ffn(x2, n1, w1, w2, n2, tm):
    t, d = x2.shape
    row = pl.BlockSpec((tm, d), lambda i: (i, 0))
    return pl.pallas_call(
        functools.partial(_ffn_body, fc=min(1024, w1.shape[1])),
        grid=(t // tm,),
        in_specs=[row, _resident(n1.shape), _resident(w1.shape), _resident(w2.shape), _resident(n2.shape)],
        out_specs=row,
        out_shape=jax.ShapeDtypeStruct((t, d), F32),
        compiler_params=pltpu.CompilerParams(dimension_semantics=("arbitrary",), vmem_limit_bytes=VMEM_LIMIT),
        name="ffn",
    )(x2, n1, w1, w2, n2)


def _rope_table():
    inv_freq = ROPE_BASE ** (-jnp.arange(0, MLA_ROPE, 2, dtype=F32) / MLA_ROPE)
    half = MLA_ROPE // 2
    zeros = jnp.zeros((LANES - MLA_ROPE,), F32)
    rows = [jnp.concatenate([inv_freq, inv_freq, zeros]),
            jnp.concatenate([jnp.ones((MLA_ROPE,), F32), zeros]),
            jnp.concatenate([-jnp.ones((half,), F32), jnp.ones((half,), F32), zeros])]
    return jnp.concatenate([jnp.stack(rows), jnp.zeros((5, LANES), F32)], axis=0)


def _pad_cols(w, width):
    return jnp.pad(w, ((0, 0), (0, width - w.shape[1])))


def _layer_weights(w_in, w_uq, w_ukv):
    hw = HEADS * HEAD_DIM
    o = 3 * hw
    qkv, z = w_in[:, :o], w_in[:, o:o + hw]
    o += hw
    a_b = w_in[:, o:o + 2 * HEADS]
    o += 2 * HEADS
    cq = w_in[:, o:o + MLA_Q_LORA]
    o += MLA_Q_LORA
    ckv = w_in[:, o:o + MLA_KV_LORA]
    o += MLA_KV_LORA
    kr = w_in[:, o:o + MLA_ROPE]
    o += MLA_ROPE
    gates = w_in[:, o:]
    half = MLA_ROPE // 2
    krs = jnp.concatenate([kr[:, half:], kr[:, :half]], axis=1)
    wbig = jnp.concatenate([qkv, z, gates], axis=1).astype(BF16)
    wsm = jnp.concatenate([cq, ckv, _pad_cols(kr, LANES), _pad_cols(krs, LANES), _pad_cols(a_b, LANES)],
                          axis=1).astype(BF16)
    uq = w_uq.reshape(MLA_Q_LORA, HEADS, MLA_QK_DIM)
    nope, pe = uq[:, :, :HEAD_DIM], uq[:, :, HEAD_DIM:]
    pes = jnp.concatenate([pe[:, :, half:], pe[:, :, :half]], axis=2)
    padh = lambda w: jnp.pad(w, ((0, 0), (0, 0), (0, HEAD_DIM - MLA_ROPE))).reshape(MLA_Q_LORA, hw)
    wuq = jnp.concatenate([nope.reshape(MLA_Q_LORA, hw), padh(pe), padh(pes)], axis=1).astype(BF16)
    ukv = w_ukv.reshape(MLA_KV_LORA, HEADS, 2 * HEAD_DIM)
    wukv = jnp.concatenate([ukv[:, :, :HEAD_DIM].reshape(MLA_KV_LORA, hw),
                            ukv[:, :, HEAD_DIM:].reshape(MLA_KV_LORA, hw)], axis=1).astype(BF16)
    return wbig, wsm, wuq, wukv


def kernel(x, positions, w_in, conv_w, A_log, dt_bias, gdn_norm_w, q_a_norm_w, w_uq, kv_a_norm_w, w_ukv, w_branch_a, w_branch_b, w_out, pre_mix_norm_w, post_mix_norm_w, pre_ffn_norm_w, post_ffn_norm_w, w_ff1, w_ff2):
    batch, seq, d = x.shape
    t = batch * seq
    depth = w_in.shape[0]
    tm_proj = min(256, t)
    tm_mlp = min(512, t)
    th = min(512, seq // 2)
    nq = 4 if seq % (4 * th) == 0 else 2
    x2 = x.reshape(t, d)
    pos2 = positions.astype(F32).reshape(t, 1)
    cosv, sinv = _rope(pos2, _rope_table(), tm_mlp)
    row = lambda v: v.reshape(1, -1)
    for l in range(depth):
        wbig, wsm, wuq, wukv = _layer_weights(w_in[l], w_uq[l], w_ukv[l])
        convw = jnp.pad(conv_w[l], ((0, 8 - CONV_WIDTH), (0, 0)))
        qkv, z, gate, ab, q, kc, va = _inproj(
            x2, cosv, sinv, row(pre_mix_norm_w[l]), wbig, wsm, convw, row(q_a_norm_w[l]), row(kv_a_norm_w[l]),
            wuq, wukv, tm_proj, seq)
        gpar = jnp.zeros((8, LANES), F32).at[0, :HEADS].set(dt_bias[l]).at[1, :HEADS].set(A_log[l])
        oa = _gdn(qkv, ab, z, gpar, row(gdn_norm_w[l]), batch, seq, GDN_CHUNKS_PER_STEP)
        at = _attn(q, kc, va, batch, seq, th, nq)
        x2 = _mix(oa, at, gate, x2, w_branch_a[l].astype(BF16), w_branch_b[l].astype(BF16),
                  w_out[l].astype(BF16), row(post_mix_norm_w[l]), tm_mlp)
        x2 = _ffn(x2, row(pre_ffn_norm_w[l]), w_ff1[l].astype(BF16), w_ff2[l].astype(BF16),
                  row(post_ffn_norm_w[l]), tm_mlp)
    return x2.reshape(batch, seq, d)
```

```python
import functools

import jax
import jax.numpy as jnp
from jax import lax
from jax.experimental import pallas as pl
from jax.experimental.pallas import tpu as pltpu

F32 = jnp.float32
BF16 = jnp.bfloat16

LANES = 128
SUBLANES = 8
CHUNK = 64
NORM_EPS = 1e-6
HEADS = 8
HEAD_DIM = 128
CONV_WIDTH = 4
CONV_TAIL = 8
MLA_Q_LORA = 384
MLA_KV_LORA = 256
MLA_ROPE = 64
MLA_QK_DIM = HEAD_DIM + MLA_ROPE
ROPE_BASE = 10000.0
LOG2_E = 1.4426950408889634
GDN_CHUNKS_PER_STEP = 4
VMEM_LIMIT = 56 * 1024 * 1024


def _rms(x, w):
    return x * lax.rsqrt(jnp.mean(x * x, axis=-1, keepdims=True) + NORM_EPS) * w


def _sigmoid(x):
    return 0.5 * jnp.tanh(0.5 * x) + 0.5


def _silu(x):
    h = 0.5 * x
    return h * jnp.tanh(h) + h


def _dot(a, b):
    return jnp.dot(a, b, preferred_element_type=F32)


def _dot_nt(a, b):
    return lax.dot_general(a, b, (((1,), (1,)), ((), ())), preferred_element_type=F32)


def _dot_tn(a, b):
    return lax.dot_general(a, b, (((0,), (0,)), ((), ())), preferred_element_type=F32)


def _dotb(a, b):
    return _dot(a.astype(BF16), b.astype(BF16))


def _resident(shape):
    nd = len(shape)
    return pl.BlockSpec(shape, lambda *_: (0,) * nd, pipeline_mode=pl.Buffered(1))


def _rope_body(pos_ref, rope_ref, cos_ref, sin_ref):
    ang = pos_ref[...] * rope_ref[0:1, :]
    cos_ref[...] = jnp.cos(ang) * rope_ref[1:2, :]
    sin_ref[...] = jnp.sin(ang) * rope_ref[2:3, :]


def _rope(pos2, rope_tab, tm):
    t = pos2.shape[0]
    out = pl.BlockSpec((tm, LANES), lambda i: (i, 0))
    return pl.pallas_call(
        _rope_body,
        grid=(t // tm,),
        in_specs=[pl.BlockSpec((tm, 1), lambda i: (i, 0)), _resident(rope_tab.shape)],
        out_specs=[out, out],
        out_shape=[jax.ShapeDtypeStruct((t, LANES), F32)] * 2,
        compiler_params=pltpu.CompilerParams(dimension_semantics=("arbitrary",), vmem_limit_bytes=VMEM_LIMIT),
        name="rope",
    )(pos2, rope_tab)


def _inproj_body(x_ref, cos_ref, sin_ref, nw_ref, wbig_ref, wsm_ref, convw_ref, qan_ref, kvn_ref, wuq_ref, wukv_ref,
                 qkv_ref, z_ref, gate_ref, ab_ref, q_ref, k_ref, vt_ref, buf_ref, *, tiles_per_seq):
    tm, d = x_ref.shape
    hw = HEADS * HEAD_DIM
    qkv_w = qkv_ref.shape[1]
    u = _rms(x_ref[...], nw_ref[...]).astype(BF16)

    @pl.when(pl.program_id(0) % tiles_per_seq == 0)
    def _():
        buf_ref[tm:tm + CONV_TAIL, :] = jnp.zeros((CONV_TAIL, qkv_w), F32)

    buf_ref[0:CONV_TAIL, :] = buf_ref[tm:tm + CONV_TAIL, :]

    sm = _dot(u, wsm_ref[...])
    cq = sm[:, :MLA_Q_LORA]
    ckv = sm[:, MLA_Q_LORA:MLA_Q_LORA + MLA_KV_LORA]
    o = MLA_Q_LORA + MLA_KV_LORA
    kr, krs, ab = sm[:, o:o + LANES], sm[:, o + LANES:o + 2 * LANES], sm[:, o + 2 * LANES:]
    ab_ref[...] = ab

    cosv = cos_ref[...]
    sinv = sin_ref[...]
    kpe = kr * cosv + krs * sinv

    cqn = _rms(cq, qan_ref[...]).astype(BF16)
    q3 = _dot(cqn, wuq_ref[...])
    scale = MLA_QK_DIM ** -0.5 * LOG2_E
    for h in range(HEADS):
        lo = h * HEAD_DIM
        qn = q3[:, lo:lo + HEAD_DIM]
        qp = q3[:, hw + lo:hw + lo + HEAD_DIM] * cosv + q3[:, 2 * hw + lo:2 * hw + lo + HEAD_DIM] * sinv
        q_ref[:, 2 * lo:2 * lo + HEAD_DIM] = qn * scale
        q_ref[:, 2 * lo + HEAD_DIM:2 * lo + 2 * HEAD_DIM] = qp * scale

    ckvn = _rms(ckv, kvn_ref[...]).astype(BF16)
    kv = _dot(ckvn, wukv_ref[...])
    for h in range(HEADS):
        lo = h * HEAD_DIM
        k_ref[:, 2 * lo:2 * lo + HEAD_DIM] = kv[:, lo:lo + HEAD_DIM].astype(BF16)
        k_ref[:, 2 * lo + HEAD_DIM:2 * lo + 2 * HEAD_DIM] = kpe.astype(BF16)
    vt_ref[0] = kv[:, hw:].T.astype(BF16)

    sl = 2 * HEAD_DIM
    for s in range(qkv_w // sl):
        buf_ref[CONV_TAIL:CONV_TAIL + tm, s * sl:(s + 1) * sl] = _dot(u, wbig_ref[:, s * sl:(s + 1) * sl])
        zg = _dot(u, wbig_ref[:, qkv_w + s * sl:qkv_w + (s + 1) * sl]).astype(BF16)
        if s * sl < d:
            z_ref[:, s * sl:(s + 1) * sl] = zg
        else:
            gate_ref[:, s * sl - d:(s + 1) * sl - d] = zg
        for grp in range(2 * s, 2 * s + 2):
            lo = grp * HEAD_DIM
            acc = None
            for j in range(CONV_WIDTH):
                r0 = CONV_TAIL - (CONV_WIDTH - 1) + j
                term = buf_ref[r0:r0 + tm, lo:lo + HEAD_DIM] * convw_ref[j:j + 1, lo:lo + HEAD_DIM]
                acc = term if acc is None else acc + term
            y = _silu(acc)
            if grp < 2 * HEADS:
                y = y * lax.rsqrt(jnp.sum(y * y, axis=-1, keepdims=True) + NORM_EPS)
            if grp < HEADS:
                y = y * (HEAD_DIM ** -0.5)
            qkv_ref[:, lo:lo + HEAD_DIM] = y


def _inproj(x2, cosv, sinv, nw, wbig, wsm, convw, qan, kvn, wuq, wukv, tm, seq):
    t, d = x2.shape
    qkv_w = 3 * HEADS * HEAD_DIM
    hw = HEADS * HEAD_DIM
    row = lambda w: pl.BlockSpec((tm, w), lambda i: (i, 0))
    outs = [(qkv_w, F32), (d, BF16), (2 * d, BF16), (LANES, F32), (2 * hw, F32), (2 * hw, BF16)]
    return pl.pallas_call(
        functools.partial(_inproj_body, tiles_per_seq=seq // tm),
        grid=(t // tm,),
        in_specs=[row(d), row(LANES), row(LANES), _resident(nw.shape), _resident(wbig.shape), _resident(wsm.shape),
                  _resident(convw.shape), _resident(qan.shape), _resident(kvn.shape), _resident(wuq.shape),
                  _resident(wukv.shape)],
        out_specs=[row(w) for w, _ in outs] + [pl.BlockSpec((1, hw, tm), lambda i: (i, 0, 0))],
        out_shape=[jax.ShapeDtypeStruct((t, w), dt) for w, dt in outs]
        + [jax.ShapeDtypeStruct((t // tm, hw, tm), BF16)],
        scratch_shapes=[pltpu.VMEM((tm + CONV_TAIL, qkv_w), F32)],
        compiler_params=pltpu.CompilerParams(dimension_semantics=("arbitrary",), vmem_limit_bytes=VMEM_LIMIT),
        name="inproj",
    )(x2, cosv, sinv, nw, wbig, wsm, convw, qan, kvn, wuq, wukv)


def _gdn_body(qkv_ref, ab_ref, z_ref, gpar_ref, nw_ref, o_ref, state_ref, *, nc):
    c = CHUNK
    ts = nc * c
    hw = HEADS * HEAD_DIM

    @pl.when(pl.program_id(1) == 0)
    def _():
        state_ref[...] = jnp.zeros_like(state_ref)

    ab = ab_ref[...]
    xg = ab + gpar_ref[0:1, :]
    softplus = jnp.maximum(xg, 0.0) + jnp.log(1.0 + jnp.exp(-jnp.abs(xg)))
    g = -jnp.exp(gpar_ref[1:2, :]) * softplus
    beta_all = _sigmoid(ab)

    rt = lax.broadcasted_iota(jnp.int32, (ts, ts), 0)
    ct = lax.broadcasted_iota(jnp.int32, (ts, ts), 1)
    same = (rt // c) == (ct // c)
    tri_t = jnp.logical_and(same, rt <= ct).astype(BF16)
    g_t = g.T
    g_hi = g_t.astype(BF16)
    rest = g_t - g_hi.astype(F32)
    g_mid = rest.astype(BF16)
    g_lo = (rest - g_mid.astype(F32)).astype(BF16)
    gc_t = _dot(g_hi, tri_t) + _dot(g_mid, tri_t) + _dot(g_lo, tri_t)
    gc_row = gc_t[0:HEADS, :]
    gc_col = gc_t.T

    ri = lax.broadcasted_iota(jnp.int32, (c, 2 * c), 0)
    ci = lax.broadcasted_iota(jnp.int32, (c, 2 * c), 1) % c
    left = lax.broadcasted_iota(jnp.int32, (1, 2 * c), 1) < c
    incl = ri >= ci
    strict = ri > ci
    eye = (ri == ci).astype(F32)
    m16 = ((ri // 16) == (ci // 16)).astype(F32)
    zero_n = jnp.zeros((c, c), BF16)
    zero_w = jnp.zeros((c, HEAD_DIM), BF16)

    def bd_narrow(x):
        xb = x.astype(BF16)
        return jnp.concatenate([jnp.concatenate([xb[:, :c], zero_n], axis=1),
                                jnp.concatenate([zero_n, xb[:, c:]], axis=1)], axis=0)

    def bd_wide(x):
        xb = x.astype(BF16)
        return jnp.concatenate([jnp.concatenate([xb[:, :HEAD_DIM], zero_w], axis=1),
                                jnp.concatenate([zero_w, xb[:, HEAD_DIM:]], axis=1)], axis=0)

    def pmul(x, y):
        return _dot(x.astype(BF16), bd_narrow(y))

    def wide(col_a, col_b):
        return jnp.concatenate([jnp.broadcast_to(col_a, (c, HEAD_DIM)), jnp.broadcast_to(col_b, (c, HEAD_DIM))],
                               axis=1)

    its = [(ch, pr) for ch in range(nc) for pr in range(HEADS // 2)]
    n = range(len(its))
    rows = lambda ch: slice(ch * c, (ch + 1) * c)
    pcols = lambda base, pr: slice(base + 2 * pr * HEAD_DIM, base + (2 * pr + 2) * HEAD_DIM)
    qf = [qkv_ref[rows(ch), pcols(0, pr)] for ch, pr in its]
    kf = [qkv_ref[rows(ch), pcols(hw, pr)] for ch, pr in its]
    v = [qkv_ref[rows(ch), pcols(2 * hw, pr)] for ch, pr in its]
    gca = [gc_col[rows(ch), 2 * pr:2 * pr + 1] for ch, pr in its]
    gcb = [gc_col[rows(ch), 2 * pr + 1:2 * pr + 2] for ch, pr in its]
    gcol_n = [jnp.where(left, gca[i], gcb[i]) for i in n]
    grow_n = [jnp.concatenate([gc_row[2 * pr:2 * pr + 1, rows(ch)], gc_row[2 * pr + 1:2 * pr + 2, rows(ch)]],
                              axis=1) for ch, pr in its]
    decay = [jnp.exp(jnp.where(incl, gcol_n[i] - grow_n[i], -jnp.inf)) for i in n]
    beta_w = [wide(beta_all[rows(ch), HEADS + 2 * pr:HEADS + 2 * pr + 1],
                   beta_all[rows(ch), HEADS + 2 * pr + 1:HEADS + 2 * pr + 2]) for ch, pr in its]
    egc_w = [jnp.exp(wide(gca[i], gcb[i])) for i in n]
    genda = [t[c - 1:c, :] for t in gca]
    gendb = [t[c - 1:c, :] for t in gcb]
    kend_w = [jnp.exp(wide(genda[i] - gca[i], gendb[i] - gcb[i])) for i in n]
    kb = [kf[i] * beta_w[i] for i in n]
    kblk = [bd_wide(t) for t in kf]
    low = [jnp.where(strict, _dot_nt(kb[i].astype(BF16), kblk[i]) * decay[i], 0.0) for i in n]
    qk = [(_dot_nt(qf[i].astype(BF16), kblk[i]) * decay[i]).astype(BF16) for i in n]

    ld = [t * m16 for t in low]
    off = [low[i] - ld[i] for i in n]
    td = [eye - t for t in ld]
    p = [pmul(t, t) for t in ld]
    td = [td[i] + pmul(td[i], p[i]) for i in n]
    p = [pmul(t, t) for t in p]
    td = [td[i] + pmul(td[i], p[i]) for i in n]
    p = [pmul(t, t) for t in p]
    td = [td[i] + pmul(td[i], p[i]) for i in n]
    m = [pmul(td[i], off[i]) for i in n]
    mm = [pmul(t, t) for t in m]
    t1 = [td[i] - pmul(m[i], td[i]) for i in n]
    tinv = [(t1[i] + pmul(mm[i], t1[i])).astype(BF16) for i in n]
    u = [_dot(tinv[i], bd_wide(v[i] * beta_w[i])) for i in n]
    w = [_dot(tinv[i], bd_wide(kb[i] * egc_w[i])) for i in n]
    wq = [jnp.concatenate([w[i], qf[i] * egc_w[i]], axis=0).astype(BF16) for i in n]
    k_end = [(kf[i] * kend_w[i]).astype(BF16) for i in n]

    state = [state_ref[h] for h in range(HEADS)]
    zero_s = jnp.zeros((HEAD_DIM, HEAD_DIM), BF16)
    prs = range(HEADS // 2)
    for ch in range(nc):
        idx = [ch * (HEADS // 2) + pr for pr in prs]
        sblk = [jnp.concatenate([jnp.concatenate([state[2 * pr].astype(BF16), zero_s], axis=1),
                                 jnp.concatenate([zero_s, state[2 * pr + 1].astype(BF16)], axis=1)], axis=0)
                for pr in prs]
        ws_qs = [_dot(wq[i], sblk[pr]) for pr, i in enumerate(idx)]
        vnb = [(u[i] - ws_qs[pr][:c]).astype(BF16) for pr, i in enumerate(idx)]
        o = [ws_qs[pr][c:] + _dot(qk[i], bd_wide(vnb[pr])) for pr, i in enumerate(idx)]
        upd = [_dot_tn(k_end[i], vnb[pr]) for pr, i in enumerate(idx)]
        for pr, i in enumerate(idx):
            state[2 * pr] = state[2 * pr] * jnp.exp(genda[i]) + upd[pr][:HEAD_DIM, :HEAD_DIM]
            state[2 * pr + 1] = state[2 * pr + 1] * jnp.exp(gendb[i]) + upd[pr][HEAD_DIM:, HEAD_DIM:]
        for h in range(HEADS):
            lo = h * HEAD_DIM
            out = _rms(o[h // 2][:, (h % 2) * HEAD_DIM:(h % 2 + 1) * HEAD_DIM], nw_ref[...])
            out = out * _silu(z_ref[rows(ch), lo:lo + HEAD_DIM].astype(F32))
            o_ref[rows(ch), lo:lo + HEAD_DIM] = out.astype(o_ref.dtype)
    for h in range(HEADS):
        state_ref[h] = state[h]


def _gdn(qkv, ab, z, gpar, nw, batch, seq, nc):
    t, qkv_w = qkv.shape
    hw = HEADS * HEAD_DIM
    ts = nc * CHUNK
    assert ts % LANES == 0 and seq % ts == 0
    n = seq // ts
    row = lambda w: pl.BlockSpec((ts, w), lambda b, s: (b * n + s, 0))
    return pl.pallas_call(
        functools.partial(_gdn_body, nc=nc),
        grid=(batch, n),
        in_specs=[row(qkv_w), row(LANES), row(hw), _resident(gpar.shape), _resident(nw.shape)],
        out_specs=row(hw),
        out_shape=jax.ShapeDtypeStruct((t, hw), BF16),
        scratch_shapes=[pltpu.VMEM((HEADS, HEAD_DIM, HEAD_DIM), F32)],
        compiler_params=pltpu.CompilerParams(dimension_semantics=("arbitrary", "arbitrary"),
                                             vmem_limit_bytes=VMEM_LIMIT),
        name="gdn",
    )(qkv, ab, z, gpar, nw)


def _attn_body(q_ref, k_ref, vt_ref, o_ref, *, th, nq):
    qi = pl.program_id(2)
    tv = vt_ref.shape[2]
    ones = jnp.ones((2 * SUBLANES, th), BF16)
    qs = [q_ref[c * th:(c + 1) * th, :].astype(BF16) for c in range(nq)]

    def scores(qx, off):
        return _dot_nt(k_ref[pl.ds(off, th), :], qx)

    def update(st, s, off):
        m, l, acc = st
        m_new = jnp.maximum(m, jnp.max(s, axis=0, keepdims=True))
        alpha = jnp.exp2(m - m_new)
        p = jnp.exp2(s - m_new).astype(BF16)
        vt = jnp.concatenate([vt_ref[off // tv + kb] for kb in range(th // tv)], axis=1)
        pv = _dot(jnp.concatenate([vt, ones], axis=0), p)
        l = alpha * l + pv[HEAD_DIM:HEAD_DIM + 1, :]
        acc = alpha * acc + pv[:HEAD_DIM, :]
        return m_new, l, acc

    def body(j, sts):
        offs = [pl.multiple_of(j * (2 * th), 2 * th) + t * th for t in range(2)]
        s = [[scores(qx, off) for qx in qs] for off in offs]
        for t, off in enumerate(offs):
            sts = tuple(update(sts[c], s[t][c], off) for c in range(nq))
        return sts

    init = (jnp.full((1, th), -jnp.inf, F32), jnp.zeros((1, th), F32), jnp.zeros((HEAD_DIM, th), F32))
    sts = lax.fori_loop(0, qi * (nq // 2), body, (init,) * nq)

    base = pl.multiple_of(qi * (nq * th), nq * th)
    ck = lax.broadcasted_iota(jnp.int32, (th, th), 0) // CHUNK
    cq = lax.broadcasted_iota(jnp.int32, (th, th), 1) // CHUNK
    visible = ck <= cq
    sts = list(sts)
    for j in range(nq):
        off = base + j * th
        s = [scores(qs[c], off) for c in range(j, nq)]
        s[0] = jnp.where(visible, s[0], -jnp.inf)
        for c in range(j, nq):
            sts[c] = update(sts[c], s[c - j], off)
    for c in range(nq):
        _, l, acc = sts[c]
        o_ref[c * th:(c + 1) * th, :] = (acc / l).T.astype(o_ref.dtype)


def _attn(q, k, vt, batch, seq, th, nq):
    hw = HEADS * HEAD_DIM
    tq = nq * th
    assert nq % 2 == 0 and seq % tq == 0
    wide = lambda a: a.reshape(batch, seq, -1)
    tv = vt.shape[2]
    vt4 = vt.reshape(batch, seq // tv, hw, tv)
    out = pl.pallas_call(
        functools.partial(_attn_body, th=th, nq=nq),
        grid=(batch, HEADS, seq // tq),
        in_specs=[pl.BlockSpec((None, tq, 2 * HEAD_DIM), lambda b, h, i: (b, i, h)),
                  pl.BlockSpec((None, seq, 2 * HEAD_DIM), lambda b, h, i: (b, 0, h)),
                  pl.BlockSpec((None, seq // tv, HEAD_DIM, tv), lambda b, h, i: (b, 0, h, 0))],
        out_specs=pl.BlockSpec((None, tq, HEAD_DIM), lambda b, h, i: (b, i, h)),
        out_shape=jax.ShapeDtypeStruct((batch, seq, hw), BF16),
        compiler_params=pltpu.CompilerParams(dimension_semantics=("arbitrary", "arbitrary", "arbitrary"),
                                             vmem_limit_bytes=VMEM_LIMIT),
        name="attn",
    )(wide(q), wide(k), vt4)
    return out.reshape(batch * seq, hw)


def _mlp_body(oa_ref, at_ref, g_ref, x_ref, wa_ref, wb_ref, wo_ref, nmix_ref, n1_ref, w1_ref, w2_ref, n2_ref,
              o_ref, *, fc):
    d = x_ref.shape[1]
    ya = _dot(oa_ref[...], wa_ref[...])
    yb = _dot(at_ref[...], wb_ref[...])
    ga = _sigmoid(g_ref[:, :d].astype(F32))
    gb = _sigmoid(g_ref[:, d:].astype(F32))
    h = ga * ya + gb * yb
    x = x_ref[...] + _rms(_dot(h.astype(BF16), wo_ref[...]), nmix_ref[...])
    u = _rms(x, n1_ref[...]).astype(BF16)
    acc = None
    for c0 in range(0, w1_ref.shape[1], fc):
        hcol = jnp.maximum(_dot(u, w1_ref[:, c0:c0 + fc]), 0.0)
        part = _dot((hcol * hcol).astype(BF16), w2_ref[c0:c0 + fc, :])
        acc = part if acc is None else acc + part
    o_ref[...] = x + _rms(acc, n2_ref[...])


def _mlp(oa, at, gate, x2, wa, wb, wo, nmix, n1, w1, w2, n2, tm):
    t, d = x2.shape
    row = lambda w: pl.BlockSpec((tm, w), lambda i: (i, 0))
    weights = (wa, wb, wo, nmix, n1, w1, w2, n2)
    return pl.pallas_call(
        functools.partial(_mlp_body, fc=min(1024, w1.shape[1])),
        grid=(t // tm,),
        in_specs=[row(oa.shape[1]), row(at.shape[1]), row(2 * d), row(d)] + [_resident(w.shape) for w in weights],
        out_specs=row(d),
        out_shape=jax.ShapeDtypeStruct((t, d), F32),
        compiler_params=pltpu.CompilerParams(dimension_semantics=("arbitrary",), vmem_limit_bytes=VMEM_LIMIT),
        name="mlp",
    )(oa, at, gate, x2, *weights)


def _rope_table():
    inv_freq = ROPE_BASE ** (-jnp.arange(0, MLA_ROPE, 2, dtype=F32) / MLA_ROPE)
    half = MLA_ROPE // 2
    zeros = jnp.zeros((LANES - MLA_ROPE,), F32)
    rows = [jnp.concatenate([inv_freq, inv_freq, zeros]),
            jnp.concatenate([jnp.ones((MLA_ROPE,), F32), zeros]),
            jnp.concatenate([-jnp.ones((half,), F32), jnp.ones((half,), F32), zeros])]
    return jnp.concatenate([jnp.stack(rows), jnp.zeros((5, LANES), F32)], axis=0)


def _pad_cols(w, width):
    return jnp.pad(w, ((0, 0), (0, width - w.shape[1])))


def _layer_weights(w_in, w_uq, w_ukv):
    hw = HEADS * HEAD_DIM
    o = 3 * hw
    qkv, z = w_in[:, :o], w_in[:, o:o + hw]
    o += hw
    a_b = w_in[:, o:o + 2 * HEADS]
    o += 2 * HEADS
    cq = w_in[:, o:o + MLA_Q_LORA]
    o += MLA_Q_LORA
    ckv = w_in[:, o:o + MLA_KV_LORA]
    o += MLA_KV_LORA
    kr = w_in[:, o:o + MLA_ROPE]
    o += MLA_ROPE
    gates = w_in[:, o:]
    half = MLA_ROPE // 2
    krs = jnp.concatenate([kr[:, half:], kr[:, :half]], axis=1)
    wbig = jnp.concatenate([qkv, z, gates], axis=1).astype(BF16)
    wsm = jnp.concatenate([cq, ckv, _pad_cols(kr, LANES), _pad_cols(krs, LANES), _pad_cols(a_b, LANES)],
                          axis=1).astype(BF16)
    uq = w_uq.reshape(MLA_Q_LORA, HEADS, MLA_QK_DIM)
    nope, pe = uq[:, :, :HEAD_DIM], uq[:, :, HEAD_DIM:]
    pes = jnp.concatenate([pe[:, :, half:], pe[:, :, :half]], axis=2)
    padh = lambda w: jnp.pad(w, ((0, 0), (0, 0), (0, HEAD_DIM - MLA_ROPE))).reshape(MLA_Q_LORA, hw)
    wuq = jnp.concatenate([nope.reshape(MLA_Q_LORA, hw), padh(pe), padh(pes)], axis=1).astype(BF16)
    ukv = w_ukv.reshape(MLA_KV_LORA, HEADS, 2 * HEAD_DIM)
    wukv = jnp.concatenate([ukv[:, :, :HEAD_DIM].reshape(MLA_KV_LORA, hw),
                            ukv[:, :, HEAD_DIM:].reshape(MLA_KV_LORA, hw)], axis=1).astype(BF16)
    return wbig, wsm, wuq, wukv


def kernel(x, positions, w_in, conv_w, A_log, dt_bias, gdn_norm_w, q_a_norm_w, w_uq, kv_a_norm_w, w_ukv, w_branch_a, w_branch_b, w_out, pre_mix_norm_w, post_mix_norm_w, pre_ffn_norm_w, post_ffn_norm_w, w_ff1, w_ff2):
    batch, seq, d = x.shape
    t = batch * seq
    depth = w_in.shape[0]
    tm_proj = min(256, t)
    tm_mlp = min(512, t)
    th = min(512, seq // 2)
    nq = 4 if seq % (4 * th) == 0 else 2
    x2 = x.reshape(t, d)
    pos2 = positions.astype(F32).reshape(t, 1)
    cosv, sinv = _rope(pos2, _rope_table(), tm_mlp)
    row = lambda v: v.reshape(1, -1)
    for l in range(depth):
        wbig, wsm, wuq, wukv = _layer_weights(w_in[l], w_uq[l], w_ukv[l])
        convw = jnp.pad(conv_w[l], ((0, 8 - CONV_WIDTH), (0, 0)))
        qkv, z, gate, ab, q, kc, va = _inproj(
            x2, cosv, sinv, row(pre_mix_norm_w[l]), wbig, wsm, convw, row(q_a_norm_w[l]), row(kv_a_norm_w[l]),
            wuq, wukv, tm_proj, seq)
        gpar = jnp.zeros((8, LANES), F32).at[0, :HEADS].set(dt_bias[l]).at[1, :HEADS].set(A_log[l])
        oa = _gdn(qkv, ab, z, gpar, row(gdn_norm_w[l]), batch, seq, GDN_CHUNKS_PER_STEP)
        at = _attn(q, kc, va, batch, seq, th, nq)
        x2 = _mlp(oa, at, gate, x2, w_branch_a[l].astype(BF16), w_branch_b[l].astype(BF16),
                  w_out[l].astype(BF16), row(post_mix_norm_w[l]), row(pre_ffn_norm_w[l]),
                  w_ff1[l].astype(BF16), w_ff2[l].astype(BF16), row(post_ffn_norm_w[l]), tm_mlp)
    return x2.reshape(batch, seq, d)
```

```python
import functools

import jax
import jax.numpy as jnp
from jax import lax
from jax.experimental import pallas as pl
from jax.experimental.pallas import tpu as pltpu

F32 = jnp.float32
BF16 = jnp.bfloat16

LANES = 128
SUBLANES = 8
CHUNK = 64
NORM_EPS = 1e-6
HEADS = 8
HEAD_DIM = 128
CONV_WIDTH = 4
CONV_TAIL = 8
MLA_Q_LORA = 384
MLA_KV_LORA = 256
MLA_ROPE = 64
MLA_QK_DIM = HEAD_DIM + MLA_ROPE
ROPE_BASE = 10000.0
LOG2_E = 1.4426950408889634
GDN_CHUNKS_PER_STEP = 4
VMEM_LIMIT = 56 * 1024 * 1024


def _rms(x, w):
    return x * lax.rsqrt(jnp.mean(x * x, axis=-1, keepdims=True) + NORM_EPS) * w


def _sigmoid(x):
    return 0.5 * jnp.tanh(0.5 * x) + 0.5


def _silu(x):
    h = 0.5 * x
    return h * jnp.tanh(h) + h


def _dot(a, b):
    return jnp.dot(a, b, preferred_element_type=F32)


def _dot_nt(a, b):
    return lax.dot_general(a, b, (((1,), (1,)), ((), ())), preferred_element_type=F32)


def _dot_tn(a, b):
    return lax.dot_general(a, b, (((0,), (0,)), ((), ())), preferred_element_type=F32)


def _dotb(a, b):
    return _dot(a.astype(BF16), b.astype(BF16))


def _resident(shape):
    nd = len(shape)
    return pl.BlockSpec(shape, lambda *_: (0,) * nd, pipeline_mode=pl.Buffered(1))


def _rope_body(pos_ref, rope_ref, cos_ref, sin_ref):
    ang = pos_ref[...] * rope_ref[0:1, :]
    cos_ref[...] = jnp.cos(ang) * rope_ref[1:2, :]
    sin_ref[...] = jnp.sin(ang) * rope_ref[2:3, :]


def _rope(pos2, rope_tab, tm):
    t = pos2.shape[0]
    out = pl.BlockSpec((tm, LANES), lambda i: (i, 0))
    return pl.pallas_call(
        _rope_body,
        grid=(t // tm,),
        in_specs=[pl.BlockSpec((tm, 1), lambda i: (i, 0)), _resident(rope_tab.shape)],
        out_specs=[out, out],
        out_shape=[jax.ShapeDtypeStruct((t, LANES), F32)] * 2,
        compiler_params=pltpu.CompilerParams(dimension_semantics=("arbitrary",), vmem_limit_bytes=VMEM_LIMIT),
        name="rope",
    )(pos2, rope_tab)


def _inproj_body(x_ref, cos_ref, sin_ref, nw_ref, wbig_ref, wsm_ref, convw_ref, qan_ref, kvn_ref, wuq_ref, wukv_ref,
                 qkv_ref, z_ref, gate_ref, ab_ref, q_ref, k_ref, vt_ref, buf_ref, *, tiles_per_seq):
    tm, d = x_ref.shape
    hw = HEADS * HEAD_DIM
    qkv_w = qkv_ref.shape[1]
    u = _rms(x_ref[...], nw_ref[...]).astype(BF16)

    @pl.when(pl.program_id(0) % tiles_per_seq == 0)
    def _():
        buf_ref[tm:tm + CONV_TAIL, :] = jnp.zeros((CONV_TAIL, qkv_w), F32)

    buf_ref[0:CONV_TAIL, :] = buf_ref[tm:tm + CONV_TAIL, :]

    sm = _dot(u, wsm_ref[...])
    cq = sm[:, :MLA_Q_LORA]
    ckv = sm[:, MLA_Q_LORA:MLA_Q_LORA + MLA_KV_LORA]
    o = MLA_Q_LORA + MLA_KV_LORA
    kr, krs, ab = sm[:, o:o + LANES], sm[:, o + LANES:o + 2 * LANES], sm[:, o + 2 * LANES:]
    ab_ref[...] = ab

    cosv = cos_ref[...]
    sinv = sin_ref[...]
    kpe = kr * cosv + krs * sinv

    cqn = _rms(cq, qan_ref[...]).astype(BF16)
    q3 = _dot(cqn, wuq_ref[...])
    for h in range(HEADS):
        lo = h * HEAD_DIM
        qn = q3[:, lo:lo + HEAD_DIM]
        qp = q3[:, hw + lo:hw + lo + HEAD_DIM] * cosv + q3[:, 2 * hw + lo:2 * hw + lo + HEAD_DIM] * sinv
        q_ref[:, 2 * lo:2 * lo + HEAD_DIM] = qn.astype(BF16)
        q_ref[:, 2 * lo + HEAD_DIM:2 * lo + 2 * HEAD_DIM] = qp.astype(BF16)

    ckvn = _rms(ckv, kvn_ref[...]).astype(BF16)
    kv = _dot(ckvn, wukv_ref[...])
    for h in range(HEADS):
        lo = h * HEAD_DIM
        k_ref[:, 2 * lo:2 * lo + HEAD_DIM] = kv[:, lo:lo + HEAD_DIM].astype(BF16)
        k_ref[:, 2 * lo + HEAD_DIM:2 * lo + 2 * HEAD_DIM] = kpe.astype(BF16)
    vt_ref[0] = kv[:, hw:].T.astype(BF16)

    sl = 2 * HEAD_DIM
    for s in range(qkv_w // sl):
        buf_ref[CONV_TAIL:CONV_TAIL + tm, s * sl:(s + 1) * sl] = _dot(u, wbig_ref[:, s * sl:(s + 1) * sl])
        zg = _dot(u, wbig_ref[:, qkv_w + s * sl:qkv_w + (s + 1) * sl]).astype(BF16)
        if s * sl < d:
            z_ref[:, s * sl:(s + 1) * sl] = zg
        else:
            gate_ref[:, s * sl - d:(s + 1) * sl - d] = zg
        for grp in range(2 * s, 2 * s + 2):
            lo = grp * HEAD_DIM
            acc = None
            for j in range(CONV_WIDTH):
                r0 = CONV_TAIL - (CONV_WIDTH - 1) + j
                term = buf_ref[r0:r0 + tm, lo:lo + HEAD_DIM] * convw_ref[j:j + 1, lo:lo + HEAD_DIM]
                acc = term if acc is None else acc + term
            y = _silu(acc)
            if grp < 2 * HEADS:
                y = y * lax.rsqrt(jnp.sum(y * y, axis=-1, keepdims=True) + NORM_EPS)
            if grp < HEADS:
                y = y * (HEAD_DIM ** -0.5)
            qkv_ref[:, lo:lo + HEAD_DIM] = y


def _inproj(x2, cosv, sinv, nw, wbig, wsm, convw, qan, kvn, wuq, wukv, tm, seq):
    t, d = x2.shape
    qkv_w = 3 * HEADS * HEAD_DIM
    hw = HEADS * HEAD_DIM
    row = lambda w: pl.BlockSpec((tm, w), lambda i: (i, 0))
    outs = [(qkv_w, F32), (d, BF16), (2 * d, BF16), (LANES, F32), (2 * hw, BF16), (2 * hw, BF16)]
    return pl.pallas_call(
        functools.partial(_inproj_body, tiles_per_seq=seq // tm),
        grid=(t // tm,),
        in_specs=[row(d), row(LANES), row(LANES), _resident(nw.shape), _resident(wbig.shape), _resident(wsm.shape),
                  _resident(convw.shape), _resident(qan.shape), _resident(kvn.shape), _resident(wuq.shape),
                  _resident(wukv.shape)],
        out_specs=[row(w) for w, _ in outs] + [pl.BlockSpec((1, hw, tm), lambda i: (i, 0, 0))],
        out_shape=[jax.ShapeDtypeStruct((t, w), dt) for w, dt in outs]
        + [jax.ShapeDtypeStruct((t // tm, hw, tm), BF16)],
        scratch_shapes=[pltpu.VMEM((tm + CONV_TAIL, qkv_w), F32)],
        compiler_params=pltpu.CompilerParams(dimension_semantics=("arbitrary",), vmem_limit_bytes=VMEM_LIMIT),
        name="inproj",
    )(x2, cosv, sinv, nw, wbig, wsm, convw, qan, kvn, wuq, wukv)


def _gdn_body(qkv_ref, ab_ref, z_ref, gpar_ref, nw_ref, o_ref, state_ref, *, nc):
    c = CHUNK
    ts = nc * c
    hw = HEADS * HEAD_DIM

    @pl.when(pl.program_id(1) == 0)
    def _():
        state_ref[...] = jnp.zeros_like(state_ref)

    ab = ab_ref[...]
    xg = ab + gpar_ref[0:1, :]
    softplus = jnp.maximum(xg, 0.0) + jnp.log(1.0 + jnp.exp(-jnp.abs(xg)))
    g = -jnp.exp(gpar_ref[1:2, :]) * softplus
    beta_all = _sigmoid(ab)

    rt = lax.broadcasted_iota(jnp.int32, (ts, ts), 0)
    ct = lax.broadcasted_iota(jnp.int32, (ts, ts), 1)
    same = (rt // c) == (ct // c)
    tri_t = jnp.logical_and(same, rt <= ct).astype(BF16)
    g_t = g.T
    g_hi = g_t.astype(BF16)
    rest = g_t - g_hi.astype(F32)
    g_mid = rest.astype(BF16)
    g_lo = (rest - g_mid.astype(F32)).astype(BF16)
    gc_t = _dot(g_hi, tri_t) + _dot(g_mid, tri_t) + _dot(g_lo, tri_t)
    gc_row = gc_t[0:HEADS, :]
    gc_col = gc_t.T

    ri = lax.broadcasted_iota(jnp.int32, (c, 2 * c), 0)
    ci = lax.broadcasted_iota(jnp.int32, (c, 2 * c), 1) % c
    left = lax.broadcasted_iota(jnp.int32, (1, 2 * c), 1) < c
    incl = ri >= ci
    strict = ri > ci
    eye = (ri == ci).astype(F32)
    m16 = ((ri // 16) == (ci // 16)).astype(F32)
    zero_n = jnp.zeros((c, c), BF16)
    zero_w = jnp.zeros((c, HEAD_DIM), BF16)

    def bd_narrow(x):
        xb = x.astype(BF16)
        return jnp.concatenate([jnp.concatenate([xb[:, :c], zero_n], axis=1),
                                jnp.concatenate([zero_n, xb[:, c:]], axis=1)], axis=0)

    def bd_wide(x):
        xb = x.astype(BF16)
        return jnp.concatenate([jnp.concatenate([xb[:, :HEAD_DIM], zero_w], axis=1),
                                jnp.concatenate([zero_w, xb[:, HEAD_DIM:]], axis=1)], axis=0)

    def pmul(x, y):
        return _dot(x.astype(BF16), bd_narrow(y))

    def wide(col_a, col_b):
        return jnp.concatenate([jnp.broadcast_to(col_a, (c, HEAD_DIM)), jnp.broadcast_to(col_b, (c, HEAD_DIM))],
                               axis=1)

    its = [(ch, pr) for ch in range(nc) for pr in range(HEADS // 2)]
    n = range(len(its))
    rows = lambda ch: slice(ch * c, (ch + 1) * c)
    pcols = lambda base, pr: slice(base + 2 * pr * HEAD_DIM, base + (2 * pr + 2) * HEAD_DIM)
    qf = [qkv_ref[rows(ch), pcols(0, pr)] for ch, pr in its]
    kf = [qkv_ref[rows(ch), pcols(hw, pr)] for ch, pr in its]
    v = [qkv_ref[rows(ch), pcols(2 * hw, pr)] for ch, pr in its]
    gca = [gc_col[rows(ch), 2 * pr:2 * pr + 1] for ch, pr in its]
    gcb = [gc_col[rows(ch), 2 * pr + 1:2 * pr + 2] for ch, pr in its]
    gcol_n = [jnp.where(left, gca[i], gcb[i]) for i in n]
    grow_n = [jnp.concatenate([gc_row[2 * pr:2 * pr + 1, rows(ch)], gc_row[2 * pr + 1:2 * pr + 2, rows(ch)]],
                              axis=1) for ch, pr in its]
    decay = [jnp.exp(jnp.where(incl, gcol_n[i] - grow_n[i], -jnp.inf)) for i in n]
    beta_w = [wide(beta_all[rows(ch), HEADS + 2 * pr:HEADS + 2 * pr + 1],
                   beta_all[rows(ch), HEADS + 2 * pr + 1:HEADS + 2 * pr + 2]) for ch, pr in its]
    egc_w = [jnp.exp(wide(gca[i], gcb[i])) for i in n]
    genda = [t[c - 1:c, :] for t in gca]
    gendb = [t[c - 1:c, :] for t in gcb]
    kend_w = [jnp.exp(wide(genda[i] - gca[i], gendb[i] - gcb[i])) for i in n]
    kb = [kf[i] * beta_w[i] for i in n]
    kblk = [bd_wide(t) for t in kf]
    low = [jnp.where(strict, _dot_nt(kb[i].astype(BF16), kblk[i]) * decay[i], 0.0) for i in n]
    qk = [(_dot_nt(qf[i].astype(BF16), kblk[i]) * decay[i]).astype(BF16) for i in n]

    ld = [t * m16 for t in low]
    off = [low[i] - ld[i] for i in n]
    td = [eye - t for t in ld]
    p = [pmul(t, t) for t in ld]
    td = [td[i] + pmul(td[i], p[i]) for i in n]
    p = [pmul(t, t) for t in p]
    td = [td[i] + pmul(td[i], p[i]) for i in n]
    p = [pmul(t, t) for t in p]
    td = [td[i] + pmul(td[i], p[i]) for i in n]
    m = [pmul(td[i], off[i]) for i in n]
    mm = [pmul(t, t) for t in m]
    t1 = [td[i] - pmul(m[i], td[i]) for i in n]
    tinv = [(t1[i] + pmul(mm[i], t1[i])).astype(BF16) for i in n]
    u = [_dot(tinv[i], bd_wide(v[i] * beta_w[i])) for i in n]
    w = [_dot(tinv[i], bd_wide(kb[i] * egc_w[i])) for i in n]
    wq = [jnp.concatenate([w[i], qf[i] * egc_w[i]], axis=0).astype(BF16) for i in n]
    k_end = [(kf[i] * kend_w[i]).astype(BF16) for i in n]

    state = [state_ref[h] for h in range(HEADS)]
    zero_s = jnp.zeros((HEAD_DIM, HEAD_DIM), BF16)
    prs = range(HEADS // 2)
    for ch in range(nc):
        idx = [ch * (HEADS // 2) + pr for pr in prs]
        sblk = [jnp.concatenate([jnp.concatenate([state[2 * pr].astype(BF16), zero_s], axis=1),
                                 jnp.concatenate([zero_s, state[2 * pr + 1].astype(BF16)], axis=1)], axis=0)
                for pr in prs]
        ws_qs = [_dot(wq[i], sblk[pr]) for pr, i in enumerate(idx)]
        vnb = [(u[i] - ws_qs[pr][:c]).astype(BF16) for pr, i in enumerate(idx)]
        o = [ws_qs[pr][c:] + _dot(qk[i], bd_wide(vnb[pr])) for pr, i in enumerate(idx)]
        upd = [_dot_tn(k_end[i], vnb[pr]) for pr, i in enumerate(idx)]
        for pr, i in enumerate(idx):
            state[2 * pr] = state[2 * pr] * jnp.exp(genda[i]) + upd[pr][:HEAD_DIM, :HEAD_DIM]
            state[2 * pr + 1] = state[2 * pr + 1] * jnp.exp(gendb[i]) + upd[pr][HEAD_DIM:, HEAD_DIM:]
        for h in range(HEADS):
            lo = h * HEAD_DIM
            out = _rms(o[h // 2][:, (h % 2) * HEAD_DIM:(h % 2 + 1) * HEAD_DIM], nw_ref[...])
            out = out * _silu(z_ref[rows(ch), lo:lo + HEAD_DIM].astype(F32))
            o_ref[rows(ch), lo:lo + HEAD_DIM] = out.astype(o_ref.dtype)
    for h in range(HEADS):
        state_ref[h] = state[h]


def _gdn(qkv, ab, z, gpar, nw, batch, seq, nc):
    t, qkv_w = qkv.shape
    hw = HEADS * HEAD_DIM
    ts = nc * CHUNK
    assert ts % LANES == 0 and seq % ts == 0
    n = seq // ts
    row = lambda w: pl.BlockSpec((ts, w), lambda b, s: (b * n + s, 0))
    return pl.pallas_call(
        functools.partial(_gdn_body, nc=nc),
        grid=(batch, n),
        in_specs=[row(qkv_w), row(LANES), row(hw), _resident(gpar.shape), _resident(nw.shape)],
        out_specs=row(hw),
        out_shape=jax.ShapeDtypeStruct((t, hw), BF16),
        scratch_shapes=[pltpu.VMEM((HEADS, HEAD_DIM, HEAD_DIM), F32)],
        compiler_params=pltpu.CompilerParams(dimension_semantics=("arbitrary", "arbitrary"),
                                             vmem_limit_bytes=VMEM_LIMIT),
        name="gdn",
    )(qkv, ab, z, gpar, nw)


def _attn_body(q_ref, k_ref, vt_ref, o_ref, *, th, nq):
    qi = pl.program_id(2)
    tv = vt_ref.shape[2]
    ones = jnp.ones((2 * SUBLANES, th), BF16)
    qs = [q_ref[c * th:(c + 1) * th, :] for c in range(nq)]

    def scores(qx, off):
        return _dot_nt(k_ref[pl.ds(off, th), :], qx)

    def update(st, s, off):
        m, l, acc = st
        m_new = jnp.maximum(m, jnp.max(s, axis=0, keepdims=True))
        alpha = jnp.exp2(m - m_new)
        p = jnp.exp2(s - m_new).astype(BF16)
        vt = jnp.concatenate([vt_ref[off // tv + kb] for kb in range(th // tv)], axis=1)
        pv = _dot(jnp.concatenate([vt, ones], axis=0), p)
        l = alpha * l + pv[HEAD_DIM:HEAD_DIM + 1, :]
        acc = alpha * acc + pv[:HEAD_DIM, :]
        return m_new, l, acc

    def body(j, sts):
        offs = [pl.multiple_of(j * (2 * th), 2 * th) + t * th for t in range(2)]
        s = [[scores(qx, off) for qx in qs] for off in offs]
        for t, off in enumerate(offs):
            sts = tuple(update(sts[c], s[t][c], off) for c in range(nq))
        return sts

    init = (jnp.full((1, th), -jnp.inf, F32), jnp.zeros((1, th), F32), jnp.zeros((HEAD_DIM, th), F32))
    sts = lax.fori_loop(0, qi * (nq // 2), body, (init,) * nq)

    base = pl.multiple_of(qi * (nq * th), nq * th)
    ck = lax.broadcasted_iota(jnp.int32, (th, th), 0) // CHUNK
    cq = lax.broadcasted_iota(jnp.int32, (th, th), 1) // CHUNK
    visible = ck <= cq
    sts = list(sts)
    for j in range(nq):
        off = base + j * th
        s = [scores(qs[c], off) for c in range(j, nq)]
        s[0] = jnp.where(visible, s[0], -jnp.inf)
        for c in range(j, nq):
            sts[c] = update(sts[c], s[c - j], off)
    for c in range(nq):
        _, l, acc = sts[c]
        o_ref[c * th:(c + 1) * th, :] = (acc / l).T.astype(o_ref.dtype)


def _attn(q, k, vt, batch, seq, th, nq):
    hw = HEADS * HEAD_DIM
    tq = nq * th
    assert nq % 2 == 0 and seq % tq == 0
    wide = lambda a: a.reshape(batch, seq, -1)
    tv = vt.shape[2]
    vt4 = vt.reshape(batch, seq // tv, hw, tv)
    out = pl.pallas_call(
        functools.partial(_attn_body, th=th, nq=nq),
        grid=(batch, HEADS, seq // tq),
        in_specs=[pl.BlockSpec((None, tq, 2 * HEAD_DIM), lambda b, h, i: (b, i, h)),
                  pl.BlockSpec((None, seq, 2 * HEAD_DIM), lambda b, h, i: (b, 0, h)),
                  pl.BlockSpec((None, seq // tv, HEAD_DIM, tv), lambda b, h, i: (b, 0, h, 0))],
        out_specs=pl.BlockSpec((None, tq, HEAD_DIM), lambda b, h, i: (b, i, h)),
        out_shape=jax.ShapeDtypeStruct((batch, seq, hw), BF16),
        compiler_params=pltpu.CompilerParams(dimension_semantics=("arbitrary", "arbitrary", "arbitrary"),
                                             vmem_limit_bytes=VMEM_LIMIT),
        name="attn",
    )(wide(q), wide(k), vt4)
    return out.reshape(batch * seq, hw)


def _mlp_body(oa_ref, at_ref, g_ref, x_ref, wa_ref, wb_ref, wo_ref, nmix_ref, n1_ref, w1_ref, w2_ref, n2_ref,
              o_ref, *, fc):
    d = x_ref.shape[1]
    ya = _dot(oa_ref[...], wa_ref[...])
    yb = _dot(at_ref[...], wb_ref[...])
    ga = _sigmoid(g_ref[:, :d].astype(F32))
    gb = _sigmoid(g_ref[:, d:].astype(F32))
    h = ga * ya + gb * yb
    x = x_ref[...] + _rms(_dot(h.astype(BF16), wo_ref[...]), nmix_ref[...])
    u = _rms(x, n1_ref[...]).astype(BF16)
    acc = None
    for c0 in range(0, w1_ref.shape[1], fc):
        hcol = jnp.maximum(_dot(u, w1_ref[:, c0:c0 + fc]), 0.0)
        part = _dot((hcol * hcol).astype(BF16), w2_ref[c0:c0 + fc, :])
        acc = part if acc is None else acc + part
    o_ref[...] = x + _rms(acc, n2_ref[...])


def _mlp(oa, at, gate, x2, wa, wb, wo, nmix, n1, w1, w2, n2, tm):
    t, d = x2.shape
    row = lambda w: pl.BlockSpec((tm, w), lambda i: (i, 0))
    weights = (wa, wb, wo, nmix, n1, w1, w2, n2)
    return pl.pallas_call(
        functools.partial(_mlp_body, fc=min(1024, w1.shape[1])),
        grid=(t // tm,),
        in_specs=[row(oa.shape[1]), row(at.shape[1]), row(2 * d), row(d)] + [_resident(w.shape) for w in weights],
        out_specs=row(d),
        out_shape=jax.ShapeDtypeStruct((t, d), F32),
        compiler_params=pltpu.CompilerParams(dimension_semantics=("arbitrary",), vmem_limit_bytes=VMEM_LIMIT),
        name="mlp",
    )(oa, at, gate, x2, *weights)


def _rope_table():
    inv_freq = ROPE_BASE ** (-jnp.arange(0, MLA_ROPE, 2, dtype=F32) / MLA_ROPE)
    half = MLA_ROPE // 2
    zeros = jnp.zeros((LANES - MLA_ROPE,), F32)
    rows = [jnp.concatenate([inv_freq, inv_freq, zeros]),
            jnp.concatenate([jnp.ones((MLA_ROPE,), F32), zeros]),
            jnp.concatenate([-jnp.ones((half,), F32), jnp.ones((half,), F32), zeros])]
    return jnp.concatenate([jnp.stack(rows), jnp.zeros((5, LANES), F32)], axis=0)


def _pad_cols(w, width):
    return jnp.pad(w, ((0, 0), (0, width - w.shape[1])))


def _layer_weights(w_in, w_uq, w_ukv):
    hw = HEADS * HEAD_DIM
    o = 3 * hw
    qkv, z = w_in[:, :o], w_in[:, o:o + hw]
    o += hw
    a_b = w_in[:, o:o + 2 * HEADS]
    o += 2 * HEADS
    cq = w_in[:, o:o + MLA_Q_LORA]
    o += MLA_Q_LORA
    ckv = w_in[:, o:o + MLA_KV_LORA]
    o += MLA_KV_LORA
    kr = w_in[:, o:o + MLA_ROPE]
    o += MLA_ROPE
    gates = w_in[:, o:]
    half = MLA_ROPE // 2
    krs = jnp.concatenate([kr[:, half:], kr[:, :half]], axis=1)
    wbig = jnp.concatenate([qkv, z, gates], axis=1).astype(BF16)
    wsm = jnp.concatenate([cq, ckv, _pad_cols(kr, LANES), _pad_cols(krs, LANES), _pad_cols(a_b, LANES)],
                          axis=1).astype(BF16)
    uq = w_uq.reshape(MLA_Q_LORA, HEADS, MLA_QK_DIM)
    nope, pe = uq[:, :, :HEAD_DIM], uq[:, :, HEAD_DIM:]
    pes = jnp.concatenate([pe[:, :, half:], pe[:, :, :half]], axis=2)
    padh = lambda w: jnp.pad(w, ((0, 0), (0, 0), (0, HEAD_DIM - MLA_ROPE))).reshape(MLA_Q_LORA, hw)
    q_scale = MLA_QK_DIM ** -0.5 * LOG2_E
    wuq = (jnp.concatenate([nope.reshape(MLA_Q_LORA, hw), padh(pe), padh(pes)], axis=1) * q_scale).astype(BF16)
    ukv = w_ukv.reshape(MLA_KV_LORA, HEADS, 2 * HEAD_DIM)
    wukv = jnp.concatenate([ukv[:, :, :HEAD_DIM].reshape(MLA_KV_LORA, hw),
                            ukv[:, :, HEAD_DIM:].reshape(MLA_KV_LORA, hw)], axis=1).astype(BF16)
    return wbig, wsm, wuq, wukv


def kernel(x, positions, w_in, conv_w, A_log, dt_bias, gdn_norm_w, q_a_norm_w, w_uq, kv_a_norm_w, w_ukv, w_branch_a, w_branch_b, w_out, pre_mix_norm_w, post_mix_norm_w, pre_ffn_norm_w, post_ffn_norm_w, w_ff1, w_ff2):
    batch, seq, d = x.shape
    t = batch * seq
    depth = w_in.shape[0]
    tm_proj = min(256, t)
    tm_mlp = min(512, t)
    th = min(512, seq // 2)
    nq = 4 if seq % (4 * th) == 0 else 2
    x2 = x.reshape(t, d)
    pos2 = positions.astype(F32).reshape(t, 1)
    cosv, sinv = _rope(pos2, _rope_table(), tm_mlp)
    row = lambda v: v.reshape(1, -1)
    for l in range(depth):
        wbig, wsm, wuq, wukv = _layer_weights(w_in[l], w_uq[l], w_ukv[l])
        convw = jnp.pad(conv_w[l], ((0, 8 - CONV_WIDTH), (0, 0)))
        qkv, z, gate, ab, q, kc, va = _inproj(
            x2, cosv, sinv, row(pre_mix_norm_w[l]), wbig, wsm, convw, row(q_a_norm_w[l]), row(kv_a_norm_w[l]),
            wuq, wukv, tm_proj, seq)
        gpar = jnp.zeros((8, LANES), F32).at[0, :HEADS].set(dt_bias[l]).at[1, :HEADS].set(A_log[l])
        oa = _gdn(qkv, ab, z, gpar, row(gdn_norm_w[l]), batch, seq, GDN_CHUNKS_PER_STEP)
        at = _attn(q, kc, va, batch, seq, th, nq)
        x2 = _mlp(oa, at, gate, x2, w_branch_a[l].astype(BF16), w_branch_b[l].astype(BF16),
                  w_out[l].astype(BF16), row(post_mix_norm_w[l]), row(pre_ffn_norm_w[l]),
                  w_ff1[l].astype(BF16), w_ff2[l].astype(BF16), row(post_ffn_norm_w[l]), tm_mlp)
    return x2.reshape(batch, seq, d)
```

```python
import functools

import jax
import jax.numpy as jnp
from jax import lax
from jax.experimental import pallas as pl
from jax.experimental.pallas import tpu as pltpu

F32 = jnp.float32
BF16 = jnp.bfloat16

LANES = 128
SUBLANES = 8
CHUNK = 64
NORM_EPS = 1e-6
HEADS = 8
HEAD_DIM = 128
CONV_WIDTH = 4
CONV_TAIL = 8
MLA_Q_LORA = 384
MLA_KV_LORA = 256
MLA_ROPE = 64
MLA_QK_DIM = HEAD_DIM + MLA_ROPE
ROPE_BASE = 10000.0
LOG2_E = 1.4426950408889634
GDN_CHUNKS_PER_STEP = 4
VMEM_LIMIT = 56 * 1024 * 1024


def _rms(x, w):
    return x * lax.rsqrt(jnp.mean(x * x, axis=-1, keepdims=True) + NORM_EPS) * w


def _sigmoid(x):
    return 0.5 * jnp.tanh(0.5 * x) + 0.5


def _silu(x):
    h = 0.5 * x
    return h * jnp.tanh(h) + h


def _dot(a, b):
    return jnp.dot(a, b, preferred_element_type=F32)


def _dot_nt(a, b):
    return lax.dot_general(a, b, (((1,), (1,)), ((), ())), preferred_element_type=F32)


def _dot_tn(a, b):
    return lax.dot_general(a, b, (((0,), (0,)), ((), ())), preferred_element_type=F32)


def _dotb(a, b):
    return _dot(a.astype(BF16), b.astype(BF16))


def _resident(shape):
    nd = len(shape)
    return pl.BlockSpec(shape, lambda *_: (0,) * nd, pipeline_mode=pl.Buffered(1))


def _rope_body(pos_ref, rope_ref, cos_ref, sin_ref):
    ang = pos_ref[...] * rope_ref[0:1, :]
    cos_ref[...] = jnp.cos(ang) * rope_ref[1:2, :]
    sin_ref[...] = jnp.sin(ang) * rope_ref[2:3, :]


def _rope(pos2, rope_tab, tm):
    t = pos2.shape[0]
    out = pl.BlockSpec((tm, LANES), lambda i: (i, 0))
    return pl.pallas_call(
        _rope_body,
        grid=(t // tm,),
        in_specs=[pl.BlockSpec((tm, 1), lambda i: (i, 0)), _resident(rope_tab.shape)],
        out_specs=[out, out],
        out_shape=[jax.ShapeDtypeStruct((t, LANES), F32)] * 2,
        compiler_params=pltpu.CompilerParams(dimension_semantics=("arbitrary",), vmem_limit_bytes=VMEM_LIMIT),
        name="rope",
    )(pos2, rope_tab)


def _inproj_body(x_ref, cos_ref, sin_ref, nw_ref, wbig_ref, wsm_ref, convw_ref, qan_ref, kvn_ref, wuq_ref, wukv_ref,
                 qkv_ref, z_ref, gate_ref, ab_ref, q_ref, k_ref, vt_ref, buf_ref, *, tiles_per_seq):
    tm, d = x_ref.shape
    hw = HEADS * HEAD_DIM
    qkv_w = qkv_ref.shape[1]
    u = _rms(x_ref[...], nw_ref[...]).astype(BF16)

    @pl.when(pl.program_id(0) % tiles_per_seq == 0)
    def _():
        buf_ref[tm:tm + CONV_TAIL, :] = jnp.zeros((CONV_TAIL, qkv_w), F32)

    buf_ref[0:CONV_TAIL, :] = buf_ref[tm:tm + CONV_TAIL, :]

    sm = _dot(u, wsm_ref[...])
    cq = sm[:, :MLA_Q_LORA]
    ckv = sm[:, MLA_Q_LORA:MLA_Q_LORA + MLA_KV_LORA]
    o = MLA_Q_LORA + MLA_KV_LORA
    kr, krs, ab = sm[:, o:o + LANES], sm[:, o + LANES:o + 2 * LANES], sm[:, o + 2 * LANES:]
    ab_ref[...] = ab

    cosv = cos_ref[...]
    sinv = sin_ref[...]
    kpe = kr * cosv + krs * sinv

    cqn = _rms(cq, qan_ref[...]).astype(BF16)
    q3 = _dot(cqn, wuq_ref[...])
    for h in range(HEADS):
        lo = h * HEAD_DIM
        qn = q3[:, lo:lo + HEAD_DIM]
        qp = q3[:, hw + lo:hw + lo + HEAD_DIM] * cosv + q3[:, 2 * hw + lo:2 * hw + lo + HEAD_DIM] * sinv
        q_ref[:, 2 * lo:2 * lo + HEAD_DIM] = qn.astype(BF16)
        q_ref[:, 2 * lo + HEAD_DIM:2 * lo + 2 * HEAD_DIM] = qp.astype(BF16)

    ckvn = _rms(ckv, kvn_ref[...]).astype(BF16)
    kv = _dot(ckvn, wukv_ref[...])
    for h in range(HEADS):
        lo = h * HEAD_DIM
        k_ref[:, 2 * lo:2 * lo + HEAD_DIM] = kv[:, lo:lo + HEAD_DIM].astype(BF16)
        k_ref[:, 2 * lo + HEAD_DIM:2 * lo + 2 * HEAD_DIM] = kpe.astype(BF16)
    vt_ref[0] = kv[:, hw:].T.astype(BF16)

    sl = 2 * HEAD_DIM
    for s in range(qkv_w // sl):
        buf_ref[CONV_TAIL:CONV_TAIL + tm, s * sl:(s + 1) * sl] = _dot(u, wbig_ref[:, s * sl:(s + 1) * sl])
        zg = _dot(u, wbig_ref[:, qkv_w + s * sl:qkv_w + (s + 1) * sl]).astype(BF16)
        if s * sl < d:
            z_ref[:, s * sl:(s + 1) * sl] = zg
        else:
            gate_ref[:, s * sl - d:(s + 1) * sl - d] = zg
        for grp in range(2 * s, 2 * s + 2):
            lo = grp * HEAD_DIM
            acc = None
            for j in range(CONV_WIDTH):
                r0 = CONV_TAIL - (CONV_WIDTH - 1) + j
                term = buf_ref[r0:r0 + tm, lo:lo + HEAD_DIM] * convw_ref[j:j + 1, lo:lo + HEAD_DIM]
                acc = term if acc is None else acc + term
            y = _silu(acc)
            if grp < 2 * HEADS:
                y = y * lax.rsqrt(jnp.sum(y * y, axis=-1, keepdims=True) + NORM_EPS)
            if grp < HEADS:
                y = y * (HEAD_DIM ** -0.5)
            qkv_ref[:, lo:lo + HEAD_DIM] = y


def _inproj(x2, cosv, sinv, nw, wbig, wsm, convw, qan, kvn, wuq, wukv, tm, seq):
    t, d = x2.shape
    qkv_w = 3 * HEADS * HEAD_DIM
    hw = HEADS * HEAD_DIM
    row = lambda w: pl.BlockSpec((tm, w), lambda i: (i, 0))
    outs = [(qkv_w, F32), (d, BF16), (2 * d, BF16), (LANES, F32), (2 * hw, BF16), (2 * hw, BF16)]
    return pl.pallas_call(
        functools.partial(_inproj_body, tiles_per_seq=seq // tm),
        grid=(t // tm,),
        in_specs=[row(d), row(LANES), row(LANES), _resident(nw.shape), _resident(wbig.shape), _resident(wsm.shape),
                  _resident(convw.shape), _resident(qan.shape), _resident(kvn.shape), _resident(wuq.shape),
                  _resident(wukv.shape)],
        out_specs=[row(w) for w, _ in outs] + [pl.BlockSpec((1, hw, tm), lambda i: (i, 0, 0))],
        out_shape=[jax.ShapeDtypeStruct((t, w), dt) for w, dt in outs]
        + [jax.ShapeDtypeStruct((t // tm, hw, tm), BF16)],
        scratch_shapes=[pltpu.VMEM((tm + CONV_TAIL, qkv_w), F32)],
        compiler_params=pltpu.CompilerParams(dimension_semantics=("arbitrary",), vmem_limit_bytes=VMEM_LIMIT),
        name="inproj",
    )(x2, cosv, sinv, nw, wbig, wsm, convw, qan, kvn, wuq, wukv)


def _gdn_body(qkv_ref, ab_ref, z_ref, gpar_ref, nw_ref, o_ref, state_ref, *, nc):
    c = CHUNK
    ts = nc * c
    hw = HEADS * HEAD_DIM

    @pl.when(pl.program_id(1) == 0)
    def _():
        state_ref[...] = jnp.zeros_like(state_ref)

    ab = ab_ref[...]
    xg = ab + gpar_ref[0:1, :]
    softplus = jnp.maximum(xg, 0.0) + jnp.log(1.0 + jnp.exp(-jnp.abs(xg)))
    g = -jnp.exp(gpar_ref[1:2, :]) * softplus
    beta_all = _sigmoid(ab)

    rt = lax.broadcasted_iota(jnp.int32, (ts, ts), 0)
    ct = lax.broadcasted_iota(jnp.int32, (ts, ts), 1)
    same = (rt // c) == (ct // c)
    tri_t = jnp.logical_and(same, rt <= ct).astype(BF16)
    g_t = g.T
    g_hi = g_t.astype(BF16)
    rest = g_t - g_hi.astype(F32)
    g_mid = rest.astype(BF16)
    g_lo = (rest - g_mid.astype(F32)).astype(BF16)
    gc_t = _dot(g_hi, tri_t) + _dot(g_mid, tri_t) + _dot(g_lo, tri_t)
    gc_row = gc_t[0:HEADS, :]
    gc_col = gc_t.T

    ri = lax.broadcasted_iota(jnp.int32, (c, 2 * c), 0)
    ci = lax.broadcasted_iota(jnp.int32, (c, 2 * c), 1) % c
    left = lax.broadcasted_iota(jnp.int32, (1, 2 * c), 1) < c
    incl = ri >= ci
    strict = ri > ci
    eye = (ri == ci).astype(F32)
    m16 = ((ri // 16) == (ci // 16)).astype(F32)
    zero_n = jnp.zeros((c, c), BF16)
    zero_w = jnp.zeros((c, HEAD_DIM), BF16)

    def bd_narrow(x):
        xb = x.astype(BF16)
        return jnp.concatenate([jnp.concatenate([xb[:, :c], zero_n], axis=1),
                                jnp.concatenate([zero_n, xb[:, c:]], axis=1)], axis=0)

    def bd_wide(x):
        xb = x.astype(BF16)
        return jnp.concatenate([jnp.concatenate([xb[:, :HEAD_DIM], zero_w], axis=1),
                                jnp.concatenate([zero_w, xb[:, HEAD_DIM:]], axis=1)], axis=0)

    def pmul(x, y):
        return _dot(x.astype(BF16), bd_narrow(y))

    def wide(col_a, col_b):
        return jnp.concatenate([jnp.broadcast_to(col_a, (c, HEAD_DIM)), jnp.broadcast_to(col_b, (c, HEAD_DIM))],
                               axis=1)

    its = [(ch, pr) for ch in range(nc) for pr in range(HEADS // 2)]
    n = range(len(its))
    rows = lambda ch: slice(ch * c, (ch + 1) * c)
    pcols = lambda base, pr: slice(base + 2 * pr * HEAD_DIM, base + (2 * pr + 2) * HEAD_DIM)
    qf = [qkv_ref[rows(ch), pcols(0, pr)] for ch, pr in its]
    kf = [qkv_ref[rows(ch), pcols(hw, pr)] for ch, pr in its]
    v = [qkv_ref[rows(ch), pcols(2 * hw, pr)] for ch, pr in its]
    gca = [gc_col[rows(ch), 2 * pr:2 * pr + 1] for ch, pr in its]
    gcb = [gc_col[rows(ch), 2 * pr + 1:2 * pr + 2] for ch, pr in its]
    gcol_n = [jnp.where(left, gca[i], gcb[i]) for i in n]
    grow_n = [jnp.concatenate([gc_row[2 * pr:2 * pr + 1, rows(ch)], gc_row[2 * pr + 1:2 * pr + 2, rows(ch)]],
                              axis=1) for ch, pr in its]
    decay = [jnp.exp(jnp.where(incl, gcol_n[i] - grow_n[i], -jnp.inf)) for i in n]
    beta_w = [wide(beta_all[rows(ch), HEADS + 2 * pr:HEADS + 2 * pr + 1],
                   beta_all[rows(ch), HEADS + 2 * pr + 1:HEADS + 2 * pr + 2]) for ch, pr in its]
    egc_w = [jnp.exp(wide(gca[i], gcb[i])) for i in n]
    genda = [t[c - 1:c, :] for t in gca]
    gendb = [t[c - 1:c, :] for t in gcb]
    kend_w = [jnp.exp(wide(genda[i] - gca[i], gendb[i] - gcb[i])) for i in n]
    kb = [kf[i] * beta_w[i] for i in n]
    kblk = [bd_wide(t) for t in kf]
    low = [jnp.where(strict, _dot_nt(kb[i].astype(BF16), kblk[i]) * decay[i], 0.0) for i in n]
    qk = [(_dot_nt(qf[i].astype(BF16), kblk[i]) * decay[i]).astype(BF16) for i in n]

    ld = [t * m16 for t in low]
    off = [low[i] - ld[i] for i in n]
    td = [eye - t for t in ld]
    p = [pmul(t, t) for t in ld]
    td = [td[i] + pmul(td[i], p[i]) for i in n]
    p = [pmul(t, t) for t in p]
    td = [td[i] + pmul(td[i], p[i]) for i in n]
    p = [pmul(t, t) for t in p]
    td = [td[i] + pmul(td[i], p[i]) for i in n]
    m = [pmul(td[i], off[i]) for i in n]
    mm = [pmul(t, t) for t in m]
    t1 = [td[i] - pmul(m[i], td[i]) for i in n]
    tinv = [(t1[i] + pmul(mm[i], t1[i])).astype(BF16) for i in n]
    u = [_dot(tinv[i], bd_wide(v[i] * beta_w[i])) for i in n]
    w = [_dot(tinv[i], bd_wide(kb[i] * egc_w[i])) for i in n]
    wq = [jnp.concatenate([w[i], qf[i] * egc_w[i]], axis=0).astype(BF16) for i in n]
    k_end = [(kf[i] * kend_w[i]).astype(BF16) for i in n]

    state = [state_ref[h] for h in range(HEADS)]
    zero_s = jnp.zeros((HEAD_DIM, HEAD_DIM), BF16)
    prs = range(HEADS // 2)
    for ch in range(nc):
        idx = [ch * (HEADS // 2) + pr for pr in prs]
        sblk = [jnp.concatenate([jnp.concatenate([state[2 * pr].astype(BF16), zero_s], axis=1),
                                 jnp.concatenate([zero_s, state[2 * pr + 1].astype(BF16)], axis=1)], axis=0)
                for pr in prs]
        ws_qs = [_dot(wq[i], sblk[pr]) for pr, i in enumerate(idx)]
        vnb = [(u[i] - ws_qs[pr][:c]).astype(BF16) for pr, i in enumerate(idx)]
        o = [ws_qs[pr][c:] + _dot(qk[i], bd_wide(vnb[pr])) for pr, i in enumerate(idx)]
        upd = [_dot_tn(k_end[i], vnb[pr]) for pr, i in enumerate(idx)]
        for pr, i in enumerate(idx):
            state[2 * pr] = state[2 * pr] * jnp.exp(genda[i]) + upd[pr][:HEAD_DIM, :HEAD_DIM]
            state[2 * pr + 1] = state[2 * pr + 1] * jnp.exp(gendb[i]) + upd[pr][HEAD_DIM:, HEAD_DIM:]
        for h in range(HEADS):
            lo = h * HEAD_DIM
            out = _rms(o[h // 2][:, (h % 2) * HEAD_DIM:(h % 2 + 1) * HEAD_DIM], nw_ref[...])
            out = out * _silu(z_ref[rows(ch), lo:lo + HEAD_DIM].astype(F32))
            o_ref[rows(ch), lo:lo + HEAD_DIM] = out.astype(o_ref.dtype)
    for h in range(HEADS):
        state_ref[h] = state[h]


def _gdn(qkv, ab, z, gpar, nw, batch, seq, nc):
    t, qkv_w = qkv.shape
    hw = HEADS * HEAD_DIM
    ts = nc * CHUNK
    assert ts % LANES == 0 and seq % ts == 0
    n = seq // ts
    row = lambda w: pl.BlockSpec((ts, w), lambda b, s: (b * n + s, 0))
    return pl.pallas_call(
        functools.partial(_gdn_body, nc=nc),
        grid=(batch, n),
        in_specs=[row(qkv_w), row(LANES), row(hw), _resident(gpar.shape), _resident(nw.shape)],
        out_specs=row(hw),
        out_shape=jax.ShapeDtypeStruct((t, hw), BF16),
        scratch_shapes=[pltpu.VMEM((HEADS, HEAD_DIM, HEAD_DIM), F32)],
        compiler_params=pltpu.CompilerParams(dimension_semantics=("arbitrary", "arbitrary"),
                                             vmem_limit_bytes=VMEM_LIMIT),
        name="gdn",
    )(qkv, ab, z, gpar, nw)


def _attn_body(q_ref, k_ref, vt_ref, o_ref, *, th, nq):
    qi = pl.program_id(2)
    tv = vt_ref.shape[2]
    ones = jnp.ones((2 * SUBLANES, th), BF16)
    qs = [q_ref[c * th:(c + 1) * th, :] for c in range(nq)]

    def scores(qx, off):
        return _dot_nt(k_ref[pl.ds(off, th), :], qx)

    def update(st, s, off):
        m, l, acc = st
        m_new = jnp.maximum(m, jnp.max(s, axis=0, keepdims=True))
        alpha = jnp.exp2(m - m_new)
        p = jnp.exp2(s - m_new).astype(BF16)
        nk = s.shape[0]
        vt = jnp.concatenate([vt_ref[off // tv + kb] for kb in range(nk // tv)], axis=1)
        pv = _dot(jnp.concatenate([vt, ones[:, :nk]], axis=0), p)
        l = alpha * l + pv[HEAD_DIM:HEAD_DIM + 1, :]
        acc = alpha * acc + pv[:HEAD_DIM, :]
        return m_new, l, acc

    def body(j, sts):
        offs = [pl.multiple_of(j * (2 * th), 2 * th) + t * th for t in range(2)]
        s = [[scores(qx, off) for qx in qs] for off in offs]
        for t, off in enumerate(offs):
            sts = tuple(update(sts[c], s[t][c], off) for c in range(nq))
        return sts

    init = (jnp.full((1, th), -jnp.inf, F32), jnp.zeros((1, th), F32), jnp.zeros((HEAD_DIM, th), F32))
    sts = lax.fori_loop(0, qi * (nq // 2), body, (init,) * nq)

    base = pl.multiple_of(qi * (nq * th), nq * th)
    h2 = th // 2
    ck = lax.broadcasted_iota(jnp.int32, (th, th), 0) // CHUNK
    cq = lax.broadcasted_iota(jnp.int32, (th, th), 1) // CHUNK
    visible = ck <= cq
    sts = list(sts)
    s = {(j, c): scores(qs[c], base + j * th) for j in range(nq) for c in range(j + 1, nq)}
    s_up = [_dot_nt(k_ref[pl.ds(base + c * th, h2), :], qs[c]) for c in range(nq)]
    s_lo = [_dot_nt(k_ref[pl.ds(base + c * th + h2, h2), :], qs[c][h2:, :]) for c in range(nq)]
    for j in range(nq):
        off = base + j * th
        st = update(sts[j], jnp.where(visible[:h2, :], s_up[j], -jnp.inf), off)
        right = update(tuple(x[:, h2:] for x in st), jnp.where(visible[h2:, h2:], s_lo[j], -jnp.inf), off + h2)
        sts[j] = tuple(jnp.concatenate([x[:, :h2], r], axis=1) for x, r in zip(st, right))
        for c in range(j + 1, nq):
            sts[c] = update(sts[c], s[j, c], off)
    for c in range(nq):
        _, l, acc = sts[c]
        o_ref[c * th:(c + 1) * th, :] = (acc / l).T.astype(o_ref.dtype)


def _attn(q, k, vt, batch, seq, th, nq):
    hw = HEADS * HEAD_DIM
    tq = nq * th
    assert nq % 2 == 0 and seq % tq == 0 and (th // 2) % vt.shape[2] == 0
    wide = lambda a: a.reshape(batch, seq, -1)
    tv = vt.shape[2]
    vt4 = vt.reshape(batch, seq // tv, hw, tv)
    out = pl.pallas_call(
        functools.partial(_attn_body, th=th, nq=nq),
        grid=(batch, HEADS, seq // tq),
        in_specs=[pl.BlockSpec((None, tq, 2 * HEAD_DIM), lambda b, h, i: (b, i, h)),
                  pl.BlockSpec((None, seq, 2 * HEAD_DIM), lambda b, h, i: (b, 0, h)),
                  pl.BlockSpec((None, seq // tv, HEAD_DIM, tv), lambda b, h, i: (b, 0, h, 0))],
        out_specs=pl.BlockSpec((None, tq, HEAD_DIM), lambda b, h, i: (b, i, h)),
        out_shape=jax.ShapeDtypeStruct((batch, seq, hw), BF16),
        compiler_params=pltpu.CompilerParams(dimension_semantics=("arbitrary", "arbitrary", "arbitrary"),
                                             vmem_limit_bytes=VMEM_LIMIT),
        name="attn",
    )(wide(q), wide(k), vt4)
    return out.reshape(batch * seq, hw)


def _mlp_body(oa_ref, at_ref, g_ref, x_ref, wa_ref, wb_ref, wo_ref, nmix_ref, n1_ref, w1_ref, w2_ref, n2_ref,
              o_ref, *, fc):
    d = x_ref.shape[1]
    ya = _dot(oa_ref[...], wa_ref[...])
    yb = _dot(at_ref[...], wb_ref[...])
    ga = _sigmoid(g_ref[:, :d].astype(F32))
    gb = _sigmoid(g_ref[:, d:].astype(F32))
    h = ga * ya + gb * yb
    x = x_ref[...] + _rms(_dot(h.astype(BF16), wo_ref[...]), nmix_ref[...])
    u = _rms(x, n1_ref[...]).astype(BF16)
    acc = None
    for c0 in range(0, w1_ref.shape[1], fc):
        hcol = jnp.maximum(_dot(u, w1_ref[:, c0:c0 + fc]), 0.0)
        part = _dot((hcol * hcol).astype(BF16), w2_ref[c0:c0 + fc, :])
        acc = part if acc is None else acc + part
    o_ref[...] = x + _rms(acc, n2_ref[...])


def _mlp(oa, at, gate, x2, wa, wb, wo, nmix, n1, w1, w2, n2, tm):
    t, d = x2.shape
    row = lambda w: pl.BlockSpec((tm, w), lambda i: (i, 0))
    weights = (wa, wb, wo, nmix, n1, w1, w2, n2)
    return pl.pallas_call(
        functools.partial(_mlp_body, fc=min(1024, w1.shape[1])),
        grid=(t // tm,),
        in_specs=[row(oa.shape[1]), row(at.shape[1]), row(2 * d), row(d)] + [_resident(w.shape) for w in weights],
        out_specs=row(d),
        out_shape=jax.ShapeDtypeStruct((t, d), F32),
        compiler_params=pltpu.CompilerParams(dimension_semantics=("arbitrary",), vmem_limit_bytes=VMEM_LIMIT),
        name="mlp",
    )(oa, at, gate, x2, *weights)


def _rope_table():
    inv_freq = ROPE_BASE ** (-jnp.arange(0, MLA_ROPE, 2, dtype=F32) / MLA_ROPE)
    half = MLA_ROPE // 2
    zeros = jnp.zeros((LANES - MLA_ROPE,), F32)
    rows = [jnp.concatenate([inv_freq, inv_freq, zeros]),
            jnp.concatenate([jnp.ones((MLA_ROPE,), F32), zeros]),
            jnp.concatenate([-jnp.ones((half,), F32), jnp.ones((half,), F32), zeros])]
    return jnp.concatenate([jnp.stack(rows), jnp.zeros((5, LANES), F32)], axis=0)


def _pad_cols(w, width):
    return jnp.pad(w, ((0, 0), (0, width - w.shape[1])))


def _layer_weights(w_in, w_uq, w_ukv):
    hw = HEADS * HEAD_DIM
    o = 3 * hw
    qkv, z = w_in[:, :o], w_in[:, o:o + hw]
    o += hw
    a_b = w_in[:, o:o + 2 * HEADS]
    o += 2 * HEADS
    cq = w_in[:, o:o + MLA_Q_LORA]
    o += MLA_Q_LORA
    ckv = w_in[:, o:o + MLA_KV_LORA]
    o += MLA_KV_LORA
    kr = w_in[:, o:o + MLA_ROPE]
    o += MLA_ROPE
    gates = w_in[:, o:]
    half = MLA_ROPE // 2
    krs = jnp.concatenate([kr[:, half:], kr[:, :half]], axis=1)
    wbig = jnp.concatenate([qkv, z, gates], axis=1).astype(BF16)
    wsm = jnp.concatenate([cq, ckv, _pad_cols(kr, LANES), _pad_cols(krs, LANES), _pad_cols(a_b, LANES)],
                          axis=1).astype(BF16)
    uq = w_uq.reshape(MLA_Q_LORA, HEADS, MLA_QK_DIM)
    nope, pe = uq[:, :, :HEAD_DIM], uq[:, :, HEAD_DIM:]
    pes = jnp.concatenate([pe[:, :, half:], pe[:, :, :half]], axis=2)
    padh = lambda w: jnp.pad(w, ((0, 0), (0, 0), (0, HEAD_DIM - MLA_ROPE))).reshape(MLA_Q_LORA, hw)
    q_scale = MLA_QK_DIM ** -0.5 * LOG2_E
    wuq = (jnp.concatenate([nope.reshape(MLA_Q_LORA, hw), padh(pe), padh(pes)], axis=1) * q_scale).astype(BF16)
    ukv = w_ukv.reshape(MLA_KV_LORA, HEADS, 2 * HEAD_DIM)
    wukv = jnp.concatenate([ukv[:, :, :HEAD_DIM].reshape(MLA_KV_LORA, hw),
                            ukv[:, :, HEAD_DIM:].reshape(MLA_KV_LORA, hw)], axis=1).astype(BF16)
    return wbig, wsm, wuq, wukv


def kernel(x, positions, w_in, conv_w, A_log, dt_bias, gdn_norm_w, q_a_norm_w, w_uq, kv_a_norm_w, w_ukv, w_branch_a, w_branch_b, w_out, pre_mix_norm_w, post_mix_norm_w, pre_ffn_norm_w, post_ffn_norm_w, w_ff1, w_ff2):
    batch, seq, d = x.shape
    t = batch * seq
    depth = w_in.shape[0]
    tm_proj = min(256, t)
    tm_mlp = min(512, t)
    th = min(512, seq // 2)
    nq = 4 if seq % (4 * th) == 0 else 2
    x2 = x.reshape(t, d)
    pos2 = positions.astype(F32).reshape(t, 1)
    cosv, sinv = _rope(pos2, _rope_table(), tm_mlp)
    row = lambda v: v.reshape(1, -1)
    for l in range(depth):
        wbig, wsm, wuq, wukv = _layer_weights(w_in[l], w_uq[l], w_ukv[l])
        convw = jnp.pad(conv_w[l], ((0, 8 - CONV_WIDTH), (0, 0)))
        qkv, z, gate, ab, q, kc, va = _inproj(
            x2, cosv, sinv, row(pre_mix_norm_w[l]), wbig, wsm, convw, row(q_a_norm_w[l]), row(kv_a_norm_w[l]),
            wuq, wukv, tm_proj, seq)
        gpar = jnp.zeros((8, LANES), F32).at[0, :HEADS].set(dt_bias[l]).at[1, :HEADS].set(A_log[l])
        oa = _gdn(qkv, ab, z, gpar, row(gdn_norm_w[l]), batch, seq, GDN_CHUNKS_PER_STEP)
        at = _attn(q, kc, va, batch, seq, th, nq)
        x2 = _mlp(oa, at, gate, x2, w_branch_a[l].astype(BF16), w_branch_b[l].astype(BF16),
                  w_out[l].astype(BF16), row(post_mix_norm_w[l]), row(pre_ffn_norm_w[l]),
                  w_ff1[l].astype(BF16), w_ff2[l].astype(BF16), row(post_ffn_norm_w[l]), tm_mlp)
    return x2.reshape(batch, seq, d)
```

```python
import functools

import jax
import jax.numpy as jnp
from jax import lax
from jax.experimental import pallas as pl
from jax.experimental.pallas import tpu as pltpu

F32 = jnp.float32
BF16 = jnp.bfloat16

LANES = 128
SUBLANES = 8
CHUNK = 64
NORM_EPS = 1e-6
HEADS = 8
HEAD_DIM = 128
CONV_WIDTH = 4
CONV_TAIL = 8
MLA_Q_LORA = 384
MLA_KV_LORA = 256
MLA_ROPE = 64
MLA_QK_DIM = HEAD_DIM + MLA_ROPE
ROPE_BASE = 10000.0
LOG2_E = 1.4426950408889634
GDN_CHUNKS_PER_STEP = 4
VMEM_LIMIT = 56 * 1024 * 1024


def _rms(x, w):
    return x * lax.rsqrt(jnp.mean(x * x, axis=-1, keepdims=True) + NORM_EPS) * w


def _sigmoid(x):
    return 0.5 * jnp.tanh(0.5 * x) + 0.5


def _silu(x):
    h = 0.5 * x
    return h * jnp.tanh(h) + h


def _dot(a, b):
    return jnp.dot(a, b, preferred_element_type=F32)


def _dot_nt(a, b):
    return lax.dot_general(a, b, (((1,), (1,)), ((), ())), preferred_element_type=F32)


def _dot_tn(a, b):
    return lax.dot_general(a, b, (((0,), (0,)), ((), ())), preferred_element_type=F32)


def _dotb(a, b):
    return _dot(a.astype(BF16), b.astype(BF16))


def _resident(shape):
    nd = len(shape)
    return pl.BlockSpec(shape, lambda *_: (0,) * nd, pipeline_mode=pl.Buffered(1))


def _layer_resident(shape, layer):
    nd = len(shape)
    return pl.BlockSpec((None,) + tuple(shape[1:]), lambda *_: (layer,) + (0,) * (nd - 1),
                        pipeline_mode=pl.Buffered(1))


def _rope_body(pos_ref, rope_ref, cos_ref, sin_ref):
    ang = pos_ref[...] * rope_ref[0:1, :]
    cos_ref[...] = jnp.cos(ang) * rope_ref[1:2, :]
    sin_ref[...] = jnp.sin(ang) * rope_ref[2:3, :]


def _rope(pos2, rope_tab, tm):
    t = pos2.shape[0]
    out = pl.BlockSpec((tm, LANES), lambda i: (i, 0))
    return pl.pallas_call(
        _rope_body,
        grid=(t // tm,),
        in_specs=[pl.BlockSpec((tm, 1), lambda i: (i, 0)), _resident(rope_tab.shape)],
        out_specs=[out, out],
        out_shape=[jax.ShapeDtypeStruct((t, LANES), F32)] * 2,
        compiler_params=pltpu.CompilerParams(dimension_semantics=("arbitrary",), vmem_limit_bytes=VMEM_LIMIT),
        name="rope",
    )(pos2, rope_tab)


def _inproj_body(x_ref, cos_ref, sin_ref, nw_ref, wbig_ref, wsm_ref, convw_ref, qan_ref, kvn_ref, wuq_ref, wukv_ref,
                 qkv_ref, z_ref, gate_ref, ab_ref, q_ref, k_ref, vt_ref, buf_ref, *, tiles_per_seq):
    tm, d = x_ref.shape
    hw = HEADS * HEAD_DIM
    qkv_w = qkv_ref.shape[1]
    u = _rms(x_ref[...], nw_ref[...]).astype(BF16)

    @pl.when(pl.program_id(0) % tiles_per_seq == 0)
    def _():
        buf_ref[tm:tm + CONV_TAIL, :] = jnp.zeros((CONV_TAIL, qkv_w), F32)

    buf_ref[0:CONV_TAIL, :] = buf_ref[tm:tm + CONV_TAIL, :]

    sm = _dot(u, wsm_ref[...])
    cq = sm[:, :MLA_Q_LORA]
    ckv = sm[:, MLA_Q_LORA:MLA_Q_LORA + MLA_KV_LORA]
    o = MLA_Q_LORA + MLA_KV_LORA
    kr, krs, ab = sm[:, o:o + LANES], sm[:, o + LANES:o + 2 * LANES], sm[:, o + 2 * LANES:]
    ab_ref[...] = ab

    cosv = cos_ref[...]
    sinv = sin_ref[...]
    kpe = kr * cosv + krs * sinv

    cqn = _rms(cq, qan_ref[...]).astype(BF16)
    q3 = _dot(cqn, wuq_ref[...])
    for h in range(HEADS):
        lo = h * HEAD_DIM
        qn = q3[:, lo:lo + HEAD_DIM]
        qp = q3[:, hw + lo:hw + lo + HEAD_DIM] * cosv + q3[:, 2 * hw + lo:2 * hw + lo + HEAD_DIM] * sinv
        q_ref[:, 2 * lo:2 * lo + HEAD_DIM] = qn.astype(BF16)
        q_ref[:, 2 * lo + HEAD_DIM:2 * lo + 2 * HEAD_DIM] = qp.astype(BF16)

    ckvn = _rms(ckv, kvn_ref[...]).astype(BF16)
    kv = _dot(ckvn, wukv_ref[...])
    for h in range(HEADS):
        lo = h * HEAD_DIM
        k_ref[:, 2 * lo:2 * lo + HEAD_DIM] = kv[:, lo:lo + HEAD_DIM].astype(BF16)
        k_ref[:, 2 * lo + HEAD_DIM:2 * lo + 2 * HEAD_DIM] = kpe.astype(BF16)
    vt_ref[0] = kv[:, hw:].T.astype(BF16)

    sl = 2 * HEAD_DIM
    for s in range(qkv_w // sl):
        buf_ref[CONV_TAIL:CONV_TAIL + tm, s * sl:(s + 1) * sl] = _dot(u, wbig_ref[:, s * sl:(s + 1) * sl])
        zg = _dot(u, wbig_ref[:, qkv_w + s * sl:qkv_w + (s + 1) * sl]).astype(BF16)
        if s * sl < d:
            z_ref[:, s * sl:(s + 1) * sl] = zg
        else:
            gate_ref[:, s * sl - d:(s + 1) * sl - d] = zg
        for grp in range(2 * s, 2 * s + 2):
            lo = grp * HEAD_DIM
            acc = None
            for j in range(CONV_WIDTH):
                r0 = CONV_TAIL - (CONV_WIDTH - 1) + j
                term = buf_ref[r0:r0 + tm, lo:lo + HEAD_DIM] * convw_ref[j:j + 1, lo:lo + HEAD_DIM]
                acc = term if acc is None else acc + term
            y = _silu(acc)
            if grp < 2 * HEADS:
                y = y * lax.rsqrt(jnp.sum(y * y, axis=-1, keepdims=True) + NORM_EPS)
            if grp < HEADS:
                y = y * (HEAD_DIM ** -0.5)
            qkv_ref[:, lo:lo + HEAD_DIM] = y


def _inproj(x2, cosv, sinv, weights, layer, tm, seq):
    t, d = x2.shape
    qkv_w = 3 * HEADS * HEAD_DIM
    hw = HEADS * HEAD_DIM
    row = lambda w: pl.BlockSpec((tm, w), lambda i: (i, 0))
    outs = [(qkv_w, F32), (d, BF16), (2 * d, BF16), (LANES, F32), (2 * hw, BF16), (2 * hw, BF16)]
    return pl.pallas_call(
        functools.partial(_inproj_body, tiles_per_seq=seq // tm),
        grid=(t // tm,),
        in_specs=[row(d), row(LANES), row(LANES)] + [_layer_resident(w.shape, layer) for w in weights],
        out_specs=[row(w) for w, _ in outs] + [pl.BlockSpec((1, hw, tm), lambda i: (i, 0, 0))],
        out_shape=[jax.ShapeDtypeStruct((t, w), dt) for w, dt in outs]
        + [jax.ShapeDtypeStruct((t // tm, hw, tm), BF16)],
        scratch_shapes=[pltpu.VMEM((tm + CONV_TAIL, qkv_w), F32)],
        compiler_params=pltpu.CompilerParams(dimension_semantics=("arbitrary",), vmem_limit_bytes=VMEM_LIMIT),
        name="inproj",
    )(x2, cosv, sinv, *weights)


def _gdn_body(qkv_ref, ab_ref, z_ref, gpar_ref, nw_ref, o_ref, state_ref, *, nc):
    c = CHUNK
    ts = nc * c
    hw = HEADS * HEAD_DIM

    @pl.when(pl.program_id(1) == 0)
    def _():
        state_ref[...] = jnp.zeros_like(state_ref)

    ab = ab_ref[...]
    xg = ab + gpar_ref[0:1, :]
    softplus = jnp.maximum(xg, 0.0) + jnp.log(1.0 + jnp.exp(-jnp.abs(xg)))
    g = -jnp.exp(gpar_ref[1:2, :]) * softplus
    beta_all = _sigmoid(ab)

    rt = lax.broadcasted_iota(jnp.int32, (ts, ts), 0)
    ct = lax.broadcasted_iota(jnp.int32, (ts, ts), 1)
    same = (rt // c) == (ct // c)
    tri_t = jnp.logical_and(same, rt <= ct).astype(BF16)
    g_t = g.T
    g_hi = g_t.astype(BF16)
    rest = g_t - g_hi.astype(F32)
    g_mid = rest.astype(BF16)
    g_lo = (rest - g_mid.astype(F32)).astype(BF16)
    gc_t = _dot(g_hi, tri_t) + _dot(g_mid, tri_t) + _dot(g_lo, tri_t)
    gc_row = gc_t[0:HEADS, :]
    gc_col = gc_t.T

    ri = lax.broadcasted_iota(jnp.int32, (c, 2 * c), 0)
    ci = lax.broadcasted_iota(jnp.int32, (c, 2 * c), 1) % c
    left = lax.broadcasted_iota(jnp.int32, (1, 2 * c), 1) < c
    incl = ri >= ci
    strict = ri > ci
    eye = (ri == ci).astype(F32)
    m16 = ((ri // 16) == (ci // 16)).astype(F32)
    zero_n = jnp.zeros((c, c), BF16)
    zero_w = jnp.zeros((c, HEAD_DIM), BF16)

    def bd_narrow(x):
        xb = x.astype(BF16)
        return jnp.concatenate([jnp.concatenate([xb[:, :c], zero_n], axis=1),
                                jnp.concatenate([zero_n, xb[:, c:]], axis=1)], axis=0)

    def bd_wide(x):
        xb = x.astype(BF16)
        return jnp.concatenate([jnp.concatenate([xb[:, :HEAD_DIM], zero_w], axis=1),
                                jnp.concatenate([zero_w, xb[:, HEAD_DIM:]], axis=1)], axis=0)

    def pmul(x, y):
        return _dot(x.astype(BF16), bd_narrow(y))

    def wide(col_a, col_b):
        return jnp.concatenate([jnp.broadcast_to(col_a, (c, HEAD_DIM)), jnp.broadcast_to(col_b, (c, HEAD_DIM))],
                               axis=1)

    its = [(ch, pr) for ch in range(nc) for pr in range(HEADS // 2)]
    n = range(len(its))
    rows = lambda ch: slice(ch * c, (ch + 1) * c)
    pcols = lambda base, pr: slice(base + 2 * pr * HEAD_DIM, base + (2 * pr + 2) * HEAD_DIM)
    qf = [qkv_ref[rows(ch), pcols(0, pr)] for ch, pr in its]
    kf = [qkv_ref[rows(ch), pcols(hw, pr)] for ch, pr in its]
    v = [qkv_ref[rows(ch), pcols(2 * hw, pr)] for ch, pr in its]
    gca = [gc_col[rows(ch), 2 * pr:2 * pr + 1] for ch, pr in its]
    gcb = [gc_col[rows(ch), 2 * pr + 1:2 * pr + 2] for ch, pr in its]
    gcol_n = [jnp.where(left, gca[i], gcb[i]) for i in n]
    grow_n = [jnp.concatenate([gc_row[2 * pr:2 * pr + 1, rows(ch)], gc_row[2 * pr + 1:2 * pr + 2, rows(ch)]],
                              axis=1) for ch, pr in its]
    decay = [jnp.exp(jnp.where(incl, gcol_n[i] - grow_n[i], -jnp.inf)) for i in n]
    beta_w = [wide(beta_all[rows(ch), HEADS + 2 * pr:HEADS + 2 * pr + 1],
                   beta_all[rows(ch), HEADS + 2 * pr + 1:HEADS + 2 * pr + 2]) for ch, pr in its]
    egc_w = [jnp.exp(wide(gca[i], gcb[i])) for i in n]
    genda = [t[c - 1:c, :] for t in gca]
    gendb = [t[c - 1:c, :] for t in gcb]
    kend_w = [jnp.exp(wide(genda[i] - gca[i], gendb[i] - gcb[i])) for i in n]
    kb = [kf[i] * beta_w[i] for i in n]
    kblk = [bd_wide(t) for t in kf]
    low = [jnp.where(strict, _dot_nt(kb[i].astype(BF16), kblk[i]) * decay[i], 0.0) for i in n]
    qk = [(_dot_nt(qf[i].astype(BF16), kblk[i]) * decay[i]).astype(BF16) for i in n]

    ld = [t * m16 for t in low]
    off = [low[i] - ld[i] for i in n]
    td = [eye - t for t in ld]
    p = [pmul(t, t) for t in ld]
    td = [td[i] + pmul(td[i], p[i]) for i in n]
    p = [pmul(t, t) for t in p]
    td = [td[i] + pmul(td[i], p[i]) for i in n]
    p = [pmul(t, t) for t in p]
    td = [td[i] + pmul(td[i], p[i]) for i in n]
    m = [pmul(td[i], off[i]) for i in n]
    mm = [pmul(t, t) for t in m]
    t1 = [td[i] - pmul(m[i], td[i]) for i in n]
    tinv = [(t1[i] + pmul(mm[i], t1[i])).astype(BF16) for i in n]
    u = [_dot(tinv[i], bd_wide(v[i] * beta_w[i])) for i in n]
    w = [_dot(tinv[i], bd_wide(kb[i] * egc_w[i])) for i in n]
    wq = [jnp.concatenate([w[i], qf[i] * egc_w[i]], axis=0).astype(BF16) for i in n]
    k_end = [(kf[i] * kend_w[i]).astype(BF16) for i in n]

    state = [state_ref[h] for h in range(HEADS)]
    zero_s = jnp.zeros((HEAD_DIM, HEAD_DIM), BF16)
    prs = range(HEADS // 2)
    for ch in range(nc):
        idx = [ch * (HEADS // 2) + pr for pr in prs]
        sblk = [jnp.concatenate([jnp.concatenate([state[2 * pr].astype(BF16), zero_s], axis=1),
                                 jnp.concatenate([zero_s, state[2 * pr + 1].astype(BF16)], axis=1)], axis=0)
                for pr in prs]
        ws_qs = [_dot(wq[i], sblk[pr]) for pr, i in enumerate(idx)]
        vnb = [(u[i] - ws_qs[pr][:c]).astype(BF16) for pr, i in enumerate(idx)]
        o = [ws_qs[pr][c:] + _dot(qk[i], bd_wide(vnb[pr])) for pr, i in enumerate(idx)]
        upd = [_dot_tn(k_end[i], vnb[pr]) for pr, i in enumerate(idx)]
        for pr, i in enumerate(idx):
            state[2 * pr] = state[2 * pr] * jnp.exp(genda[i]) + upd[pr][:HEAD_DIM, :HEAD_DIM]
            state[2 * pr + 1] = state[2 * pr + 1] * jnp.exp(gendb[i]) + upd[pr][HEAD_DIM:, HEAD_DIM:]
        for h in range(HEADS):
            lo = h * HEAD_DIM
            out = _rms(o[h // 2][:, (h % 2) * HEAD_DIM:(h % 2 + 1) * HEAD_DIM], nw_ref[...])
            out = out * _silu(z_ref[rows(ch), lo:lo + HEAD_DIM].astype(F32))
            o_ref[rows(ch), lo:lo + HEAD_DIM] = out.astype(o_ref.dtype)
    for h in range(HEADS):
        state_ref[h] = state[h]


def _gdn(qkv, ab, z, gpar, nw, layer, batch, seq, nc):
    t, qkv_w = qkv.shape
    hw = HEADS * HEAD_DIM
    ts = nc * CHUNK
    assert ts % LANES == 0 and seq % ts == 0
    n = seq // ts
    row = lambda w: pl.BlockSpec((ts, w), lambda b, s: (b * n + s, 0))
    return pl.pallas_call(
        functools.partial(_gdn_body, nc=nc),
        grid=(batch, n),
        in_specs=[row(qkv_w), row(LANES), row(hw), _layer_resident(gpar.shape, layer), _layer_resident(nw.shape, layer)],
        out_specs=row(hw),
        out_shape=jax.ShapeDtypeStruct((t, hw), BF16),
        scratch_shapes=[pltpu.VMEM((HEADS, HEAD_DIM, HEAD_DIM), F32)],
        compiler_params=pltpu.CompilerParams(dimension_semantics=("arbitrary", "arbitrary"),
                                             vmem_limit_bytes=VMEM_LIMIT),
        name="gdn",
    )(qkv, ab, z, gpar, nw)


def _attn_body(q_ref, k_ref, vt_ref, o_ref, *, th, nq):
    qi = pl.program_id(2)
    tv = vt_ref.shape[2]
    ones = jnp.ones((2 * SUBLANES, th), BF16)
    qs = [q_ref[c * th:(c + 1) * th, :] for c in range(nq)]

    def scores(qx, off):
        return _dot_nt(k_ref[pl.ds(off, th), :], qx)

    def update(st, s, off):
        m, l, acc = st
        m_new = jnp.maximum(m, jnp.max(s, axis=0, keepdims=True))
        alpha = jnp.exp2(m - m_new)
        p = jnp.exp2(s - m_new).astype(BF16)
        nk = s.shape[0]
        vt = jnp.concatenate([vt_ref[off // tv + kb] for kb in range(nk // tv)], axis=1)
        pv = _dot(jnp.concatenate([vt, ones[:, :nk]], axis=0), p)
        l = alpha * l + pv[HEAD_DIM:HEAD_DIM + 1, :]
        acc = alpha * acc + pv[:HEAD_DIM, :]
        return m_new, l, acc

    def body(j, sts):
        offs = [pl.multiple_of(j * (2 * th), 2 * th) + t * th for t in range(2)]
        s = [[scores(qx, off) for qx in qs] for off in offs]
        for t, off in enumerate(offs):
            sts = tuple(update(sts[c], s[t][c], off) for c in range(nq))
        return sts

    init = (jnp.full((1, th), -jnp.inf, F32), jnp.zeros((1, th), F32), jnp.zeros((HEAD_DIM, th), F32))
    sts = lax.fori_loop(0, qi * (nq // 2), body, (init,) * nq)

    base = pl.multiple_of(qi * (nq * th), nq * th)
    h2 = th // 2
    ck = lax.broadcasted_iota(jnp.int32, (th, th), 0) // CHUNK
    cq = lax.broadcasted_iota(jnp.int32, (th, th), 1) // CHUNK
    visible = ck <= cq
    sts = list(sts)
    s = {(j, c): scores(qs[c], base + j * th) for j in range(nq) for c in range(j + 1, nq)}
    s_up = [_dot_nt(k_ref[pl.ds(base + c * th, h2), :], qs[c]) for c in range(nq)]
    s_lo = [_dot_nt(k_ref[pl.ds(base + c * th + h2, h2), :], qs[c][h2:, :]) for c in range(nq)]
    for j in range(nq):
        off = base + j * th
        st = update(sts[j], jnp.where(visible[:h2, :], s_up[j], -jnp.inf), off)
        right = update(tuple(x[:, h2:] for x in st), jnp.where(visible[h2:, h2:], s_lo[j], -jnp.inf), off + h2)
        sts[j] = tuple(jnp.concatenate([x[:, :h2], r], axis=1) for x, r in zip(st, right))
        for c in range(j + 1, nq):
            sts[c] = update(sts[c], s[j, c], off)
    for c in range(nq):
        _, l, acc = sts[c]
        o_ref[c * th:(c + 1) * th, :] = (acc / l).T.astype(o_ref.dtype)


def _attn(q, k, vt, batch, seq, th, nq):
    hw = HEADS * HEAD_DIM
    tq = nq * th
    assert nq % 2 == 0 and seq % tq == 0 and (th // 2) % vt.shape[2] == 0
    wide = lambda a: a.reshape(batch, seq, -1)
    tv = vt.shape[2]
    vt4 = vt.reshape(batch, seq // tv, hw, tv)
    out = pl.pallas_call(
        functools.partial(_attn_body, th=th, nq=nq),
        grid=(batch, HEADS, seq // tq),
        in_specs=[pl.BlockSpec((None, tq, 2 * HEAD_DIM), lambda b, h, i: (b, i, h)),
                  pl.BlockSpec((None, seq, 2 * HEAD_DIM), lambda b, h, i: (b, 0, h)),
                  pl.BlockSpec((None, seq // tv, HEAD_DIM, tv), lambda b, h, i: (b, 0, h, 0))],
        out_specs=pl.BlockSpec((None, tq, HEAD_DIM), lambda b, h, i: (b, i, h)),
        out_shape=jax.ShapeDtypeStruct((batch, seq, hw), BF16),
        compiler_params=pltpu.CompilerParams(dimension_semantics=("arbitrary", "arbitrary", "arbitrary"),
                                             vmem_limit_bytes=VMEM_LIMIT),
        name="attn",
    )(wide(q), wide(k), vt4)
    return out.reshape(batch * seq, hw)


def _mlp_body(oa_ref, at_ref, g_ref, x_ref, wa_ref, wb_ref, wo_ref, nmix_ref, n1_ref, w1_ref, w2_ref, n2_ref,
              o_ref, *, fc):
    d = x_ref.shape[1]
    ya = _dot(oa_ref[...], wa_ref[...])
    yb = _dot(at_ref[...], wb_ref[...])
    ga = _sigmoid(g_ref[:, :d].astype(F32))
    gb = _sigmoid(g_ref[:, d:].astype(F32))
    h = ga * ya + gb * yb
    x = x_ref[...] + _rms(_dot(h.astype(BF16), wo_ref[...]), nmix_ref[...])
    u = _rms(x, n1_ref[...]).astype(BF16)
    acc = None
    for c0 in range(0, w1_ref.shape[1], fc):
        hcol = jnp.maximum(_dot(u, w1_ref[:, c0:c0 + fc]), 0.0)
        part = _dot((hcol * hcol).astype(BF16), w2_ref[c0:c0 + fc, :])
        acc = part if acc is None else acc + part
    o_ref[...] = x + _rms(acc, n2_ref[...])


def _mlp(oa, at, gate, x2, weights, layer, tm):
    t, d = x2.shape
    row = lambda w: pl.BlockSpec((tm, w), lambda i: (i, 0))
    return pl.pallas_call(
        functools.partial(_mlp_body, fc=min(1024, weights[5].shape[2])),
        grid=(t // tm,),
        in_specs=[row(oa.shape[1]), row(at.shape[1]), row(2 * d), row(d)]
        + [_layer_resident(w.shape, layer) for w in weights],
        out_specs=row(d),
        out_shape=jax.ShapeDtypeStruct((t, d), F32),
        compiler_params=pltpu.CompilerParams(dimension_semantics=("arbitrary",), vmem_limit_bytes=VMEM_LIMIT),
        name="mlp",
    )(oa, at, gate, x2, *weights)


def _rope_table():
    inv_freq = ROPE_BASE ** (-jnp.arange(0, MLA_ROPE, 2, dtype=F32) / MLA_ROPE)
    half = MLA_ROPE // 2
    zeros = jnp.zeros((LANES - MLA_ROPE,), F32)
    rows = [jnp.concatenate([inv_freq, inv_freq, zeros]),
            jnp.concatenate([jnp.ones((MLA_ROPE,), F32), zeros]),
            jnp.concatenate([-jnp.ones((half,), F32), jnp.ones((half,), F32), zeros])]
    return jnp.concatenate([jnp.stack(rows), jnp.zeros((5, LANES), F32)], axis=0)


def _layer_weights(w_in, w_uq, w_ukv):
    depth = w_in.shape[0]
    hw = HEADS * HEAD_DIM
    o = 3 * hw
    qkv, z = w_in[..., :o], w_in[..., o:o + hw]
    o += hw
    a_b = w_in[..., o:o + 2 * HEADS]
    o += 2 * HEADS
    cq = w_in[..., o:o + MLA_Q_LORA]
    o += MLA_Q_LORA
    ckv = w_in[..., o:o + MLA_KV_LORA]
    o += MLA_KV_LORA
    kr = w_in[..., o:o + MLA_ROPE]
    o += MLA_ROPE
    gates = w_in[..., o:]
    half = MLA_ROPE // 2
    lane_pad = lambda w: jnp.pad(w, ((0, 0), (0, 0), (0, LANES - w.shape[-1])))
    krs = jnp.concatenate([kr[..., half:], kr[..., :half]], axis=-1)
    wbig = jnp.concatenate([qkv, z, gates], axis=-1).astype(BF16)
    wsm = jnp.concatenate([cq, ckv, lane_pad(kr), lane_pad(krs), lane_pad(a_b)], axis=-1).astype(BF16)
    uq = w_uq.reshape(depth, MLA_Q_LORA, HEADS, MLA_QK_DIM)
    nope, pe = uq[..., :HEAD_DIM], uq[..., HEAD_DIM:]
    pes = jnp.concatenate([pe[..., half:], pe[..., :half]], axis=-1)
    flat = lambda w: w.reshape(depth, w.shape[1], hw)
    padh = lambda w: flat(jnp.pad(w, ((0, 0), (0, 0), (0, 0), (0, HEAD_DIM - MLA_ROPE))))
    q_scale = MLA_QK_DIM ** -0.5 * LOG2_E
    wuq = (jnp.concatenate([flat(nope), padh(pe), padh(pes)], axis=-1) * q_scale).astype(BF16)
    ukv = w_ukv.reshape(depth, MLA_KV_LORA, HEADS, 2 * HEAD_DIM)
    wukv = jnp.concatenate([flat(ukv[..., :HEAD_DIM]), flat(ukv[..., HEAD_DIM:])], axis=-1).astype(BF16)
    return wbig, wsm, wuq, wukv


def kernel(x, positions, w_in, conv_w, A_log, dt_bias, gdn_norm_w, q_a_norm_w, w_uq, kv_a_norm_w, w_ukv, w_branch_a, w_branch_b, w_out, pre_mix_norm_w, post_mix_norm_w, pre_ffn_norm_w, post_ffn_norm_w, w_ff1, w_ff2):
    batch, seq, d = x.shape
    t = batch * seq
    depth = w_in.shape[0]
    tm_proj = min(256, t)
    tm_mlp = min(512, t)
    th = min(512, seq // 2)
    nq = 4 if seq % (4 * th) == 0 else 2
    x2 = x.reshape(t, d)
    pos2 = positions.astype(F32).reshape(t, 1)
    cosv, sinv = _rope(pos2, _rope_table(), tm_mlp)
    rows = lambda v: v.reshape(depth, 1, -1)
    wbig, wsm, wuq, wukv = _layer_weights(w_in, w_uq, w_ukv)
    convw = jnp.pad(conv_w, ((0, 0), (0, 8 - CONV_WIDTH), (0, 0)))
    proj_weights = (rows(pre_mix_norm_w), wbig, wsm, convw, rows(q_a_norm_w), rows(kv_a_norm_w), wuq, wukv)
    gpar = jnp.zeros((depth, 8, LANES), F32).at[:, 0, :HEADS].set(dt_bias).at[:, 1, :HEADS].set(A_log)
    gdn_nw = rows(gdn_norm_w)
    mlp_weights = (w_branch_a.astype(BF16), w_branch_b.astype(BF16), w_out.astype(BF16), rows(post_mix_norm_w),
                   rows(pre_ffn_norm_w), w_ff1.astype(BF16), w_ff2.astype(BF16), rows(post_ffn_norm_w))
    for l in range(depth):
        qkv, z, gate, ab, q, kc, va = _inproj(x2, cosv, sinv, proj_weights, l, tm_proj, seq)
        oa = _gdn(qkv, ab, z, gpar, gdn_nw, l, batch, seq, GDN_CHUNKS_PER_STEP)
        at = _attn(q, kc, va, batch, seq, th, nq)
        x2 = _mlp(oa, at, gate, x2, mlp_weights, l, tm_mlp)
    return x2.reshape(batch, seq, d)
```

```python
import functools

import jax
import jax.numpy as jnp
from jax import lax
from jax.experimental import pallas as pl
from jax.experimental.pallas import tpu as pltpu

F32 = jnp.float32
BF16 = jnp.bfloat16

LANES = 128
SUBLANES = 8
CHUNK = 64
NORM_EPS = 1e-6
HEADS = 8
HEAD_DIM = 128
CONV_WIDTH = 4
CONV_TAIL = 8
MLA_Q_LORA = 384
MLA_KV_LORA = 256
MLA_ROPE = 64
MLA_QK_DIM = HEAD_DIM + MLA_ROPE
ROPE_BASE = 10000.0
LOG2_E = 1.4426950408889634
GDN_CHUNKS_PER_STEP = 4
VMEM_LIMIT = 56 * 1024 * 1024


def _rms(x, w):
    return x * lax.rsqrt(jnp.mean(x * x, axis=-1, keepdims=True) + NORM_EPS) * w


def _sigmoid(x):
    return 0.5 * jnp.tanh(0.5 * x) + 0.5


def _silu(x):
    h = 0.5 * x
    return h * jnp.tanh(h) + h


def _dot(a, b):
    return jnp.dot(a, b, preferred_element_type=F32)


def _dot_nt(a, b):
    return lax.dot_general(a, b, (((1,), (1,)), ((), ())), preferred_element_type=F32)


def _dot_tn(a, b):
    return lax.dot_general(a, b, (((0,), (0,)), ((), ())), preferred_element_type=F32)


def _resident(shape):
    nd = len(shape)
    return pl.BlockSpec(shape, lambda *_: (0,) * nd, pipeline_mode=pl.Buffered(1))


def _layer_resident(shape, layer):
    nd = len(shape)
    return pl.BlockSpec((None,) + tuple(shape[1:]), lambda *_: (layer,) + (0,) * (nd - 1),
                        pipeline_mode=pl.Buffered(1))


def _rope_body(pos_ref, rope_ref, cos_ref, sin_ref):
    ang = pos_ref[...] * rope_ref[0:1, :]
    cos_ref[...] = jnp.cos(ang) * rope_ref[1:2, :]
    sin_ref[...] = jnp.sin(ang) * rope_ref[2:3, :]


def _rope(pos2, rope_tab, tm):
    t = pos2.shape[0]
    out = pl.BlockSpec((tm, LANES), lambda i: (i, 0))
    return pl.pallas_call(
        _rope_body,
        grid=(t // tm,),
        in_specs=[pl.BlockSpec((tm, 1), lambda i: (i, 0)), _resident(rope_tab.shape)],
        out_specs=[out, out],
        out_shape=[jax.ShapeDtypeStruct((t, LANES), F32)] * 2,
        compiler_params=pltpu.CompilerParams(dimension_semantics=("arbitrary",), vmem_limit_bytes=VMEM_LIMIT),
        name="rope",
    )(pos2, rope_tab)


def _inproj_body(x_ref, cos_ref, sin_ref, nw_ref, wbig_ref, wsm_ref, convw_ref, qan_ref, kvn_ref, wuq_ref, wukv_ref,
                 qkv_ref, z_ref, gate_ref, ab_ref, q_ref, k_ref, vt_ref, buf_ref, *, tiles_per_seq):
    tm, d = x_ref.shape
    hw = HEADS * HEAD_DIM
    qkv_w = qkv_ref.shape[1]
    u = _rms(x_ref[...], nw_ref[...]).astype(BF16)

    @pl.when(pl.program_id(0) % tiles_per_seq == 0)
    def _():
        buf_ref[tm:tm + CONV_TAIL, :] = jnp.zeros((CONV_TAIL, qkv_w), F32)

    buf_ref[0:CONV_TAIL, :] = buf_ref[tm:tm + CONV_TAIL, :]

    sm = _dot(u, wsm_ref[...])
    cq = sm[:, :MLA_Q_LORA]
    ckv = sm[:, MLA_Q_LORA:MLA_Q_LORA + MLA_KV_LORA]
    o = MLA_Q_LORA + MLA_KV_LORA
    kr, krs, ab = sm[:, o:o + LANES], sm[:, o + LANES:o + 2 * LANES], sm[:, o + 2 * LANES:]
    ab_ref[...] = ab

    cosv = cos_ref[...]
    sinv = sin_ref[...]
    kpe = kr * cosv + krs * sinv

    cqn = _rms(cq, qan_ref[...]).astype(BF16)
    q3 = _dot(cqn, wuq_ref[...])
    for h in range(HEADS):
        lo = h * HEAD_DIM
        qn = q3[:, lo:lo + HEAD_DIM]
        qp = q3[:, hw + lo:hw + lo + HEAD_DIM] * cosv + q3[:, 2 * hw + lo:2 * hw + lo + HEAD_DIM] * sinv
        q_ref[:, 2 * lo:2 * lo + HEAD_DIM] = qn.astype(BF16)
        q_ref[:, 2 * lo + HEAD_DIM:2 * lo + 2 * HEAD_DIM] = qp.astype(BF16)

    ckvn = _rms(ckv, kvn_ref[...]).astype(BF16)
    kv = _dot(ckvn, wukv_ref[...])
    for h in range(HEADS):
        lo = h * HEAD_DIM
        k_ref[:, 2 * lo:2 * lo + HEAD_DIM] = kv[:, lo:lo + HEAD_DIM].astype(BF16)
        k_ref[:, 2 * lo + HEAD_DIM:2 * lo + 2 * HEAD_DIM] = kpe.astype(BF16)
    vt_ref[0] = kv[:, hw:].T.astype(BF16)

    sl = 2 * HEAD_DIM
    for s in range(qkv_w // sl):
        buf_ref[CONV_TAIL:CONV_TAIL + tm, s * sl:(s + 1) * sl] = _dot(u, wbig_ref[:, s * sl:(s + 1) * sl])
        zg = _dot(u, wbig_ref[:, qkv_w + s * sl:qkv_w + (s + 1) * sl]).astype(BF16)
        if s * sl < d:
            z_ref[:, s * sl:(s + 1) * sl] = zg
        else:
            gate_ref[:, s * sl - d:(s + 1) * sl - d] = zg
        for grp in range(2 * s, 2 * s + 2):
            lo = grp * HEAD_DIM
            acc = None
            for j in range(CONV_WIDTH):
                r0 = CONV_TAIL - (CONV_WIDTH - 1) + j
                term = buf_ref[r0:r0 + tm, lo:lo + HEAD_DIM] * convw_ref[j:j + 1, lo:lo + HEAD_DIM]
                acc = term if acc is None else acc + term
            y = _silu(acc)
            if grp < 2 * HEADS:
                y = y * lax.rsqrt(jnp.sum(y * y, axis=-1, keepdims=True) + NORM_EPS)
            if grp < HEADS:
                y = y * (HEAD_DIM ** -0.5)
            qkv_ref[:, lo:lo + HEAD_DIM] = y


def _inproj(x2, cosv, sinv, weights, layer, tm, seq):
    t, d = x2.shape
    assert seq % tm == 0 and d == HEADS * HEAD_DIM
    qkv_w = 3 * HEADS * HEAD_DIM
    hw = HEADS * HEAD_DIM
    row = lambda w: pl.BlockSpec((tm, w), lambda i: (i, 0))
    outs = [(qkv_w, F32), (d, BF16), (2 * d, BF16), (LANES, F32), (2 * hw, BF16), (2 * hw, BF16)]
    return pl.pallas_call(
        functools.partial(_inproj_body, tiles_per_seq=seq // tm),
        grid=(t // tm,),
        in_specs=[row(d), row(LANES), row(LANES)] + [_layer_resident(w.shape, layer) for w in weights],
        out_specs=[row(w) for w, _ in outs] + [pl.BlockSpec((1, hw, tm), lambda i: (i, 0, 0))],
        out_shape=[jax.ShapeDtypeStruct((t, w), dt) for w, dt in outs]
        + [jax.ShapeDtypeStruct((t // tm, hw, tm), BF16)],
        scratch_shapes=[pltpu.VMEM((tm + CONV_TAIL, qkv_w), F32)],
        compiler_params=pltpu.CompilerParams(dimension_semantics=("arbitrary",), vmem_limit_bytes=VMEM_LIMIT),
        name="inproj",
    )(x2, cosv, sinv, *weights)


def _gdn_body(qkv_ref, ab_ref, z_ref, gpar_ref, nw_ref, o_ref, state_ref, *, nc):
    c = CHUNK
    ts = nc * c
    hw = HEADS * HEAD_DIM

    @pl.when(pl.program_id(1) == 0)
    def _():
        state_ref[...] = jnp.zeros_like(state_ref)

    ab = ab_ref[...]
    xg = ab + gpar_ref[0:1, :]
    softplus = jnp.maximum(xg, 0.0) + jnp.log(1.0 + jnp.exp(-jnp.abs(xg)))
    g = -jnp.exp(gpar_ref[1:2, :]) * softplus
    beta_all = _sigmoid(ab)

    rt = lax.broadcasted_iota(jnp.int32, (ts, ts), 0)
    ct = lax.broadcasted_iota(jnp.int32, (ts, ts), 1)
    same = (rt // c) == (ct // c)
    tri_t = jnp.logical_and(same, rt <= ct).astype(BF16)
    g_t = g.T
    g_hi = g_t.astype(BF16)
    rest = g_t - g_hi.astype(F32)
    g_mid = rest.astype(BF16)
    g_lo = (rest - g_mid.astype(F32)).astype(BF16)
    gc_t = _dot(g_hi, tri_t) + _dot(g_mid, tri_t) + _dot(g_lo, tri_t)
    gc_row = gc_t[0:HEADS, :]
    gc_col = gc_t.T

    ri = lax.broadcasted_iota(jnp.int32, (c, 2 * c), 0)
    ci = lax.broadcasted_iota(jnp.int32, (c, 2 * c), 1) % c
    left = lax.broadcasted_iota(jnp.int32, (1, 2 * c), 1) < c
    incl = ri >= ci
    strict = ri > ci
    eye = (ri == ci).astype(F32)
    m16 = ((ri // 16) == (ci // 16)).astype(F32)
    zero_n = jnp.zeros((c, c), BF16)
    zero_w = jnp.zeros((c, HEAD_DIM), BF16)

    def bd_narrow(x):
        xb = x.astype(BF16)
        return jnp.concatenate([jnp.concatenate([xb[:, :c], zero_n], axis=1),
                                jnp.concatenate([zero_n, xb[:, c:]], axis=1)], axis=0)

    def bd_wide(x):
        xb = x.astype(BF16)
        return jnp.concatenate([jnp.concatenate([xb[:, :HEAD_DIM], zero_w], axis=1),
                                jnp.concatenate([zero_w, xb[:, HEAD_DIM:]], axis=1)], axis=0)

    def pmul(x, y):
        return _dot(x.astype(BF16), bd_narrow(y))

    def wide(col_a, col_b):
        return jnp.concatenate([jnp.broadcast_to(col_a, (c, HEAD_DIM)), jnp.broadcast_to(col_b, (c, HEAD_DIM))],
                               axis=1)

    its = [(ch, pr) for ch in range(nc) for pr in range(HEADS // 2)]
    n = range(len(its))
    rows = lambda ch: slice(ch * c, (ch + 1) * c)
    pcols = lambda base, pr: slice(base + 2 * pr * HEAD_DIM, base + (2 * pr + 2) * HEAD_DIM)
    qf = [qkv_ref[rows(ch), pcols(0, pr)] for ch, pr in its]
    kf = [qkv_ref[rows(ch), pcols(hw, pr)] for ch, pr in its]
    v = [qkv_ref[rows(ch), pcols(2 * hw, pr)] for ch, pr in its]
    gca = [gc_col[rows(ch), 2 * pr:2 * pr + 1] for ch, pr in its]
    gcb = [gc_col[rows(ch), 2 * pr + 1:2 * pr + 2] for ch, pr in its]
    gcol_n = [jnp.where(left, gca[i], gcb[i]) for i in n]
    grow_n = [jnp.concatenate([gc_row[2 * pr:2 * pr + 1, rows(ch)], gc_row[2 * pr + 1:2 * pr + 2, rows(ch)]],
                              axis=1) for ch, pr in its]
    decay = [jnp.exp(jnp.where(incl, gcol_n[i] - grow_n[i], -jnp.inf)) for i in n]
    beta_w = [wide(beta_all[rows(ch), HEADS + 2 * pr:HEADS + 2 * pr + 1],
                   beta_all[rows(ch), HEADS + 2 * pr + 1:HEADS + 2 * pr + 2]) for ch, pr in its]
    egc_w = [jnp.exp(wide(gca[i], gcb[i])) for i in n]
    genda = [t[c - 1:c, :] for t in gca]
    gendb = [t[c - 1:c, :] for t in gcb]
    kend_w = [jnp.exp(wide(genda[i] - gca[i], gendb[i] - gcb[i])) for i in n]
    kb = [kf[i] * beta_w[i] for i in n]
    kblk = [bd_wide(t) for t in kf]
    low = [jnp.where(strict, _dot_nt(kb[i].astype(BF16), kblk[i]) * decay[i], 0.0) for i in n]
    qk = [(_dot_nt(qf[i].astype(BF16), kblk[i]) * decay[i]).astype(BF16) for i in n]

    ld = [t * m16 for t in low]
    off = [low[i] - ld[i] for i in n]
    td = [eye - t for t in ld]
    p = [pmul(t, t) for t in ld]
    td = [td[i] + pmul(td[i], p[i]) for i in n]
    p = [pmul(t, t) for t in p]
    td = [td[i] + pmul(td[i], p[i]) for i in n]
    p = [pmul(t, t) for t in p]
    td = [td[i] + pmul(td[i], p[i]) for i in n]
    m = [pmul(td[i], off[i]) for i in n]
    mm = [pmul(t, t) for t in m]
    t1 = [td[i] - pmul(m[i], td[i]) for i in n]
    tinv = [(t1[i] + pmul(mm[i], t1[i])).astype(BF16) for i in n]
    u = [_dot(tinv[i], bd_wide(v[i] * beta_w[i])) for i in n]
    w = [_dot(tinv[i], bd_wide(kb[i] * egc_w[i])) for i in n]
    wq = [jnp.concatenate([w[i], qf[i] * egc_w[i]], axis=0).astype(BF16) for i in n]
    k_end = [(kf[i] * kend_w[i]).astype(BF16) for i in n]

    state = [state_ref[h] for h in range(HEADS)]
    zero_s = jnp.zeros((HEAD_DIM, HEAD_DIM), BF16)
    prs = range(HEADS // 2)
    for ch in range(nc):
        idx = [ch * (HEADS // 2) + pr for pr in prs]
        sblk = [jnp.concatenate([jnp.concatenate([state[2 * pr].astype(BF16), zero_s], axis=1),
                                 jnp.concatenate([zero_s, state[2 * pr + 1].astype(BF16)], axis=1)], axis=0)
                for pr in prs]
        ws_qs = [_dot(wq[i], sblk[pr]) for pr, i in enumerate(idx)]
        vnb = [(u[i] - ws_qs[pr][:c]).astype(BF16) for pr, i in enumerate(idx)]
        o = [ws_qs[pr][c:] + _dot(qk[i], bd_wide(vnb[pr])) for pr, i in enumerate(idx)]
        upd = [_dot_tn(k_end[i], vnb[pr]) for pr, i in enumerate(idx)]
        for pr, i in enumerate(idx):
            state[2 * pr] = state[2 * pr] * jnp.exp(genda[i]) + upd[pr][:HEAD_DIM, :HEAD_DIM]
            state[2 * pr + 1] = state[2 * pr + 1] * jnp.exp(gendb[i]) + upd[pr][HEAD_DIM:, HEAD_DIM:]
        for h in range(HEADS):
            lo = h * HEAD_DIM
            out = _rms(o[h // 2][:, (h % 2) * HEAD_DIM:(h % 2 + 1) * HEAD_DIM], nw_ref[...])
            out = out * _silu(z_ref[rows(ch), lo:lo + HEAD_DIM].astype(F32))
            o_ref[rows(ch), lo:lo + HEAD_DIM] = out.astype(o_ref.dtype)
    for h in range(HEADS):
        state_ref[h] = state[h]


def _gdn(qkv, ab, z, gpar, nw, layer, batch, seq, nc):
    t, qkv_w = qkv.shape
    hw = HEADS * HEAD_DIM
    ts = nc * CHUNK
    assert ts % LANES == 0 and seq % ts == 0
    n = seq // ts
    row = lambda w: pl.BlockSpec((ts, w), lambda b, s: (b * n + s, 0))
    return pl.pallas_call(
        functools.partial(_gdn_body, nc=nc),
        grid=(batch, n),
        in_specs=[row(qkv_w), row(LANES), row(hw), _layer_resident(gpar.shape, layer), _layer_resident(nw.shape, layer)],
        out_specs=row(hw),
        out_shape=jax.ShapeDtypeStruct((t, hw), BF16),
        scratch_shapes=[pltpu.VMEM((HEADS, HEAD_DIM, HEAD_DIM), F32)],
        compiler_params=pltpu.CompilerParams(dimension_semantics=("arbitrary", "arbitrary"),
                                             vmem_limit_bytes=VMEM_LIMIT),
        name="gdn",
    )(qkv, ab, z, gpar, nw)


def _attn_body(q_ref, k_ref, vt_ref, o_ref, *, th, nq):
    qi = pl.program_id(2)
    tv = vt_ref.shape[2]
    ones = jnp.ones((2 * SUBLANES, th), BF16)
    qs = [q_ref[c * th:(c + 1) * th, :] for c in range(nq)]

    def scores(qx, off):
        return _dot_nt(k_ref[pl.ds(off, th), :], qx)

    def update(st, s, off):
        m, l, acc = st
        m_new = jnp.maximum(m, jnp.max(s, axis=0, keepdims=True))
        alpha = jnp.exp2(m - m_new)
        p = jnp.exp2(s - m_new).astype(BF16)
        nk = s.shape[0]
        vt = jnp.concatenate([vt_ref[off // tv + kb] for kb in range(nk // tv)], axis=1)
        pv = _dot(jnp.concatenate([vt, ones[:, :nk]], axis=0), p)
        l = alpha * l + pv[HEAD_DIM:HEAD_DIM + 1, :]
        acc = alpha * acc + pv[:HEAD_DIM, :]
        return m_new, l, acc

    def body(j, sts):
        offs = [pl.multiple_of(j * (2 * th), 2 * th) + t * th for t in range(2)]
        s = [[scores(qx, off) for qx in qs] for off in offs]
        for t, off in enumerate(offs):
            sts = tuple(update(sts[c], s[t][c], off) for c in range(nq))
        return sts

    init = (jnp.full((1, th), -jnp.inf, F32), jnp.zeros((1, th), F32), jnp.zeros((HEAD_DIM, th), F32))
    sts = lax.fori_loop(0, qi * (nq // 2), body, (init,) * nq)

    base = pl.multiple_of(qi * (nq * th), nq * th)
    h2 = th // 2
    ck = lax.broadcasted_iota(jnp.int32, (th, th), 0) // CHUNK
    cq = lax.broadcasted_iota(jnp.int32, (th, th), 1) // CHUNK
    visible = ck <= cq
    sts = list(sts)
    s = {(j, c): scores(qs[c], base + j * th) for j in range(nq) for c in range(j + 1, nq)}
    s_up = [_dot_nt(k_ref[pl.ds(base + c * th, h2), :], qs[c]) for c in range(nq)]
    s_lo = [_dot_nt(k_ref[pl.ds(base + c * th + h2, h2), :], qs[c][h2:, :]) for c in range(nq)]
    for j in range(nq):
        off = base + j * th
        st = update(sts[j], jnp.where(visible[:h2, :], s_up[j], -jnp.inf), off)
        right = update(tuple(x[:, h2:] for x in st), jnp.where(visible[h2:, h2:], s_lo[j], -jnp.inf), off + h2)
        sts[j] = tuple(jnp.concatenate([x[:, :h2], r], axis=1) for x, r in zip(st, right))
        for c in range(j + 1, nq):
            sts[c] = update(sts[c], s[j, c], off)
    for c in range(nq):
        _, l, acc = sts[c]
        o_ref[c * th:(c + 1) * th, :] = (acc / l).T.astype(o_ref.dtype)


def _attn(q, k, vt, batch, seq, th, nq):
    hw = HEADS * HEAD_DIM
    tq = nq * th
    assert nq % 2 == 0 and seq % tq == 0 and (th // 2) % vt.shape[2] == 0
    wide = lambda a: a.reshape(batch, seq, -1)
    tv = vt.shape[2]
    vt4 = vt.reshape(batch, seq // tv, hw, tv)
    out = pl.pallas_call(
        functools.partial(_attn_body, th=th, nq=nq),
        grid=(batch, HEADS, seq // tq),
        in_specs=[pl.BlockSpec((None, tq, 2 * HEAD_DIM), lambda b, h, i: (b, i, h)),
                  pl.BlockSpec((None, seq, 2 * HEAD_DIM), lambda b, h, i: (b, 0, h)),
                  pl.BlockSpec((None, seq // tv, HEAD_DIM, tv), lambda b, h, i: (b, 0, h, 0))],
        out_specs=pl.BlockSpec((None, tq, HEAD_DIM), lambda b, h, i: (b, i, h)),
        out_shape=jax.ShapeDtypeStruct((batch, seq, hw), BF16),
        compiler_params=pltpu.CompilerParams(dimension_semantics=("arbitrary", "arbitrary", "arbitrary"),
                                             vmem_limit_bytes=VMEM_LIMIT),
        name="attn",
    )(wide(q), wide(k), vt4)
    return out.reshape(batch * seq, hw)


def _mlp_body(oa_ref, at_ref, g_ref, x_ref, wa_ref, wb_ref, wo_ref, nmix_ref, n1_ref, w1_ref, w2_ref, n2_ref,
              o_ref, *, fc):
    d = x_ref.shape[1]
    ya = _dot(oa_ref[...], wa_ref[...])
    yb = _dot(at_ref[...], wb_ref[...])
    ga = _sigmoid(g_ref[:, :d].astype(F32))
    gb = _sigmoid(g_ref[:, d:].astype(F32))
    h = ga * ya + gb * yb
    x = x_ref[...] + _rms(_dot(h.astype(BF16), wo_ref[...]), nmix_ref[...])
    u = _rms(x, n1_ref[...]).astype(BF16)
    acc = None
    for c0 in range(0, w1_ref.shape[1], fc):
        hcol = jnp.maximum(_dot(u, w1_ref[:, c0:c0 + fc]), 0.0)
        part = _dot((hcol * hcol).astype(BF16), w2_ref[c0:c0 + fc, :])
        acc = part if acc is None else acc + part
    o_ref[...] = x + _rms(acc, n2_ref[...])


def _mlp(oa, at, gate, x2, weights, layer, tm):
    t, d = x2.shape
    row = lambda w: pl.BlockSpec((tm, w), lambda i: (i, 0))
    return pl.pallas_call(
        functools.partial(_mlp_body, fc=min(1024, weights[5].shape[2])),
        grid=(t // tm,),
        in_specs=[row(oa.shape[1]), row(at.shape[1]), row(2 * d), row(d)]
        + [_layer_resident(w.shape, layer) for w in weights],
        out_specs=row(d),
        out_shape=jax.ShapeDtypeStruct((t, d), F32),
        compiler_params=pltpu.CompilerParams(dimension_semantics=("arbitrary",), vmem_limit_bytes=VMEM_LIMIT),
        name="mlp",
    )(oa, at, gate, x2, *weights)


def _rope_table():
    inv_freq = ROPE_BASE ** (-jnp.arange(0, MLA_ROPE, 2, dtype=F32) / MLA_ROPE)
    half = MLA_ROPE // 2
    zeros = jnp.zeros((LANES - MLA_ROPE,), F32)
    rows = [jnp.concatenate([inv_freq, inv_freq, zeros]),
            jnp.concatenate([jnp.ones((MLA_ROPE,), F32), zeros]),
            jnp.concatenate([-jnp.ones((half,), F32), jnp.ones((half,), F32), zeros])]
    return jnp.concatenate([jnp.stack(rows), jnp.zeros((5, LANES), F32)], axis=0)


def _layer_weights(w_in, w_uq, w_ukv):
    depth = w_in.shape[0]
    hw = HEADS * HEAD_DIM
    o = 3 * hw
    qkv, z = w_in[..., :o], w_in[..., o:o + hw]
    o += hw
    a_b = w_in[..., o:o + 2 * HEADS]
    o += 2 * HEADS
    cq = w_in[..., o:o + MLA_Q_LORA]
    o += MLA_Q_LORA
    ckv = w_in[..., o:o + MLA_KV_LORA]
    o += MLA_KV_LORA
    kr = w_in[..., o:o + MLA_ROPE]
    o += MLA_ROPE
    gates = w_in[..., o:]
    half = MLA_ROPE // 2
    lane_pad = lambda w: jnp.pad(w, ((0, 0), (0, 0), (0, LANES - w.shape[-1])))
    krs = jnp.concatenate([kr[..., half:], kr[..., :half]], axis=-1)
    wbig = jnp.concatenate([qkv, z, gates], axis=-1).astype(BF16)
    wsm = jnp.concatenate([cq, ckv, lane_pad(kr), lane_pad(krs), lane_pad(a_b)], axis=-1).astype(BF16)
    uq = w_uq.reshape(depth, MLA_Q_LORA, HEADS, MLA_QK_DIM)
    nope, pe = uq[..., :HEAD_DIM], uq[..., HEAD_DIM:]
    pes = jnp.concatenate([pe[..., half:], pe[..., :half]], axis=-1)
    flat = lambda w: w.reshape(depth, w.shape[1], hw)
    padh = lambda w: flat(jnp.pad(w, ((0, 0), (0, 0), (0, 0), (0, HEAD_DIM - MLA_ROPE))))
    q_scale = MLA_QK_DIM ** -0.5 * LOG2_E
    wuq = (jnp.concatenate([flat(nope), padh(pe), padh(pes)], axis=-1) * q_scale).astype(BF16)
    ukv = w_ukv.reshape(depth, MLA_KV_LORA, HEADS, 2 * HEAD_DIM)
    wukv = jnp.concatenate([flat(ukv[..., :HEAD_DIM]), flat(ukv[..., HEAD_DIM:])], axis=-1).astype(BF16)
    return wbig, wsm, wuq, wukv


def kernel(x, positions, w_in, conv_w, A_log, dt_bias, gdn_norm_w, q_a_norm_w, w_uq, kv_a_norm_w, w_ukv, w_branch_a, w_branch_b, w_out, pre_mix_norm_w, post_mix_norm_w, pre_ffn_norm_w, post_ffn_norm_w, w_ff1, w_ff2):
    batch, seq, d = x.shape
    t = batch * seq
    depth = w_in.shape[0]
    tm_proj = min(256, t)
    tm_mlp = min(512, t)
    th = min(512, seq // 2)
    nq = 4 if seq % (4 * th) == 0 else 2
    assert t % tm_mlp == 0 and t % tm_proj == 0
    x2 = x.reshape(t, d)
    pos2 = positions.astype(F32).reshape(t, 1)
    cosv, sinv = _rope(pos2, _rope_table(), tm_mlp)
    rows = lambda v: v.reshape(depth, 1, -1)
    wbig, wsm, wuq, wukv = _layer_weights(w_in, w_uq, w_ukv)
    convw = jnp.pad(conv_w, ((0, 0), (0, 8 - CONV_WIDTH), (0, 0)))
    proj_weights = (rows(pre_mix_norm_w), wbig, wsm, convw, rows(q_a_norm_w), rows(kv_a_norm_w), wuq, wukv)
    gpar = jnp.zeros((depth, 8, LANES), F32).at[:, 0, :HEADS].set(dt_bias).at[:, 1, :HEADS].set(A_log)
    gdn_nw = rows(gdn_norm_w)
    mlp_weights = (w_branch_a.astype(BF16), w_branch_b.astype(BF16), w_out.astype(BF16), rows(post_mix_norm_w),
                   rows(pre_ffn_norm_w), w_ff1.astype(BF16), w_ff2.astype(BF16), rows(post_ffn_norm_w))
    for l in range(depth):
        qkv, z, gate, ab, q, kc, va = _inproj(x2, cosv, sinv, proj_weights, l, tm_proj, seq)
        oa = _gdn(qkv, ab, z, gpar, gdn_nw, l, batch, seq, GDN_CHUNKS_PER_STEP)
        at = _attn(q, kc, va, batch, seq, th, nq)
        x2 = _mlp(oa, at, gate, x2, mlp_weights, l, tm_mlp)
    return x2.reshape(batch, seq, d)
```

```python
import functools

import jax
import jax.numpy as jnp
import numpy as np
from jax import lax
from jax.experimental import pallas as pl
from jax.experimental.pallas import tpu as pltpu

F32 = jnp.float32
BF16 = jnp.bfloat16

LANES = 128
SUBLANES = 8
CHUNK = 64
NORM_EPS = 1e-6
HEADS = 8
HEAD_DIM = 128
CONV_WIDTH = 4
CONV_TAIL = 8
MLA_Q_LORA = 384
MLA_KV_LORA = 256
MLA_ROPE = 64
MLA_QK_DIM = HEAD_DIM + MLA_ROPE
ROPE_BASE = 10000.0
LOG2_E = 1.4426950408889634
GDN_CHUNKS_PER_STEP = 4
VMEM_LIMIT = 56 * 1024 * 1024


def _rms(x, w):
    return x * lax.rsqrt(jnp.mean(x * x, axis=-1, keepdims=True) + NORM_EPS) * w


def _sigmoid(x):
    return 0.5 * jnp.tanh(0.5 * x) + 0.5


def _silu(x):
    h = 0.5 * x
    return h * jnp.tanh(h) + h


def _dot(a, b):
    return jnp.dot(a, b, preferred_element_type=F32)


def _dot_nt(a, b):
    return lax.dot_general(a, b, (((1,), (1,)), ((), ())), preferred_element_type=F32)


def _dot_tn(a, b):
    return lax.dot_general(a, b, (((0,), (0,)), ((), ())), preferred_element_type=F32)


def _resident(shape):
    nd = len(shape)
    return pl.BlockSpec(shape, lambda *_: (0,) * nd, pipeline_mode=pl.Buffered(1))


def _layer_resident(shape, layer):
    nd = len(shape)
    return pl.BlockSpec((None,) + tuple(shape[1:]), lambda *_: (layer,) + (0,) * (nd - 1),
                        pipeline_mode=pl.Buffered(1))


def _rope_body(pos_ref, rope_ref, tab_ref):
    tab_ref[...] = jnp.cos(pos_ref[...] * rope_ref[0:1, :] + rope_ref[1:2, :])


def _rope(pos2, rope_tab, tm):
    t = pos2.shape[0]
    out = pl.BlockSpec((tm, LANES), lambda i: (i, 0))
    return pl.pallas_call(
        _rope_body,
        grid=(t // tm,),
        in_specs=[pl.BlockSpec((tm, 1), lambda i: (i, 0)), _resident(rope_tab.shape)],
        out_specs=out,
        out_shape=jax.ShapeDtypeStruct((t, LANES), F32),
        compiler_params=pltpu.CompilerParams(dimension_semantics=("arbitrary",), vmem_limit_bytes=VMEM_LIMIT),
        name="rope",
    )(pos2, rope_tab)


def _inproj_body(x_ref, rot_ref, nw_ref, wbig_ref, wsm_ref, convw_ref, qan_ref, kvn_ref, wuq_ref, wukv_ref,
                 qkv_ref, z_ref, gate_ref, ab_ref, q_ref, k_ref, vt_ref, buf_ref, *, tiles_per_seq):
    tm, d = x_ref.shape
    hw = HEADS * HEAD_DIM
    qkv_w = qkv_ref.shape[1]
    u = _rms(x_ref[...], nw_ref[...]).astype(BF16)

    @pl.when(pl.program_id(0) % tiles_per_seq == 0)
    def _():
        buf_ref[tm:tm + CONV_TAIL, :] = jnp.zeros((CONV_TAIL, qkv_w), F32)

    buf_ref[0:CONV_TAIL, :] = buf_ref[tm:tm + CONV_TAIL, :]

    sm = _dot(u, wsm_ref[...])
    cq = sm[:, :MLA_Q_LORA]
    ckv = sm[:, MLA_Q_LORA:MLA_Q_LORA + MLA_KV_LORA]
    o = MLA_Q_LORA + MLA_KV_LORA
    kr2, ab = sm[:, o:o + LANES], sm[:, o + LANES:]
    ab_ref[...] = ab

    rot = rot_ref[...]
    half_sum = lambda t: t + pltpu.roll(t, LANES // 2, axis=1)
    first_half = lax.broadcasted_iota(jnp.int32, (1, LANES), 1) < MLA_ROPE
    kpe = jnp.where(first_half, half_sum(kr2 * rot), 0.0)

    cqn = _rms(cq, qan_ref[...]).astype(BF16)
    q3 = _dot(cqn, wuq_ref[...])
    for h in range(HEADS):
        lo = h * HEAD_DIM
        qn = q3[:, lo:lo + HEAD_DIM]
        qp = half_sum(q3[:, hw + lo:hw + lo + HEAD_DIM] * rot)
        q_ref[:, 2 * lo:2 * lo + HEAD_DIM] = qn.astype(BF16)
        q_ref[:, 2 * lo + HEAD_DIM:2 * lo + 2 * HEAD_DIM] = qp.astype(BF16)

    ckvn = _rms(ckv, kvn_ref[...]).astype(BF16)
    kv = _dot(ckvn, wukv_ref[...])
    for h in range(HEADS):
        lo = h * HEAD_DIM
        k_ref[:, 2 * lo:2 * lo + HEAD_DIM] = kv[:, lo:lo + HEAD_DIM].astype(BF16)
        k_ref[:, 2 * lo + HEAD_DIM:2 * lo + 2 * HEAD_DIM] = kpe.astype(BF16)
    vt_ref[0] = kv[:, hw:].T.astype(BF16)

    sl = 2 * HEAD_DIM
    for s in range(qkv_w // sl):
        buf_ref[CONV_TAIL:CONV_TAIL + tm, s * sl:(s + 1) * sl] = _dot(u, wbig_ref[:, s * sl:(s + 1) * sl])
        zg = _dot(u, wbig_ref[:, qkv_w + s * sl:qkv_w + (s + 1) * sl]).astype(BF16)
        if s * sl < d:
            z_ref[:, s * sl:(s + 1) * sl] = zg
        else:
            gate_ref[:, s * sl - d:(s + 1) * sl - d] = zg
        for grp in range(2 * s, 2 * s + 2):
            lo = grp * HEAD_DIM
            acc = None
            for j in range(CONV_WIDTH):
                r0 = CONV_TAIL - (CONV_WIDTH - 1) + j
                term = buf_ref[r0:r0 + tm, lo:lo + HEAD_DIM] * convw_ref[j:j + 1, lo:lo + HEAD_DIM]
                acc = term if acc is None else acc + term
            y = _silu(acc)
            if grp < 2 * HEADS:
                y = y * lax.rsqrt(jnp.sum(y * y, axis=-1, keepdims=True) + NORM_EPS)
            if grp < HEADS:
                y = y * (HEAD_DIM ** -0.5)
            qkv_ref[:, lo:lo + HEAD_DIM] = y


def _inproj(x2, rot, weights, layer, tm, seq):
    t, d = x2.shape
    assert seq % tm == 0 and d == HEADS * HEAD_DIM
    qkv_w = 3 * HEADS * HEAD_DIM
    hw = HEADS * HEAD_DIM
    row = lambda w: pl.BlockSpec((tm, w), lambda i: (i, 0))
    outs = [(qkv_w, F32), (d, BF16), (2 * d, BF16), (LANES, F32), (2 * hw, BF16), (2 * hw, BF16)]
    return pl.pallas_call(
        functools.partial(_inproj_body, tiles_per_seq=seq // tm),
        grid=(t // tm,),
        in_specs=[row(d), row(LANES)] + [_layer_resident(w.shape, layer) for w in weights],
        out_specs=[row(w) for w, _ in outs] + [pl.BlockSpec((1, hw, tm), lambda i: (i, 0, 0))],
        out_shape=[jax.ShapeDtypeStruct((t, w), dt) for w, dt in outs]
        + [jax.ShapeDtypeStruct((t // tm, hw, tm), BF16)],
        scratch_shapes=[pltpu.VMEM((tm + CONV_TAIL, qkv_w), F32)],
        compiler_params=pltpu.CompilerParams(dimension_semantics=("arbitrary",), vmem_limit_bytes=VMEM_LIMIT),
        name="inproj",
    )(x2, rot, *weights)


def _gdn_body(qkv_ref, ab_ref, z_ref, gpar_ref, nw_ref, o_ref, state_ref, *, nc):
    c = CHUNK
    ts = nc * c
    hw = HEADS * HEAD_DIM

    @pl.when(pl.program_id(1) == 0)
    def _():
        state_ref[...] = jnp.zeros_like(state_ref)

    ab = ab_ref[...]
    xg = ab + gpar_ref[0:1, :]
    softplus = jnp.maximum(xg, 0.0) + jnp.log(1.0 + jnp.exp(-jnp.abs(xg)))
    g = -jnp.exp(gpar_ref[1:2, :]) * softplus
    beta_all = _sigmoid(ab)

    rt = lax.broadcasted_iota(jnp.int32, (ts, ts), 0)
    ct = lax.broadcasted_iota(jnp.int32, (ts, ts), 1)
    same = (rt // c) == (ct // c)
    tri_t = jnp.logical_and(same, rt <= ct).astype(BF16)
    g_t = g.T
    g_hi = g_t.astype(BF16)
    rest = g_t - g_hi.astype(F32)
    g_mid = rest.astype(BF16)
    g_lo = (rest - g_mid.astype(F32)).astype(BF16)
    gc_t = _dot(g_hi, tri_t) + _dot(g_mid, tri_t) + _dot(g_lo, tri_t)
    gc_row = gc_t[0:HEADS, :]
    gc_col = gc_t.T

    ri = lax.broadcasted_iota(jnp.int32, (c, 2 * c), 0)
    ci = lax.broadcasted_iota(jnp.int32, (c, 2 * c), 1) % c
    left = lax.broadcasted_iota(jnp.int32, (1, 2 * c), 1) < c
    incl = ri >= ci
    strict = ri > ci
    eye = (ri == ci).astype(F32)
    m16 = ((ri // 16) == (ci // 16)).astype(F32)
    zero_n = jnp.zeros((c, c), BF16)
    zero_w = jnp.zeros((c, HEAD_DIM), BF16)

    def bd_narrow(x):
        xb = x.astype(BF16)
        return jnp.concatenate([jnp.concatenate([xb[:, :c], zero_n], axis=1),
                                jnp.concatenate([zero_n, xb[:, c:]], axis=1)], axis=0)

    def bd_wide(x):
        xb = x.astype(BF16)
        return jnp.concatenate([jnp.concatenate([xb[:, :HEAD_DIM], zero_w], axis=1),
                                jnp.concatenate([zero_w, xb[:, HEAD_DIM:]], axis=1)], axis=0)

    def pmul(x, y):
        return _dot(x.astype(BF16), bd_narrow(y))

    def wide(col_a, col_b):
        return jnp.concatenate([jnp.broadcast_to(col_a, (c, HEAD_DIM)), jnp.broadcast_to(col_b, (c, HEAD_DIM))],
                               axis=1)

    its = [(ch, pr) for ch in range(nc) for pr in range(HEADS // 2)]
    n = range(len(its))
    rows = lambda ch: slice(ch * c, (ch + 1) * c)
    pcols = lambda base, pr: slice(base + 2 * pr * HEAD_DIM, base + (2 * pr + 2) * HEAD_DIM)
    qf = [qkv_ref[rows(ch), pcols(0, pr)] for ch, pr in its]
    kf = [qkv_ref[rows(ch), pcols(hw, pr)] for ch, pr in its]
    v = [qkv_ref[rows(ch), pcols(2 * hw, pr)] for ch, pr in its]
    gca = [gc_col[rows(ch), 2 * pr:2 * pr + 1] for ch, pr in its]
    gcb = [gc_col[rows(ch), 2 * pr + 1:2 * pr + 2] for ch, pr in its]
    gcol_n = [jnp.where(left, gca[i], gcb[i]) for i in n]
    grow_n = [jnp.concatenate([gc_row[2 * pr:2 * pr + 1, rows(ch)], gc_row[2 * pr + 1:2 * pr + 2, rows(ch)]],
                              axis=1) for ch, pr in its]
    decay = [jnp.exp(jnp.where(incl, gcol_n[i] - grow_n[i], -jnp.inf)) for i in n]
    beta_w = [wide(beta_all[rows(ch), HEADS + 2 * pr:HEADS + 2 * pr + 1],
                   beta_all[rows(ch), HEADS + 2 * pr + 1:HEADS + 2 * pr + 2]) for ch, pr in its]
    egc_w = [jnp.exp(wide(gca[i], gcb[i])) for i in n]
    genda = [t[c - 1:c, :] for t in gca]
    gendb = [t[c - 1:c, :] for t in gcb]
    kend_w = [jnp.exp(wide(genda[i] - gca[i], gendb[i] - gcb[i])) for i in n]
    kb = [kf[i] * beta_w[i] for i in n]
    kblk = [bd_wide(t) for t in kf]
    low = [jnp.where(strict, _dot_nt(kb[i].astype(BF16), kblk[i]) * decay[i], 0.0) for i in n]
    qk = [(_dot_nt(qf[i].astype(BF16), kblk[i]) * decay[i]).astype(BF16) for i in n]

    ld = [t * m16 for t in low]
    off = [low[i] - ld[i] for i in n]
    td = [eye - t for t in ld]
    p = [pmul(t, t) for t in ld]
    td = [td[i] + pmul(td[i], p[i]) for i in n]
    p = [pmul(t, t) for t in p]
    td = [td[i] + pmul(td[i], p[i]) for i in n]
    p = [pmul(t, t) for t in p]
    td = [td[i] + pmul(td[i], p[i]) for i in n]
    m = [pmul(td[i], off[i]) for i in n]
    mm = [pmul(t, t) for t in m]
    t1 = [td[i] - pmul(m[i], td[i]) for i in n]
    tinv = [(t1[i] + pmul(mm[i], t1[i])).astype(BF16) for i in n]
    u = [_dot(tinv[i], bd_wide(v[i] * beta_w[i])) for i in n]
    w = [_dot(tinv[i], bd_wide(kb[i] * egc_w[i])) for i in n]
    wq = [jnp.concatenate([w[i], qf[i] * egc_w[i]], axis=0).astype(BF16) for i in n]
    k_end = [(kf[i] * kend_w[i]).astype(BF16) for i in n]

    state = [state_ref[h] for h in range(HEADS)]
    zero_s = jnp.zeros((HEAD_DIM, HEAD_DIM), BF16)
    prs = range(HEADS // 2)
    for ch in range(nc):
        idx = [ch * (HEADS // 2) + pr for pr in prs]
        sblk = [jnp.concatenate([jnp.concatenate([state[2 * pr].astype(BF16), zero_s], axis=1),
                                 jnp.concatenate([zero_s, state[2 * pr + 1].astype(BF16)], axis=1)], axis=0)
                for pr in prs]
        ws_qs = [_dot(wq[i], sblk[pr]) for pr, i in enumerate(idx)]
        vnb = [(u[i] - ws_qs[pr][:c]).astype(BF16) for pr, i in enumerate(idx)]
        o = [ws_qs[pr][c:] + _dot(qk[i], bd_wide(vnb[pr])) for pr, i in enumerate(idx)]
        upd = [_dot_tn(k_end[i], vnb[pr]) for pr, i in enumerate(idx)]
        for pr, i in enumerate(idx):
            state[2 * pr] = state[2 * pr] * jnp.exp(genda[i]) + upd[pr][:HEAD_DIM, :HEAD_DIM]
            state[2 * pr + 1] = state[2 * pr + 1] * jnp.exp(gendb[i]) + upd[pr][HEAD_DIM:, HEAD_DIM:]
        for h in range(HEADS):
            lo = h * HEAD_DIM
            out = _rms(o[h // 2][:, (h % 2) * HEAD_DIM:(h % 2 + 1) * HEAD_DIM], nw_ref[...])
            out = out * _silu(z_ref[rows(ch), lo:lo + HEAD_DIM].astype(F32))
            o_ref[rows(ch), lo:lo + HEAD_DIM] = out.astype(o_ref.dtype)
    for h in range(HEADS):
        state_ref[h] = state[h]


def _gdn(qkv, ab, z, gpar, nw, layer, batch, seq, nc):
    t, qkv_w = qkv.shape
    hw = HEADS * HEAD_DIM
    ts = nc * CHUNK
    assert ts % LANES == 0 and seq % ts == 0
    n = seq // ts
    row = lambda w: pl.BlockSpec((ts, w), lambda b, s: (b * n + s, 0))
    return pl.pallas_call(
        functools.partial(_gdn_body, nc=nc),
        grid=(batch, n),
        in_specs=[row(qkv_w), row(LANES), row(hw), _layer_resident(gpar.shape, layer), _layer_resident(nw.shape, layer)],
        out_specs=row(hw),
        out_shape=jax.ShapeDtypeStruct((t, hw), BF16),
        scratch_shapes=[pltpu.VMEM((HEADS, HEAD_DIM, HEAD_DIM), F32)],
        compiler_params=pltpu.CompilerParams(dimension_semantics=("arbitrary", "arbitrary"),
                                             vmem_limit_bytes=VMEM_LIMIT),
        name="gdn",
    )(qkv, ab, z, gpar, nw)


def _attn_body(q_ref, k_ref, vt_ref, o_ref, *, th, nq):
    qi = pl.program_id(2)
    tv = vt_ref.shape[2]
    ones = jnp.ones((2 * SUBLANES, th), BF16)
    qs = [q_ref[c * th:(c + 1) * th, :] for c in range(nq)]

    def scores(qx, off):
        return _dot_nt(k_ref[pl.ds(off, th), :], qx)

    def update(st, s, off):
        m, l, acc = st
        m_new = jnp.maximum(m, jnp.max(s, axis=0, keepdims=True))
        alpha = jnp.exp2(m - m_new)
        p = jnp.exp2(s - m_new).astype(BF16)
        nk = s.shape[0]
        vt = jnp.concatenate([vt_ref[off // tv + kb] for kb in range(nk // tv)], axis=1)
        pv = _dot(jnp.concatenate([vt, ones[:, :nk]], axis=0), p)
        l = alpha * l + pv[HEAD_DIM:HEAD_DIM + 1, :]
        acc = alpha * acc + pv[:HEAD_DIM, :]
        return m_new, l, acc

    def body(j, sts):
        offs = [pl.multiple_of(j * (2 * th), 2 * th) + t * th for t in range(2)]
        s = [[scores(qx, off) for qx in qs] for off in offs]
        for t, off in enumerate(offs):
            sts = tuple(update(sts[c], s[t][c], off) for c in range(nq))
        return sts

    init = (jnp.full((1, th), -jnp.inf, F32), jnp.zeros((1, th), F32), jnp.zeros((HEAD_DIM, th), F32))
    sts = lax.fori_loop(0, qi * (nq // 2), body, (init,) * nq)

    base = pl.multiple_of(qi * (nq * th), nq * th)
    h2 = th // 2
    ck = lax.broadcasted_iota(jnp.int32, (th, th), 0) // CHUNK
    cq = lax.broadcasted_iota(jnp.int32, (th, th), 1) // CHUNK
    visible = ck <= cq
    sts = list(sts)
    s = {(j, c): scores(qs[c], base + j * th) for j in range(nq) for c in range(j + 1, nq)}
    s_up = [_dot_nt(k_ref[pl.ds(base + c * th, h2), :], qs[c]) for c in range(nq)]
    s_lo = [_dot_nt(k_ref[pl.ds(base + c * th + h2, h2), :], qs[c][h2:, :]) for c in range(nq)]
    for j in range(nq):
        off = base + j * th
        st = update(sts[j], jnp.where(visible[:h2, :], s_up[j], -jnp.inf), off)
        right = update(tuple(x[:, h2:] for x in st), jnp.where(visible[h2:, h2:], s_lo[j], -jnp.inf), off + h2)
        sts[j] = tuple(jnp.concatenate([x[:, :h2], r], axis=1) for x, r in zip(st, right))
        for c in range(j + 1, nq):
            sts[c] = update(sts[c], s[j, c], off)
    for c in range(nq):
        _, l, acc = sts[c]
        o_ref[c * th:(c + 1) * th, :] = (acc / l).T.astype(o_ref.dtype)


def _attn(q, k, vt, batch, seq, th, nq):
    hw = HEADS * HEAD_DIM
    tq = nq * th
    assert nq % 2 == 0 and seq % tq == 0 and (th // 2) % vt.shape[2] == 0
    wide = lambda a: a.reshape(batch, seq, -1)
    tv = vt.shape[2]
    vt4 = vt.reshape(batch, seq // tv, hw, tv)
    out = pl.pallas_call(
        functools.partial(_attn_body, th=th, nq=nq),
        grid=(batch, HEADS, seq // tq),
        in_specs=[pl.BlockSpec((None, tq, 2 * HEAD_DIM), lambda b, h, i: (b, i, h)),
                  pl.BlockSpec((None, seq, 2 * HEAD_DIM), lambda b, h, i: (b, 0, h)),
                  pl.BlockSpec((None, seq // tv, HEAD_DIM, tv), lambda b, h, i: (b, 0, h, 0))],
        out_specs=pl.BlockSpec((None, tq, HEAD_DIM), lambda b, h, i: (b, i, h)),
        out_shape=jax.ShapeDtypeStruct((batch, seq, hw), BF16),
        compiler_params=pltpu.CompilerParams(dimension_semantics=("arbitrary", "arbitrary", "arbitrary"),
                                             vmem_limit_bytes=VMEM_LIMIT),
        name="attn",
    )(wide(q), wide(k), vt4)
    return out.reshape(batch * seq, hw)


def _mlp_body(oa_ref, at_ref, g_ref, x_ref, wa_ref, wb_ref, wo_ref, nmix_ref, n1_ref, w1_ref, w2_ref, n2_ref,
              o_ref, *, fc):
    d = x_ref.shape[1]
    ya = _dot(oa_ref[...], wa_ref[...])
    yb = _dot(at_ref[...], wb_ref[...])
    ga = _sigmoid(g_ref[:, :d].astype(F32))
    gb = _sigmoid(g_ref[:, d:].astype(F32))
    h = ga * ya + gb * yb
    x = x_ref[...] + _rms(_dot(h.astype(BF16), wo_ref[...]), nmix_ref[...])
    u = _rms(x, n1_ref[...]).astype(BF16)
    acc = None
    for c0 in range(0, w1_ref.shape[1], fc):
        hcol = jnp.maximum(_dot(u, w1_ref[:, c0:c0 + fc]), 0.0)
        part = _dot((hcol * hcol).astype(BF16), w2_ref[c0:c0 + fc, :])
        acc = part if acc is None else acc + part
    o_ref[...] = x + _rms(acc, n2_ref[...])


def _mlp(oa, at, gate, x2, weights, layer, tm):
    t, d = x2.shape
    row = lambda w: pl.BlockSpec((tm, w), lambda i: (i, 0))
    return pl.pallas_call(
        functools.partial(_mlp_body, fc=min(1024, weights[5].shape[2])),
        grid=(t // tm,),
        in_specs=[row(oa.shape[1]), row(at.shape[1]), row(2 * d), row(d)]
        + [_layer_resident(w.shape, layer) for w in weights],
        out_specs=row(d),
        out_shape=jax.ShapeDtypeStruct((t, d), F32),
        compiler_params=pltpu.CompilerParams(dimension_semantics=("arbitrary",), vmem_limit_bytes=VMEM_LIMIT),
        name="mlp",
    )(oa, at, gate, x2, *weights)


def _rope_table():
    inv_freq = ROPE_BASE ** (-jnp.arange(0, MLA_ROPE, 2, dtype=F32) / MLA_ROPE)
    half = MLA_ROPE // 2
    zero, quarter_turn = jnp.zeros((MLA_ROPE,), F32), jnp.full((half,), np.pi / 2, F32)
    rows = [jnp.tile(inv_freq, LANES // half), jnp.concatenate([zero, quarter_turn, -quarter_turn])]
    return jnp.concatenate([jnp.stack(rows), jnp.zeros((6, LANES), F32)], axis=0)


def _layer_weights(w_in, w_uq, w_ukv):
    depth = w_in.shape[0]
    hw = HEADS * HEAD_DIM
    o = 3 * hw
    qkv, z = w_in[..., :o], w_in[..., o:o + hw]
    o += hw
    a_b = w_in[..., o:o + 2 * HEADS]
    o += 2 * HEADS
    cq = w_in[..., o:o + MLA_Q_LORA]
    o += MLA_Q_LORA
    ckv = w_in[..., o:o + MLA_KV_LORA]
    o += MLA_KV_LORA
    kr = w_in[..., o:o + MLA_ROPE]
    o += MLA_ROPE
    gates = w_in[..., o:]
    half = MLA_ROPE // 2
    lane_pad = lambda w: jnp.pad(w, ((0, 0), (0, 0), (0, LANES - w.shape[-1])))
    krs = jnp.concatenate([kr[..., half:], kr[..., :half]], axis=-1)
    wbig = jnp.concatenate([qkv, z, gates], axis=-1).astype(BF16)
    wsm = jnp.concatenate([cq, ckv, kr, krs, lane_pad(a_b)], axis=-1).astype(BF16)
    uq = w_uq.reshape(depth, MLA_Q_LORA, HEADS, MLA_QK_DIM)
    nope, pe = uq[..., :HEAD_DIM], uq[..., HEAD_DIM:]
    pes = jnp.concatenate([pe[..., half:], pe[..., :half]], axis=-1)
    flat = lambda w: w.reshape(depth, w.shape[1], hw)
    q_scale = MLA_QK_DIM ** -0.5 * LOG2_E
    wuq = (jnp.concatenate([flat(nope), flat(jnp.concatenate([pe, pes], axis=-1))], axis=-1) * q_scale).astype(BF16)
    ukv = w_ukv.reshape(depth, MLA_KV_LORA, HEADS, 2 * HEAD_DIM)
    wukv = jnp.concatenate([flat(ukv[..., :HEAD_DIM]), flat(ukv[..., HEAD_DIM:])], axis=-1).astype(BF16)
    return wbig, wsm, wuq, wukv


def kernel(x, positions, w_in, conv_w, A_log, dt_bias, gdn_norm_w, q_a_norm_w, w_uq, kv_a_norm_w, w_ukv, w_branch_a, w_branch_b, w_out, pre_mix_norm_w, post_mix_norm_w, pre_ffn_norm_w, post_ffn_norm_w, w_ff1, w_ff2):
    batch, seq, d = x.shape
    t = batch * seq
    depth = w_in.shape[0]
    tm_proj = min(256, t)
    tm_mlp = min(512, t)
    th = min(512, seq // 2)
    nq = 4 if seq % (4 * th) == 0 else 2
    assert t % tm_mlp == 0 and t % tm_proj == 0
    x2 = x.reshape(t, d)
    pos2 = positions.astype(F32).reshape(t, 1)
    rot = _rope(pos2, _rope_table(), tm_mlp)
    rows = lambda v: v.reshape(depth, 1, -1)
    wbig, wsm, wuq, wukv = _layer_weights(w_in, w_uq, w_ukv)
    convw = jnp.pad(conv_w, ((0, 0), (0, 8 - CONV_WIDTH), (0, 0)))
    proj_weights = (rows(pre_mix_norm_w), wbig, wsm, convw, rows(q_a_norm_w), rows(kv_a_norm_w), wuq, wukv)
    gpar = jnp.zeros((depth, 8, LANES), F32).at[:, 0, :HEADS].set(dt_bias).at[:, 1, :HEADS].set(A_log)
    gdn_nw = rows(gdn_norm_w)
    mlp_weights = (w_branch_a.astype(BF16), w_branch_b.astype(BF16), w_out.astype(BF16), rows(post_mix_norm_w),
                   rows(pre_ffn_norm_w), w_ff1.astype(BF16), w_ff2.astype(BF16), rows(post_ffn_norm_w))
    for l in range(depth):
        qkv, z, gate, ab, q, kc, va = _inproj(x2, rot, proj_weights, l, tm_proj, seq)
        oa = _gdn(qkv, ab, z, gpar, gdn_nw, l, batch, seq, GDN_CHUNKS_PER_STEP)
        at = _attn(q, kc, va, batch, seq, th, nq)
        x2 = _mlp(oa, at, gate, x2, mlp_weights, l, tm_mlp)
    return x2.reshape(batch, seq, d)
```

```python
import functools

import jax
import jax.numpy as jnp
import numpy as np
from jax import lax
from jax.experimental import pallas as pl
from jax.experimental.pallas import tpu as pltpu

F32 = jnp.float32
BF16 = jnp.bfloat16

LANES = 128
SUBLANES = 8
CHUNK = 64
NORM_EPS = 1e-6
HEADS = 8
HEAD_DIM = 128
CONV_WIDTH = 4
CONV_TAIL = 8
MLA_Q_LORA = 384
MLA_KV_LORA = 256
MLA_ROPE = 64
MLA_QK_DIM = HEAD_DIM + MLA_ROPE
ROPE_BASE = 10000.0
LOG2_E = 1.4426950408889634
GDN_CHUNKS_PER_STEP = 8
VMEM_LIMIT = 56 * 1024 * 1024


def _rms(x, w):
    return x * lax.rsqrt(jnp.mean(x * x, axis=-1, keepdims=True) + NORM_EPS) * w


def _sigmoid(x):
    return 0.5 * jnp.tanh(0.5 * x) + 0.5


def _silu(x):
    h = 0.5 * x
    return h * jnp.tanh(h) + h


def _dot(a, b):
    return jnp.dot(a, b, preferred_element_type=F32)


def _dot_nt(a, b):
    return lax.dot_general(a, b, (((1,), (1,)), ((), ())), preferred_element_type=F32)


def _dot_tn(a, b):
    return lax.dot_general(a, b, (((0,), (0,)), ((), ())), preferred_element_type=F32)


def _resident(shape):
    nd = len(shape)
    return pl.BlockSpec(shape, lambda *_: (0,) * nd, pipeline_mode=pl.Buffered(1))


def _layer_resident(shape, layer):
    nd = len(shape)
    return pl.BlockSpec((None,) + tuple(shape[1:]), lambda *_: (layer,) + (0,) * (nd - 1),
                        pipeline_mode=pl.Buffered(1))


def _rope_body(pos_ref, rope_ref, tab_ref):
    tab_ref[...] = jnp.cos(pos_ref[...] * rope_ref[0:1, :] + rope_ref[1:2, :])


def _rope(pos2, rope_tab, tm):
    t = pos2.shape[0]
    out = pl.BlockSpec((tm, LANES), lambda i: (i, 0))
    return pl.pallas_call(
        _rope_body,
        grid=(t // tm,),
        in_specs=[pl.BlockSpec((tm, 1), lambda i: (i, 0)), _resident(rope_tab.shape)],
        out_specs=out,
        out_shape=jax.ShapeDtypeStruct((t, LANES), F32),
        compiler_params=pltpu.CompilerParams(dimension_semantics=("arbitrary",), vmem_limit_bytes=VMEM_LIMIT),
        name="rope",
    )(pos2, rope_tab)


def _inproj_body(x_ref, rot_ref, nw_ref, wbig_ref, wsm_ref, convw_ref, qan_ref, kvn_ref, wuq_ref, wukv_ref,
                 qkv_ref, z_ref, gate_ref, ab_ref, q_ref, k_ref, vt_ref, buf_ref, *, tiles_per_seq):
    tm, d = x_ref.shape
    hw = HEADS * HEAD_DIM
    qkv_w = qkv_ref.shape[1]
    u = _rms(x_ref[...], nw_ref[...]).astype(BF16)

    @pl.when(pl.program_id(0) % tiles_per_seq == 0)
    def _():
        buf_ref[tm:tm + CONV_TAIL, :] = jnp.zeros((CONV_TAIL, qkv_w), F32)

    buf_ref[0:CONV_TAIL, :] = buf_ref[tm:tm + CONV_TAIL, :]

    sm = _dot(u, wsm_ref[...])
    cq = sm[:, :MLA_Q_LORA]
    ckv = sm[:, MLA_Q_LORA:MLA_Q_LORA + MLA_KV_LORA]
    o = MLA_Q_LORA + MLA_KV_LORA
    kr2, ab = sm[:, o:o + LANES], sm[:, o + LANES:]
    ab_ref[...] = ab

    rot = rot_ref[...]
    half_sum = lambda t: t + pltpu.roll(t, LANES // 2, axis=1)
    first_half = lax.broadcasted_iota(jnp.int32, (1, LANES), 1) < MLA_ROPE
    kpe = jnp.where(first_half, half_sum(kr2 * rot), 0.0)

    cqn = _rms(cq, qan_ref[...]).astype(BF16)
    q3 = _dot(cqn, wuq_ref[...])
    for h in range(HEADS):
        lo = h * HEAD_DIM
        qn = q3[:, lo:lo + HEAD_DIM]
        qp = half_sum(q3[:, hw + lo:hw + lo + HEAD_DIM] * rot)
        q_ref[:, 2 * lo:2 * lo + HEAD_DIM] = qn.astype(BF16)
        q_ref[:, 2 * lo + HEAD_DIM:2 * lo + 2 * HEAD_DIM] = qp.astype(BF16)

    ckvn = _rms(ckv, kvn_ref[...]).astype(BF16)
    kv = _dot(ckvn, wukv_ref[...])
    for h in range(HEADS):
        lo = h * HEAD_DIM
        k_ref[:, 2 * lo:2 * lo + HEAD_DIM] = kv[:, lo:lo + HEAD_DIM].astype(BF16)
        k_ref[:, 2 * lo + HEAD_DIM:2 * lo + 2 * HEAD_DIM] = kpe.astype(BF16)
    vt_ref[0] = kv[:, hw:].T.astype(BF16)

    sl = 2 * HEAD_DIM
    for s in range(qkv_w // sl):
        buf_ref[CONV_TAIL:CONV_TAIL + tm, s * sl:(s + 1) * sl] = _dot(u, wbig_ref[:, s * sl:(s + 1) * sl])
        zg = _dot(u, wbig_ref[:, qkv_w + s * sl:qkv_w + (s + 1) * sl]).astype(BF16)
        if s * sl < d:
            z_ref[:, s * sl:(s + 1) * sl] = zg
        else:
            gate_ref[:, s * sl - d:(s + 1) * sl - d] = zg
        for grp in range(2 * s, 2 * s + 2):
            lo = grp * HEAD_DIM
            acc = None
            for j in range(CONV_WIDTH):
                r0 = CONV_TAIL - (CONV_WIDTH - 1) + j
                term = buf_ref[r0:r0 + tm, lo:lo + HEAD_DIM] * convw_ref[j:j + 1, lo:lo + HEAD_DIM]
                acc = term if acc is None else acc + term
            y = _silu(acc)
            if grp < 2 * HEADS:
                y = y * lax.rsqrt(jnp.sum(y * y, axis=-1, keepdims=True) + NORM_EPS)
            if grp < HEADS:
                y = y * (HEAD_DIM ** -0.5)
            qkv_ref[:, lo:lo + HEAD_DIM] = y


def _inproj(x2, rot, weights, layer, tm, seq):
    t, d = x2.shape
    assert seq % tm == 0 and d == HEADS * HEAD_DIM
    qkv_w = 3 * HEADS * HEAD_DIM
    hw = HEADS * HEAD_DIM
    row = lambda w: pl.BlockSpec((tm, w), lambda i: (i, 0))
    outs = [(qkv_w, F32), (d, BF16), (2 * d, BF16), (LANES, F32), (2 * hw, BF16), (2 * hw, BF16)]
    return pl.pallas_call(
        functools.partial(_inproj_body, tiles_per_seq=seq // tm),
        grid=(t // tm,),
        in_specs=[row(d), row(LANES)] + [_layer_resident(w.shape, layer) for w in weights],
        out_specs=[row(w) for w, _ in outs] + [pl.BlockSpec((1, hw, tm), lambda i: (i, 0, 0))],
        out_shape=[jax.ShapeDtypeStruct((t, w), dt) for w, dt in outs]
        + [jax.ShapeDtypeStruct((t // tm, hw, tm), BF16)],
        scratch_shapes=[pltpu.VMEM((tm + CONV_TAIL, qkv_w), F32)],
        compiler_params=pltpu.CompilerParams(dimension_semantics=("arbitrary",), vmem_limit_bytes=VMEM_LIMIT),
        name="inproj",
    )(x2, rot, *weights)


def _gdn_body(qkv_ref, ab_ref, z_ref, gpar_ref, nw_ref, o_ref, state_ref, *, nc):
    c = CHUNK
    ts = nc * c
    hw = HEADS * HEAD_DIM

    @pl.when(pl.program_id(1) == 0)
    def _():
        state_ref[...] = jnp.zeros_like(state_ref)

    ab = ab_ref[...]
    xg = ab + gpar_ref[0:1, :]
    softplus = jnp.maximum(xg, 0.0) + jnp.log(1.0 + jnp.exp(-jnp.abs(xg)))
    g = -jnp.exp(gpar_ref[1:2, :]) * softplus
    beta_all = _sigmoid(ab)

    rt = lax.broadcasted_iota(jnp.int32, (ts, ts), 0)
    ct = lax.broadcasted_iota(jnp.int32, (ts, ts), 1)
    same = (rt // c) == (ct // c)
    tri_t = jnp.logical_and(same, rt <= ct).astype(BF16)
    g_t = g.T
    g_hi = g_t.astype(BF16)
    rest = g_t - g_hi.astype(F32)
    g_mid = rest.astype(BF16)
    g_lo = (rest - g_mid.astype(F32)).astype(BF16)
    gc_t = _dot(g_hi, tri_t) + _dot(g_mid, tri_t) + _dot(g_lo, tri_t)
    gc_row = gc_t[0:HEADS, :]
    gc_col = gc_t.T

    ri = lax.broadcasted_iota(jnp.int32, (c, 2 * c), 0)
    ci = lax.broadcasted_iota(jnp.int32, (c, 2 * c), 1) % c
    left = lax.broadcasted_iota(jnp.int32, (1, 2 * c), 1) < c
    incl = ri >= ci
    strict = ri > ci
    eye = (ri == ci).astype(F32)
    m16 = ((ri // 16) == (ci // 16)).astype(F32)
    zero_n = jnp.zeros((c, c), BF16)
    zero_w = jnp.zeros((c, HEAD_DIM), BF16)

    def bd_narrow(x):
        xb = x.astype(BF16)
        return jnp.concatenate([jnp.concatenate([xb[:, :c], zero_n], axis=1),
                                jnp.concatenate([zero_n, xb[:, c:]], axis=1)], axis=0)

    def bd_wide(x):
        xb = x.astype(BF16)
        return jnp.concatenate([jnp.concatenate([xb[:, :HEAD_DIM], zero_w], axis=1),
                                jnp.concatenate([zero_w, xb[:, HEAD_DIM:]], axis=1)], axis=0)

    def pmul(x, y):
        return _dot(x.astype(BF16), bd_narrow(y))

    def wide(col_a, col_b):
        return jnp.concatenate([jnp.broadcast_to(col_a, (c, HEAD_DIM)), jnp.broadcast_to(col_b, (c, HEAD_DIM))],
                               axis=1)

    its = [(ch, pr) for ch in range(nc) for pr in range(HEADS // 2)]
    n = range(len(its))
    rows = lambda ch: slice(ch * c, (ch + 1) * c)
    pcols = lambda base, pr: slice(base + 2 * pr * HEAD_DIM, base + (2 * pr + 2) * HEAD_DIM)
    qf = [qkv_ref[rows(ch), pcols(0, pr)] for ch, pr in its]
    kf = [qkv_ref[rows(ch), pcols(hw, pr)] for ch, pr in its]
    v = [qkv_ref[rows(ch), pcols(2 * hw, pr)] for ch, pr in its]
    gca = [gc_col[rows(ch), 2 * pr:2 * pr + 1] for ch, pr in its]
    gcb = [gc_col[rows(ch), 2 * pr + 1:2 * pr + 2] for ch, pr in its]
    gcol_n = [jnp.where(left, gca[i], gcb[i]) for i in n]
    grow_n = [jnp.concatenate([gc_row[2 * pr:2 * pr + 1, rows(ch)], gc_row[2 * pr + 1:2 * pr + 2, rows(ch)]],
                              axis=1) for ch, pr in its]
    decay = [jnp.exp(jnp.where(incl, gcol_n[i] - grow_n[i], -jnp.inf)) for i in n]
    beta_w = [wide(beta_all[rows(ch), HEADS + 2 * pr:HEADS + 2 * pr + 1],
                   beta_all[rows(ch), HEADS + 2 * pr + 1:HEADS + 2 * pr + 2]) for ch, pr in its]
    egc_w = [jnp.exp(wide(gca[i], gcb[i])) for i in n]
    genda = [t[c - 1:c, :] for t in gca]
    gendb = [t[c - 1:c, :] for t in gcb]
    kend_w = [jnp.exp(wide(genda[i] - gca[i], gendb[i] - gcb[i])) for i in n]
    kb = [kf[i] * beta_w[i] for i in n]
    kblk = [bd_wide(t) for t in kf]
    low = [jnp.where(strict, _dot_nt(kb[i].astype(BF16), kblk[i]) * decay[i], 0.0) for i in n]
    qk = [(_dot_nt(qf[i].astype(BF16), kblk[i]) * decay[i]).astype(BF16) for i in n]

    ld = [t * m16 for t in low]
    off = [low[i] - ld[i] for i in n]
    td = [eye - t for t in ld]
    p = [pmul(t, t) for t in ld]
    td = [td[i] + pmul(td[i], p[i]) for i in n]
    p = [pmul(t, t) for t in p]
    td = [td[i] + pmul(td[i], p[i]) for i in n]
    p = [pmul(t, t) for t in p]
    td = [td[i] + pmul(td[i], p[i]) for i in n]
    m = [pmul(td[i], off[i]) for i in n]
    mm = [pmul(t, t) for t in m]
    t1 = [td[i] - pmul(m[i], td[i]) for i in n]
    tinv = [(t1[i] + pmul(mm[i], t1[i])).astype(BF16) for i in n]
    u = [_dot(tinv[i], bd_wide(v[i] * beta_w[i])) for i in n]
    w = [_dot(tinv[i], bd_wide(kb[i] * egc_w[i])) for i in n]
    wq = [jnp.concatenate([w[i], qf[i] * egc_w[i]], axis=0).astype(BF16) for i in n]
    k_end = [(kf[i] * kend_w[i]).astype(BF16) for i in n]

    state = [state_ref[h] for h in range(HEADS)]
    zero_s = jnp.zeros((HEAD_DIM, HEAD_DIM), BF16)
    prs = range(HEADS // 2)
    for ch in range(nc):
        idx = [ch * (HEADS // 2) + pr for pr in prs]
        sblk = [jnp.concatenate([jnp.concatenate([state[2 * pr].astype(BF16), zero_s], axis=1),
                                 jnp.concatenate([zero_s, state[2 * pr + 1].astype(BF16)], axis=1)], axis=0)
                for pr in prs]
        ws_qs = [_dot(wq[i], sblk[pr]) for pr, i in enumerate(idx)]
        vnb = [(u[i] - ws_qs[pr][:c]).astype(BF16) for pr, i in enumerate(idx)]
        o = [ws_qs[pr][c:] + _dot(qk[i], bd_wide(vnb[pr])) for pr, i in enumerate(idx)]
        upd = [_dot_tn(k_end[i], vnb[pr]) for pr, i in enumerate(idx)]
        for pr, i in enumerate(idx):
            state[2 * pr] = state[2 * pr] * jnp.exp(genda[i]) + upd[pr][:HEAD_DIM, :HEAD_DIM]
            state[2 * pr + 1] = state[2 * pr + 1] * jnp.exp(gendb[i]) + upd[pr][HEAD_DIM:, HEAD_DIM:]
        for h in range(HEADS):
            lo = h * HEAD_DIM
            out = _rms(o[h // 2][:, (h % 2) * HEAD_DIM:(h % 2 + 1) * HEAD_DIM], nw_ref[...])
            out = out * _silu(z_ref[rows(ch), lo:lo + HEAD_DIM].astype(F32))
            o_ref[rows(ch), lo:lo + HEAD_DIM] = out.astype(o_ref.dtype)
    for h in range(HEADS):
        state_ref[h] = state[h]


def _gdn(qkv, ab, z, gpar, nw, layer, batch, seq, nc):
    t, qkv_w = qkv.shape
    hw = HEADS * HEAD_DIM
    ts = nc * CHUNK
    assert ts % LANES == 0 and seq % ts == 0
    n = seq // ts
    row = lambda w: pl.BlockSpec((ts, w), lambda b, s: (b * n + s, 0))
    return pl.pallas_call(
        functools.partial(_gdn_body, nc=nc),
        grid=(batch, n),
        in_specs=[row(qkv_w), row(LANES), row(hw), _layer_resident(gpar.shape, layer), _layer_resident(nw.shape, layer)],
        out_specs=row(hw),
        out_shape=jax.ShapeDtypeStruct((t, hw), BF16),
        scratch_shapes=[pltpu.VMEM((HEADS, HEAD_DIM, HEAD_DIM), F32)],
        compiler_params=pltpu.CompilerParams(dimension_semantics=("arbitrary", "arbitrary"),
                                             vmem_limit_bytes=VMEM_LIMIT),
        name="gdn",
    )(qkv, ab, z, gpar, nw)


def _attn_body(q_ref, k_ref, vt_ref, o_ref, *, th, nq):
    qi = pl.program_id(2)
    tv = vt_ref.shape[2]
    ones = jnp.ones((2 * SUBLANES, th), BF16)
    qs = [q_ref[c * th:(c + 1) * th, :] for c in range(nq)]

    def scores(qx, off):
        return _dot_nt(k_ref[pl.ds(off, th), :], qx)

    def update(st, s, off):
        m, l, acc = st
        m_new = jnp.maximum(m, jnp.max(s, axis=0, keepdims=True))
        alpha = jnp.exp2(m - m_new)
        p = jnp.exp2(s - m_new).astype(BF16)
        nk = s.shape[0]
        vt = jnp.concatenate([vt_ref[off // tv + kb] for kb in range(nk // tv)], axis=1)
        pv = _dot(jnp.concatenate([vt, ones[:, :nk]], axis=0), p)
        l = alpha * l + pv[HEAD_DIM:HEAD_DIM + 1, :]
        acc = alpha * acc + pv[:HEAD_DIM, :]
        return m_new, l, acc

    def body(j, sts):
        offs = [pl.multiple_of(j * (2 * th), 2 * th) + t * th for t in range(2)]
        s = [[scores(qx, off) for qx in qs] for off in offs]
        for t, off in enumerate(offs):
            sts = tuple(update(sts[c], s[t][c], off) for c in range(nq))
        return sts

    init = (jnp.full((1, th), -jnp.inf, F32), jnp.zeros((1, th), F32), jnp.zeros((HEAD_DIM, th), F32))
    sts = lax.fori_loop(0, qi * (nq // 2), body, (init,) * nq)

    base = pl.multiple_of(qi * (nq * th), nq * th)
    h2 = th // 2
    ck = lax.broadcasted_iota(jnp.int32, (th, th), 0) // CHUNK
    cq = lax.broadcasted_iota(jnp.int32, (th, th), 1) // CHUNK
    visible = ck <= cq
    sts = list(sts)
    s = {(j, c): scores(qs[c], base + j * th) for j in range(nq) for c in range(j + 1, nq)}
    s_up = [_dot_nt(k_ref[pl.ds(base + c * th, h2), :], qs[c]) for c in range(nq)]
    s_lo = [_dot_nt(k_ref[pl.ds(base + c * th + h2, h2), :], qs[c][h2:, :]) for c in range(nq)]
    for j in range(nq):
        off = base + j * th
        st = update(sts[j], jnp.where(visible[:h2, :], s_up[j], -jnp.inf), off)
        right = update(tuple(x[:, h2:] for x in st), jnp.where(visible[h2:, h2:], s_lo[j], -jnp.inf), off + h2)
        sts[j] = tuple(jnp.concatenate([x[:, :h2], r], axis=1) for x, r in zip(st, right))
        for c in range(j + 1, nq):
            sts[c] = update(sts[c], s[j, c], off)
    for c in range(nq):
        _, l, acc = sts[c]
        o_ref[c * th:(c + 1) * th, :] = (acc / l).T.astype(o_ref.dtype)


def _attn(q, k, vt, batch, seq, th, nq):
    hw = HEADS * HEAD_DIM
    tq = nq * th
    assert nq % 2 == 0 and seq % tq == 0 and (th // 2) % vt.shape[2] == 0
    wide = lambda a: a.reshape(batch, seq, -1)
    tv = vt.shape[2]
    vt4 = vt.reshape(batch, seq // tv, hw, tv)
    out = pl.pallas_call(
        functools.partial(_attn_body, th=th, nq=nq),
        grid=(batch, HEADS, seq // tq),
        in_specs=[pl.BlockSpec((None, tq, 2 * HEAD_DIM), lambda b, h, i: (b, i, h)),
                  pl.BlockSpec((None, seq, 2 * HEAD_DIM), lambda b, h, i: (b, 0, h)),
                  pl.BlockSpec((None, seq // tv, HEAD_DIM, tv), lambda b, h, i: (b, 0, h, 0))],
        out_specs=pl.BlockSpec((None, tq, HEAD_DIM), lambda b, h, i: (b, i, h)),
        out_shape=jax.ShapeDtypeStruct((batch, seq, hw), BF16),
        compiler_params=pltpu.CompilerParams(dimension_semantics=("arbitrary", "arbitrary", "arbitrary"),
                                             vmem_limit_bytes=VMEM_LIMIT),
        name="attn",
    )(wide(q), wide(k), vt4)
    return out.reshape(batch * seq, hw)


def _mlp_body(oa_ref, at_ref, g_ref, x_ref, wa_ref, wb_ref, wo_ref, nmix_ref, n1_ref, w1_ref, w2_ref, n2_ref,
              o_ref, *, fc):
    d = x_ref.shape[1]
    ya = _dot(oa_ref[...], wa_ref[...])
    yb = _dot(at_ref[...], wb_ref[...])
    ga = _sigmoid(g_ref[:, :d].astype(F32))
    gb = _sigmoid(g_ref[:, d:].astype(F32))
    h = ga * ya + gb * yb
    x = x_ref[...] + _rms(_dot(h.astype(BF16), wo_ref[...]), nmix_ref[...])
    u = _rms(x, n1_ref[...]).astype(BF16)
    acc = None
    for c0 in range(0, w1_ref.shape[1], fc):
        hcol = jnp.maximum(_dot(u, w1_ref[:, c0:c0 + fc]), 0.0)
        part = _dot((hcol * hcol).astype(BF16), w2_ref[c0:c0 + fc, :])
        acc = part if acc is None else acc + part
    o_ref[...] = x + _rms(acc, n2_ref[...])


def _mlp(oa, at, gate, x2, weights, layer, tm):
    t, d = x2.shape
    row = lambda w: pl.BlockSpec((tm, w), lambda i: (i, 0))
    return pl.pallas_call(
        functools.partial(_mlp_body, fc=min(1024, weights[5].shape[2])),
        grid=(t // tm,),
        in_specs=[row(oa.shape[1]), row(at.shape[1]), row(2 * d), row(d)]
        + [_layer_resident(w.shape, layer) for w in weights],
        out_specs=row(d),
        out_shape=jax.ShapeDtypeStruct((t, d), F32),
        compiler_params=pltpu.CompilerParams(dimension_semantics=("arbitrary",), vmem_limit_bytes=VMEM_LIMIT),
        name="mlp",
    )(oa, at, gate, x2, *weights)


def _rope_table():
    inv_freq = ROPE_BASE ** (-jnp.arange(0, MLA_ROPE, 2, dtype=F32) / MLA_ROPE)
    half = MLA_ROPE // 2
    zero, quarter_turn = jnp.zeros((MLA_ROPE,), F32), jnp.full((half,), np.pi / 2, F32)
    rows = [jnp.tile(inv_freq, LANES // half), jnp.concatenate([zero, quarter_turn, -quarter_turn])]
    return jnp.concatenate([jnp.stack(rows), jnp.zeros((6, LANES), F32)], axis=0)


def _layer_weights(w_in, w_uq, w_ukv):
    depth = w_in.shape[0]
    hw = HEADS * HEAD_DIM
    o = 3 * hw
    qkv, z = w_in[..., :o], w_in[..., o:o + hw]
    o += hw
    a_b = w_in[..., o:o + 2 * HEADS]
    o += 2 * HEADS
    cq = w_in[..., o:o + MLA_Q_LORA]
    o += MLA_Q_LORA
    ckv = w_in[..., o:o + MLA_KV_LORA]
    o += MLA_KV_LORA
    kr = w_in[..., o:o + MLA_ROPE]
    o += MLA_ROPE
    gates = w_in[..., o:]
    half = MLA_ROPE // 2
    lane_pad = lambda w: jnp.pad(w, ((0, 0), (0, 0), (0, LANES - w.shape[-1])))
    krs = jnp.concatenate([kr[..., half:], kr[..., :half]], axis=-1)
    wbig = jnp.concatenate([qkv, z, gates], axis=-1).astype(BF16)
    wsm = jnp.concatenate([cq, ckv, kr, krs, lane_pad(a_b)], axis=-1).astype(BF16)
    uq = w_uq.reshape(depth, MLA_Q_LORA, HEADS, MLA_QK_DIM)
    nope, pe = uq[..., :HEAD_DIM], uq[..., HEAD_DIM:]
    pes = jnp.concatenate([pe[..., half:], pe[..., :half]], axis=-1)
    flat = lambda w: w.reshape(depth, w.shape[1], hw)
    q_scale = MLA_QK_DIM ** -0.5 * LOG2_E
    wuq = (jnp.concatenate([flat(nope), flat(jnp.concatenate([pe, pes], axis=-1))], axis=-1) * q_scale).astype(BF16)
    ukv = w_ukv.reshape(depth, MLA_KV_LORA, HEADS, 2 * HEAD_DIM)
    wukv = jnp.concatenate([flat(ukv[..., :HEAD_DIM]), flat(ukv[..., HEAD_DIM:])], axis=-1).astype(BF16)
    return wbig, wsm, wuq, wukv


def kernel(x, positions, w_in, conv_w, A_log, dt_bias, gdn_norm_w, q_a_norm_w, w_uq, kv_a_norm_w, w_ukv, w_branch_a, w_branch_b, w_out, pre_mix_norm_w, post_mix_norm_w, pre_ffn_norm_w, post_ffn_norm_w, w_ff1, w_ff2):
    batch, seq, d = x.shape
    t = batch * seq
    depth = w_in.shape[0]
    tm_proj = min(256, t)
    tm_mlp = min(512, t)
    th = min(512, seq // 2)
    nq = 4 if seq % (4 * th) == 0 else 2
    assert t % tm_mlp == 0 and t % tm_proj == 0
    x2 = x.reshape(t, d)
    pos2 = positions.astype(F32).reshape(t, 1)
    rot = _rope(pos2, _rope_table(), tm_mlp)
    rows = lambda v: v.reshape(depth, 1, -1)
    wbig, wsm, wuq, wukv = _layer_weights(w_in, w_uq, w_ukv)
    convw = jnp.pad(conv_w, ((0, 0), (0, 8 - CONV_WIDTH), (0, 0)))
    proj_weights = (rows(pre_mix_norm_w), wbig, wsm, convw, rows(q_a_norm_w), rows(kv_a_norm_w), wuq, wukv)
    gpar = jnp.zeros((depth, 8, LANES), F32).at[:, 0, :HEADS].set(dt_bias).at[:, 1, :HEADS].set(A_log)
    gdn_nw = rows(gdn_norm_w)
    mlp_weights = (w_branch_a.astype(BF16), w_branch_b.astype(BF16), w_out.astype(BF16), rows(post_mix_norm_w),
                   rows(pre_ffn_norm_w), w_ff1.astype(BF16), w_ff2.astype(BF16), rows(post_ffn_norm_w))
    for l in range(depth):
        qkv, z, gate, ab, q, kc, va = _inproj(x2, rot, proj_weights, l, tm_proj, seq)
        oa = _gdn(qkv, ab, z, gpar, gdn_nw, l, batch, seq, GDN_CHUNKS_PER_STEP)
        at = _attn(q, kc, va, batch, seq, th, nq)
        x2 = _mlp(oa, at, gate, x2, mlp_weights, l, tm_mlp)
    return x2.reshape(batch, seq, d)
```
